```python
import math
import jax, jax.numpy as jnp
from jax import lax
import numpy as np

D_MODEL = 1024
BATCH = 4
SEQ = 8192
DEPTH = 4
DEC_BATCH = 4
DEC_SEQ = 4096
PAST_LEN = 128

N_MIXERS = 2
N_LAYERS_A = (DEPTH + 1) // 2
N_LAYERS_B = DEPTH // 2

DA_HEADS = 8
DA_HEAD_DIM = 64
DA_V_DIM = 2 * DA_HEAD_DIM
DA_QK_WIDTH = DA_HEADS * DA_HEAD_DIM
DA_OUT_WIDTH = DA_HEADS * DA_V_DIM
DA_IN_WIDTH = 4 * DA_QK_WIDTH + DA_OUT_WIDTH
Q_BLOCK = 128

DIL_PATTERNS = ((128, 1), (512, 4), (2048, 16))
DIL_GROUPS = len(DIL_PATTERNS)
DIL_HEADS = 16
DIL_HEAD_DIM = 64
DIL_GROUP_WIDTH = DIL_HEADS * DIL_HEAD_DIM
DIL_IN_WIDTH = DIL_GROUPS * 3 * DIL_GROUP_WIDTH

ROPE_THETA = 500000.0
ROPE_FRACTION_DIV = 4

LN_EPS = 1e-5
DEEPNORM_ALPHA = (2.0 * DEPTH) ** 0.25
DEEPNORM_BETA = (8.0 * DEPTH) ** -0.25

MOE_GROUPS = 4
MOE_EXPERTS = 4
MOE_TOP_K = 2
MOE_FF = 512

NEG_INF = -1e30

kernel_name = 'hybrid_diff_dilated_hmoe_encoder'


def layer_norm(x, g, b):
    xf = x.astype(jnp.float32)
    mu = jnp.mean(xf, axis=-1, keepdims=True)
    var = jnp.mean(jnp.square(xf - mu), axis=-1, keepdims=True)
    y = (xf - mu) * lax.rsqrt(var + LN_EPS) * g.astype(jnp.float32) + b.astype(jnp.float32)
    return y.astype(x.dtype)


def partial_rope(x):
    S, dh = x.shape[1], x.shape[-1]
    rot = dh // ROPE_FRACTION_DIV
    half = rot // 2
    inv = ROPE_THETA ** (-jnp.arange(half, dtype=jnp.float32) * (2.0 / rot))
    ang = jnp.arange(S, dtype=jnp.float32)[:, None] * inv[None, :]
    cos = jnp.cos(ang)[None, :, None, :]
    sin = jnp.sin(ang)[None, :, None, :]
    xr = x[..., :rot].astype(jnp.float32)
    x1, x2 = xr[..., :half], xr[..., half:]
    r = jnp.concatenate([x1 * cos - x2 * sin, x2 * cos + x1 * sin], axis=-1).astype(x.dtype)
    return jnp.concatenate([r, x[..., rot:]], axis=-1)


def diff_attention(x, w_in, w_out, lam_q1, lam_k1, lam_q2, lam_k2, subln_g, lambda_init):
    B, S, _ = x.shape
    H, dh = DA_HEADS, DA_HEAD_DIM
    nblk = S // Q_BLOCK
    proj = x @ w_in
    W = DA_QK_WIDTH
    q1, q2, k1, k2, v = jnp.split(proj, [W, 2 * W, 3 * W, 4 * W], axis=-1)
    hd = lambda t: partial_rope(t.reshape(B, S, H, dh))
    q = jnp.stack([hd(q1), hd(q2)], axis=0)
    k = jnp.stack([hd(k1), hd(k2)], axis=0).transpose(0, 1, 3, 2, 4)
    v = v.reshape(B, S, H, DA_V_DIM).transpose(0, 2, 1, 3)
    lam = (jnp.exp(jnp.sum(lam_q1.astype(jnp.float32) * lam_k1.astype(jnp.float32)))
           - jnp.exp(jnp.sum(lam_q2.astype(jnp.float32) * lam_k2.astype(jnp.float32)))
           + lambda_init)
    scale = dh ** -0.5
    qb = q.reshape(2, B, nblk, Q_BLOCK, H, dh).transpose(2, 0, 1, 4, 3, 5)

    def block(qblk):
        s = jnp.einsum('mbhqd,mbhkd->mbhqk', qblk, k).astype(jnp.float32) * scale
        p = jax.nn.softmax(s, axis=-1)
        a = p[0] - lam * p[1]
        return jnp.einsum('bhqk,bhkd->bhqd', a.astype(v.dtype), v)

    o = lax.map(block, qb)
    o = o.transpose(1, 0, 3, 2, 4).reshape(B, S, H, DA_V_DIM)
    of = o.astype(jnp.float32)
    of = of * lax.rsqrt(jnp.mean(jnp.square(of), axis=-1, keepdims=True) + LN_EPS)
    of = of * subln_g.astype(jnp.float32) * (1.0 - lambda_init)
    return of.astype(x.dtype).reshape(B, S, DA_OUT_WIDTH) @ w_out


def dilated_group(q, k, v, dilation, half):
    B, S, H, dh = q.shape
    L = S // dilation
    nb = -(-L // half)
    Lp = nb * half

    def to_sub(t, extra):
        t = t.reshape(B, L, dilation, H, dh).transpose(0, 2, 1, 3, 4)
        return jnp.pad(t, ((0, 0), (0, 0), (extra, Lp - L + extra), (0, 0), (0, 0)))

    def neighbours(t):
        return jnp.concatenate([t[:, :, :-2], t[:, :, 1:-1], t[:, :, 2:]], axis=3)

    qs = to_sub(q, 0).reshape(B, dilation, nb, half, H, dh)
    kw = neighbours(to_sub(k, half).reshape(B, dilation, nb + 2, half, H, dh))
    vw = neighbours(to_sub(v, half).reshape(B, dilation, nb + 2, half, H, dh))
    s = jnp.einsum('brnqhd,brnkhd->brnhqk', qs, kw).astype(jnp.float32) * (dh ** -0.5)
    qi = jnp.arange(half)[:, None]
    kj = jnp.arange(3 * half)[None, :]
    rel = kj - half - qi
    kpos = (jnp.arange(nb)[:, None, None] - 1) * half + kj[None]
    valid = (jnp.abs(rel) <= half)[None] & (kpos >= 0) & (kpos < L)
    s = jnp.where(valid[None, None, :, None], s, NEG_INF)
    lse = jax.nn.logsumexp(s, axis=-1)
    p = jnp.exp(s - lse[..., None])
    o = jnp.einsum('brnhqk,brnkhd->brnqhd', p.astype(v.dtype), vw)
    o = o.reshape(B, dilation, Lp, H, dh)[:, :, :L].transpose(0, 2, 1, 3, 4).reshape(B, S, H, dh)
    lse = lse.transpose(0, 1, 2, 4, 3).reshape(B, dilation, Lp, H)[:, :, :L]
    lse = lse.transpose(0, 2, 1, 3).reshape(B, S, H)
    return o, lse


def dilated_attention(x, w_in, w_out):
    B, S, _ = x.shape
    G, H, dh = DIL_GROUPS, DIL_HEADS, DIL_HEAD_DIM
    proj = (x @ w_in).reshape(B, S, G, 3, H, dh)
    q = partial_rope(proj[:, :, :, 0].reshape(B, S, G * H, dh)).reshape(B, S, G, H, dh)
    k = partial_rope(proj[:, :, :, 1].reshape(B, S, G * H, dh)).reshape(B, S, G, H, dh)
    v = proj[:, :, :, 2]
    outs, lses = [], []
    for g, (window, dilation) in enumerate(DIL_PATTERNS):
        half = window // (2 * dilation)
        o, l = dilated_group(q[:, :, g], k[:, :, g], v[:, :, g], dilation, half)
        outs.append(o)
        lses.append(l)
    wts = jax.nn.softmax(jnp.stack(lses, axis=0), axis=0)
    o = jnp.sum(wts[..., None].astype(v.dtype) * jnp.stack(outs, axis=0), axis=0)
    return o.reshape(B, S, DIL_GROUP_WIDTH) @ w_out


def hier_moe(x, w_router_group, w_router_expert, w_gate, w_up, w_down):
    B, S, D = x.shape
    t = x.reshape(B * S, D)
    gp = jax.nn.softmax((t @ w_router_group).astype(jnp.float32), axis=-1)
    gw, gsel = lax.top_k(gp, 1)
    gmask = jax.nn.one_hot(gsel[:, 0], MOE_GROUPS, dtype=jnp.float32)
    el = jnp.einsum('td,dge->tge', t, w_router_expert).astype(jnp.float32)
    el_sel = jnp.einsum('tge,tg->te', el, gmask)
    topv, topi = lax.top_k(el_sel, MOE_TOP_K)
    tw = jax.nn.softmax(topv, axis=-1) * gw
    ew = jnp.sum(jax.nn.one_hot(topi, MOE_EXPERTS, dtype=jnp.float32) * tw[..., None], axis=1)
    combine = gmask[:, :, None] * ew[:, None, :]
    y = jnp.zeros_like(t)
    for g in range(MOE_GROUPS):
        h = jax.nn.silu(jnp.einsum('td,edf->tef', t, w_gate[g])) * jnp.einsum('td,edf->tef', t, w_up[g])
        h = h * combine[:, g, :, None].astype(h.dtype)
        y = y + jnp.einsum('tef,efd->td', h, w_down[g])
    return y.reshape(B, S, D)


def trunk(x, da_w_in, da_w_out, da_lambda_q1, da_lambda_k1, da_lambda_q2, da_lambda_k2, da_subln_g,
          dl_w_in, dl_w_out, ln1_g, ln1_b, ln2_g, ln2_b,
          moe_router_group, moe_router_expert, moe_w_gate, moe_w_up, moe_w_down):
    for i in range(DEPTH):
        j = i // N_MIXERS
        if i % N_MIXERS == 0:
            lambda_init = 0.8 - 0.6 * math.exp(-0.3 * i)
            h = diff_attention(x, da_w_in[j], da_w_out[j], da_lambda_q1[j], da_lambda_k1[j],
                               da_lambda_q2[j], da_lambda_k2[j], da_subln_g[j], lambda_init)
        else:
            h = dilated_attention(x, dl_w_in[j], dl_w_out[j])
        x = layer_norm(DEEPNORM_ALPHA * x + h, ln1_g[i], ln1_b[i])
        f = hier_moe(x, moe_router_group[i], moe_router_expert[i], moe_w_gate[i], moe_w_up[i], moe_w_down[i])
        x = layer_norm(DEEPNORM_ALPHA * x + f, ln2_g[i], ln2_b[i])
    return x


def setup_inputs(seed: int = 0) -> dict:
    key = jax.random.key(seed)
    ks = jax.random.split(key, 24)
    nrm = lambda k, shape: jax.random.normal(k, shape, dtype=jnp.float32)
    D = D_MODEL
    beta = DEEPNORM_BETA
    da_col = jnp.concatenate([jnp.ones((4 * DA_QK_WIDTH,), jnp.float32),
                              jnp.full((DA_OUT_WIDTH,), beta, jnp.float32)])
    dl_col = jnp.ones((DIL_GROUPS, 3, DIL_GROUP_WIDTH), jnp.float32).at[:, 2].set(beta).reshape(-1)
    return {
        'x_prompt': nrm(ks[0], (BATCH, SEQ, D)),
        'x_sample': nrm(ks[1], (DEC_BATCH, DEC_SEQ, D)),
        'da_w_in': nrm(ks[2], (N_LAYERS_A, D, DA_IN_WIDTH)) * (D ** -0.5) * da_col,
        'da_w_out': nrm(ks[3], (N_LAYERS_A, DA_OUT_WIDTH, D)) * (DA_OUT_WIDTH ** -0.5) * beta,
        'da_lambda_q1': nrm(ks[4], (N_LAYERS_A, DA_HEAD_DIM)) * 0.1,
        'da_lambda_k1': nrm(ks[5], (N_LAYERS_A, DA_HEAD_DIM)) * 0.1,
        'da_lambda_q2': nrm(ks[6], (N_LAYERS_A, DA_HEAD_DIM)) * 0.1,
        'da_lambda_k2': nrm(ks[7], (N_LAYERS_A, DA_HEAD_DIM)) * 0.1,
        'da_subln_g': 1.0 + 0.02 * nrm(ks[8], (N_LAYERS_A, DA_V_DIM)),
        'dl_w_in': nrm(ks[9], (N_LAYERS_B, D, DIL_IN_WIDTH)) * (D ** -0.5) * dl_col,
        'dl_w_out': nrm(ks[10], (N_LAYERS_B, DIL_GROUP_WIDTH, D)) * (DIL_GROUP_WIDTH ** -0.5) * beta,
        'ln1_g': 1.0 + 0.02 * nrm(ks[11], (DEPTH, D)),
        'ln1_b': 0.02 * nrm(ks[12], (DEPTH, D)),
        'ln2_g': 1.0 + 0.02 * nrm(ks[13], (DEPTH, D)),
        'ln2_b': 0.02 * nrm(ks[14], (DEPTH, D)),
        'moe_router_group': nrm(ks[15], (DEPTH, D, MOE_GROUPS)) * (D ** -0.5),
        'moe_router_expert': nrm(ks[16], (DEPTH, D, MOE_GROUPS, MOE_EXPERTS)) * (D ** -0.5),
        'moe_w_gate': nrm(ks[17], (DEPTH, MOE_GROUPS, MOE_EXPERTS, D, MOE_FF)) * (D ** -0.5),
        'moe_w_up': nrm(ks[18], (DEPTH, MOE_GROUPS, MOE_EXPERTS, D, MOE_FF)) * (D ** -0.5) * beta,
        'moe_w_down': nrm(ks[19], (DEPTH, MOE_GROUPS, MOE_EXPERTS, MOE_FF, D)) * (MOE_FF ** -0.5) * beta,
    }


def reference(x_prompt, x_sample, da_w_in, da_w_out, da_lambda_q1, da_lambda_k1, da_lambda_q2,
              da_lambda_k2, da_subln_g, dl_w_in, dl_w_out, ln1_g, ln1_b, ln2_g, ln2_b,
              moe_router_group, moe_router_expert, moe_w_gate, moe_w_up, moe_w_down):
    y_prompt = trunk(x_prompt, da_w_in, da_w_out, da_lambda_q1, da_lambda_k1, da_lambda_q2, da_lambda_k2,
                     da_subln_g, dl_w_in, dl_w_out, ln1_g, ln1_b, ln2_g, ln2_b,
                     moe_router_group, moe_router_expert, moe_w_gate, moe_w_up, moe_w_down)
    y_sample = trunk(x_sample, da_w_in, da_w_out, da_lambda_q1, da_lambda_k1, da_lambda_q2, da_lambda_k2,
                     da_subln_g, dl_w_in, dl_w_out, ln1_g, ln1_b, ln2_g, ln2_b,
                     moe_router_group, moe_router_expert, moe_w_gate, moe_w_up, moe_w_down)
    return (y_prompt, y_sample)
```

```python
import functools
import math

import jax
import jax.numpy as jnp
import numpy as np
from jax import lax
from jax.experimental import pallas as pl
from jax.experimental.pallas import tpu as pltpu

F32 = jnp.float32
BF16 = jnp.bfloat16

LANES = 128
VMEM_LIMIT_BYTES = 56 * 1024 * 1024

D_MODEL = 1024
DEPTH = 4
N_MIXERS = 2

DA_HEADS = 8
DA_HEAD_DIM = 64
DA_V_DIM = 2 * DA_HEAD_DIM
DA_QK_WIDTH = DA_HEADS * DA_HEAD_DIM

DIL_PATTERNS = ((128, 1), (512, 4), (2048, 16))
DIL_GROUPS = len(DIL_PATTERNS)
DIL_HEADS = 16
DIL_HEAD_DIM = 64
DIL_GROUP_WIDTH = DIL_HEADS * DIL_HEAD_DIM
DIL_HALF = 64

ROPE_THETA = 500000.0
ROPE_ROT = DA_HEAD_DIM // 4
ROPE_HALF = ROPE_ROT // 2

LN_EPS = 1e-5
DEEPNORM_ALPHA = (2.0 * DEPTH) ** 0.25

MOE_GROUPS = 4
MOE_EXPERTS = 4
MOE_FF = 512
N_EXPERTS = MOE_GROUPS * MOE_EXPERTS

NEG_INF = -1e30
QK_SCALE = DA_HEAD_DIM ** -0.5


def _params(*sem):
    return pltpu.CompilerParams(dimension_semantics=sem, vmem_limit_bytes=VMEM_LIMIT_BYTES)


PROJ_TM = 1024
PROJ_TN = 1024


def _rope_tables(max_len):
    inv = ROPE_THETA ** (-jnp.arange(ROPE_HALF, dtype=F32) * (2.0 / ROPE_ROT))
    ang = jnp.arange(max_len, dtype=F32)[:, None] * inv[None, :]
    cos, sin = jnp.cos(ang), jnp.sin(ang)
    ones = jnp.ones((max_len, DA_HEAD_DIM - ROPE_ROT), F32)
    zeros_h = jnp.zeros((max_len, ROPE_HALF), F32)
    zeros_r = jnp.zeros((max_len, DA_HEAD_DIM - ROPE_ROT), F32)
    c = jnp.concatenate([cos, cos, ones], axis=1)
    s_up = jnp.concatenate([zeros_h, sin, zeros_r], axis=1)
    s_dn = jnp.concatenate([-sin, zeros_h, zeros_r], axis=1)
    reps = LANES // DA_HEAD_DIM
    return jnp.stack([jnp.tile(c, (1, reps)), jnp.tile(s_up, (1, reps)), jnp.tile(s_dn, (1, reps))])


def _proj_kernel(pos_ref, x_ref, w_ref, rope_ref, o_ref, *, v_period):
    del pos_ref
    j = pl.program_id(1)
    y = jnp.dot(x_ref[...], w_ref[...], preferred_element_type=F32)
    is_v = (j % v_period) == (v_period - 1)

    @pl.when(is_v)
    def _():
        o_ref[...] = y.astype(o_ref.dtype)

    @pl.when(jnp.logical_not(is_v))
    def _():
        c = rope_ref[0]
        s_up = rope_ref[1]
        s_dn = rope_ref[2]
        for k in range(y.shape[1] // LANES):
            yc = y[:, k * LANES:(k + 1) * LANES]
            r = yc * c + pltpu.roll(yc, ROPE_HALF, 1) * s_up + pltpu.roll(yc, LANES - ROPE_HALF, 1) * s_dn
            o_ref[:, k * LANES:(k + 1) * LANES] = r.astype(o_ref.dtype)


def _proj(xb, w, rope, pos_blocks, v_period):
    t, d = xb.shape
    n = w.shape[1]
    grid_spec = pltpu.PrefetchScalarGridSpec(
        num_scalar_prefetch=1,
        grid=(t // PROJ_TM, n // PROJ_TN),
        in_specs=[
            pl.BlockSpec((PROJ_TM, d), lambda i, j, pos: (i, 0)),
            pl.BlockSpec((d, PROJ_TN), lambda i, j, pos: (0, j)),
            pl.BlockSpec((3, PROJ_TM, LANES), lambda i, j, pos: (0, pos[i], 0)),
        ],
        out_specs=pl.BlockSpec((PROJ_TM, PROJ_TN), lambda i, j, pos: (i, j)),
    )
    return pl.pallas_call(
        functools.partial(_proj_kernel, v_period=v_period),
        grid_spec=grid_spec,
        out_shape=jax.ShapeDtypeStruct((t, n), BF16),
        compiler_params=_params("parallel", "arbitrary"),
        name="proj_rope",
    )(pos_blocks, xb, w, rope)


DA_TK = 512


def _projt_kernel(x_ref, wt_ref, o_ref):
    y = lax.dot_general(wt_ref[...], x_ref[...], (((1,), (1,)), ((), ())), preferred_element_type=F32)
    o_ref[...] = y.reshape(o_ref.shape).astype(o_ref.dtype)


def _proj_t(xb, wt):
    t, d = xb.shape
    n = wt.shape[0]
    h = n // DA_V_DIM
    return pl.pallas_call(
        _projt_kernel,
        grid=(t // DA_TK,),
        in_specs=[
            pl.BlockSpec((DA_TK, d), lambda i: (i, 0)),
            pl.BlockSpec((n, d), lambda i: (0, 0)),
        ],
        out_specs=pl.BlockSpec((h, 1, DA_V_DIM, DA_TK), lambda i: (0, i, 0, 0)),
        out_shape=jax.ShapeDtypeStruct((h, t // DA_TK, DA_V_DIM, DA_TK), BF16),
        compiler_params=_params("parallel"),
        name="proj_vt",
    )(xb, wt)


DA_TQ = 512


def _da_attn_kernel(lam_ref, g_ref, q_ref, k_ref, vt_ref, o_ref,
                    m1_ref, l1_ref, a1_ref, m2_ref, l2_ref, a2_ref, *, n_kblk, lambda_init):
    q = q_ref[...]
    lane = lax.broadcasted_iota(jnp.int32, q.shape, 1)
    zero = jnp.zeros_like(q)
    scale = jnp.asarray(QK_SCALE, q.dtype)
    q1 = jnp.where(lane < DA_HEAD_DIM, q, zero) * scale
    q2 = jnp.where(lane >= DA_HEAD_DIM, q, zero) * scale

    for m_ref, l_ref, a_ref in ((m1_ref, l1_ref, a1_ref), (m2_ref, l2_ref, a2_ref)):
        m_ref[...] = jnp.full(m_ref.shape, NEG_INF, F32)
        l_ref[...] = jnp.zeros(l_ref.shape, F32)
        a_ref[...] = jnp.zeros(a_ref.shape, F32)

    def step(j, carry):
        k = k_ref[pl.ds(pl.multiple_of(j * DA_TK, DA_TK), DA_TK), :]
        vt = vt_ref[0, j]
        for qm, m_ref, l_ref, a_ref in ((q1, m1_ref, l1_ref, a1_ref), (q2, m2_ref, l2_ref, a2_ref)):
            s = lax.dot_general(k, qm, (((1,), (1,)), ((), ())), preferred_element_type=F32)
            m_old = m_ref[...]
            m_new = jnp.maximum(m_old, jnp.max(s, axis=0, keepdims=True))
            alpha = jnp.exp(m_old - m_new)
            p = jnp.exp(s - m_new)
            l_ref[...] = alpha * l_ref[...] + jnp.sum(p, axis=0, keepdims=True)
            a_ref[...] = alpha * a_ref[...] + jnp.dot(vt, p.astype(vt.dtype), preferred_element_type=F32)
            m_ref[...] = m_new
        return carry

    lax.fori_loop(0, n_kblk, step, 0)

    lp = lam_ref[...]
    lam = (jnp.exp(jnp.sum(lp[0:1] * lp[1:2], axis=1, keepdims=True))
           - jnp.exp(jnp.sum(lp[2:3] * lp[3:4], axis=1, keepdims=True)) + lambda_init)
    o = a1_ref[...] * (1.0 / l1_ref[...]) - lam * (a2_ref[...] * (1.0 / l2_ref[...]))
    o = o * lax.rsqrt(jnp.mean(jnp.square(o), axis=0, keepdims=True) + LN_EPS)
    o = o * g_ref[...] * (1.0 - lambda_init)
    o_ref[...] = o.T.astype(o_ref.dtype)


def _da_attention(qk, vt, lam_params, subln_g, lambda_init, row_off, batch, seq):
    h = DA_HEADS
    nq = seq // DA_TQ
    nk = seq // DA_TK
    qoff = row_off // DA_TQ
    soff = row_off // seq
    return pl.pallas_call(
        functools.partial(_da_attn_kernel, n_kblk=nk, lambda_init=lambda_init),
        grid=(batch, h, nq),
        in_specs=[
            pl.BlockSpec((4, DA_HEAD_DIM), lambda b, hh, i: (0, 0)),
            pl.BlockSpec((DA_V_DIM, 1), lambda b, hh, i: (0, 0)),
            pl.BlockSpec((DA_TQ, LANES), lambda b, hh, i: (qoff + b * nq + i, hh)),
            pl.BlockSpec((seq, LANES), lambda b, hh, i: (soff + b, h + hh)),
            pl.BlockSpec((1, nk, DA_V_DIM, DA_TK), lambda b, hh, i: (hh, soff + b, 0, 0)),
        ],
        out_specs=pl.BlockSpec((DA_TQ, DA_V_DIM), lambda b, hh, i: (b * nq + i, hh)),
        out_shape=jax.ShapeDtypeStruct((batch * seq, h * DA_V_DIM), BF16),
        scratch_shapes=[
            pltpu.VMEM((1, DA_TQ), F32), pltpu.VMEM((1, DA_TQ), F32), pltpu.VMEM((DA_V_DIM, DA_TQ), F32),
            pltpu.VMEM((1, DA_TQ), F32), pltpu.VMEM((1, DA_TQ), F32), pltpu.VMEM((DA_V_DIM, DA_TQ), F32),
        ],
        compiler_params=_params("parallel", "parallel", "arbitrary"),
        name="da_attn",
    )(lam_params, subln_g, qk, qk, vt)


DIL_QSUB = 128
DIL_KWIN = DIL_QSUB + 2 * DIL_HALF
DIL_QBLK = 1024


def _dil_attn_kernel(q_ref, k_ref, v_ref, o_ref, lse_ref, *, sub_len, qblk):
    qi = pl.program_id(3)
    lane = lax.broadcasted_iota(jnp.int32, (DIL_QSUB, LANES), 1)
    lo = lane < DIL_HEAD_DIM
    row = lax.broadcasted_iota(jnp.int32, (DIL_QSUB, DIL_KWIN), 0)
    col = lax.broadcasted_iota(jnp.int32, (DIL_QSUB, DIL_KWIN), 1)
    scale = jnp.asarray(QK_SCALE, q_ref.dtype)

    def step(n, carry):
        q0 = pl.multiple_of(n * DIL_QSUB, DIL_QSUB)
        gq0 = qi * qblk + q0
        ws = jnp.clip(gq0 - DIL_HALF, 0, sub_len - DIL_KWIN)
        ws = pl.multiple_of(ws, DIL_HALF)
        qb = q_ref[pl.ds(q0, DIL_QSUB), :]
        kw = k_ref[pl.ds(ws, DIL_KWIN), :]
        vw = v_ref[pl.ds(ws, DIL_KWIN), :]
        valid = jnp.abs((ws + col) - (gq0 + row)) <= DIL_HALF
        zero = jnp.zeros_like(qb)
        outs, lses = [], []
        for qm in (jnp.where(lo, qb, zero) * scale, jnp.where(lo, zero, qb) * scale):
            s = lax.dot_general(qm, kw, (((1,), (1,)), ((), ())), preferred_element_type=F32)
            s = jnp.where(valid, s, NEG_INF)
            m = jnp.max(s, axis=1, keepdims=True)
            p = jnp.exp(s - m)
            l = jnp.sum(p, axis=1, keepdims=True)
            o = jnp.dot(p.astype(vw.dtype), vw, preferred_element_type=F32) * (1.0 / l)
            outs.append(o)
            lses.append(jnp.broadcast_to(m + jnp.log(l), (DIL_QSUB, LANES)))
        o_ref[pl.ds(q0, DIL_QSUB), :] = jnp.where(lo, outs[0], outs[1]).astype(o_ref.dtype)
        lse_ref[pl.ds(q0, DIL_QSUB), :] = jnp.where(lo, lses[0], lses[1])
        return carry

    lax.fori_loop(0, qblk // DIL_QSUB, step, 0)


def _dil_attention_group(proj, g, dilation, row_off, batch, seq, t_total):
    n_cols = proj.shape[1]
    sub_len = seq // dilation
    qblk = min(DIL_QBLK, sub_len)
    nqb = sub_len // qblk
    cb = n_cols // LANES
    hp = DIL_GROUP_WIDTH // LANES
    view = proj.reshape(t_total // dilation, dilation * n_cols)
    soff = row_off // seq
    base = g * 3 * hp

    def qmap(b, r, p, i):
        return ((soff + b) * nqb + i, r * cb + base + p)

    def kmap(b, r, p, i):
        return (soff + b, r * cb + base + hp + p)

    def vmap(b, r, p, i):
        return (soff + b, r * cb + base + 2 * hp + p)

    def omap(b, r, p, i):
        return (b * nqb + i, r * hp + p)

    out_rows = batch * sub_len
    o, lse = pl.pallas_call(
        functools.partial(_dil_attn_kernel, sub_len=sub_len, qblk=qblk),
        grid=(batch, dilation, hp, nqb),
        in_specs=[
            pl.BlockSpec((qblk, LANES), qmap),
            pl.BlockSpec((sub_len, LANES), kmap),
            pl.BlockSpec((sub_len, LANES), vmap),
        ],
        out_specs=[
            pl.BlockSpec((qblk, LANES), omap),
            pl.BlockSpec((qblk, LANES), omap),
        ],
        out_shape=[
            jax.ShapeDtypeStruct((out_rows, dilation * DIL_GROUP_WIDTH), BF16),
            jax.ShapeDtypeStruct((out_rows, dilation * DIL_GROUP_WIDTH), F32),
        ],
        compiler_params=_params("parallel", "parallel", "parallel", "arbitrary"),
        name=f"dil_attn_g{g}",
    )(view, view, view)
    return (o.reshape(batch * seq, DIL_GROUP_WIDTH), lse.reshape(batch * seq, DIL_GROUP_WIDTH))


POST_TM = 512


def _layer_norm_rows(z, g, b):
    mu = jnp.mean(z, axis=-1, keepdims=True)
    zc = z - mu
    var = jnp.mean(jnp.square(zc), axis=-1, keepdims=True)
    return zc * lax.rsqrt(var + LN_EPS) * g + b


def _post_kernel(*refs, n_groups):
    o_refs = refs[:n_groups]
    lse_refs = refs[n_groups:2 * n_groups] if n_groups > 1 else ()
    w_ref, x_ref, g_ref, b_ref, y_ref, yb_ref = refs[-6:]
    if n_groups == 1:
        o = o_refs[0][...]
    else:
        lses = [r[...] for r in lse_refs]
        m = functools.reduce(jnp.maximum, lses)
        es = [jnp.exp(l - m) for l in lses]
        inv = 1.0 / functools.reduce(jnp.add, es)
        acc = None
        for e, o_ref in zip(es, o_refs):
            term = (e * inv) * o_ref[...].astype(F32)
            acc = term if acc is None else acc + term
        o = acc.astype(BF16)
    h = jnp.dot(o, w_ref[...], preferred_element_type=F32)
    y = _layer_norm_rows(DEEPNORM_ALPHA * x_ref[...] + h, g_ref[...], b_ref[...])
    y_ref[...] = y
    yb_ref[...] = y.astype(BF16)


def _post_attention(os_, lses, w_out, x, g, b):
    t, d = x.shape
    n_groups = len(os_)
    row = pl.BlockSpec((POST_TM, d), lambda i: (i, 0))
    full = pl.BlockSpec((d, d), lambda i: (0, 0))
    vec = pl.BlockSpec((1, d), lambda i: (0, 0))
    return pl.pallas_call(
        functools.partial(_post_kernel, n_groups=n_groups),
        grid=(t // POST_TM,),
        in_specs=[row] * (n_groups + len(lses)) + [full, row, vec, vec],
        out_specs=[row, row],
        out_shape=[jax.ShapeDtypeStruct((t, d), F32), jax.ShapeDtypeStruct((t, d), BF16)],
        compiler_params=_params("parallel"),
        name="post_attn_ln",
    )(*os_, *lses, w_out, x, g, b)


ROUTER_TM = 512
ROUTER_EXPERT_LANE0 = MOE_GROUPS


def _first_argmax(vals, vmax, lane_f):
    return jnp.min(jnp.where(vals == vmax, lane_f, float(LANES)), axis=1, keepdims=True)


def _router_kernel(x_ref, w_ref, comb_ref):
    logits = jnp.dot(x_ref[...], w_ref[...], preferred_element_type=F32, precision=lax.Precision.HIGHEST)
    lane_f = lax.broadcasted_iota(jnp.int32, logits.shape, 1).astype(F32)
    gl = jnp.where(lane_f < MOE_GROUPS, logits, NEG_INF)
    gmax = jnp.max(gl, axis=1, keepdims=True)
    gw = 1.0 / jnp.sum(jnp.exp(gl - gmax), axis=1, keepdims=True)
    gsel = _first_argmax(gl, gmax, lane_f)
    e0 = ROUTER_EXPERT_LANE0 + MOE_EXPERTS * gsel
    el = jnp.where((lane_f >= e0) & (lane_f < e0 + MOE_EXPERTS), logits, NEG_INF)
    v1 = jnp.max(el, axis=1, keepdims=True)
    i1 = _first_argmax(el, v1, lane_f)
    el2 = jnp.where(lane_f == i1, NEG_INF, el)
    v2 = jnp.max(el2, axis=1, keepdims=True)
    i2 = _first_argmax(el2, v2, lane_f)
    e2 = jnp.exp(v2 - v1)
    inv = gw / (1.0 + e2)
    comb_ref[...] = jnp.where(lane_f == i1, inv, 0.0) + jnp.where(lane_f == i2, e2 * inv, 0.0)


def _router(x, w_router):
    t, d = x.shape
    return pl.pallas_call(
        _router_kernel,
        grid=(t // ROUTER_TM,),
        in_specs=[pl.BlockSpec((ROUTER_TM, d), lambda i: (i, 0)), pl.BlockSpec((d, LANES), lambda i: (0, 0))],
        out_specs=pl.BlockSpec((ROUTER_TM, LANES), lambda i: (i, 0)),
        out_shape=jax.ShapeDtypeStruct((t, LANES), F32),
        compiler_params=_params("parallel"),
        name="router",
    )(x, w_router)


MOE_TM = 512


def _moe_kernel(xb_ref, comb_ref, wg_ref, wu_ref, wd_ref, x_ref, g_ref, b_ref, y_ref, yb_ref, acc_ref):
    e = pl.program_id(1)

    @pl.when(e == 0)
    def _():
        acc_ref[...] = jnp.zeros(acc_ref.shape, F32)

    xb = xb_ref[...]
    comb = comb_ref[...]
    lane = lax.broadcasted_iota(jnp.int32, comb.shape, 1)
    ce = jnp.sum(jnp.where(lane == e + ROUTER_EXPERT_LANE0, comb, 0.0), axis=1, keepdims=True)
    gate = jnp.dot(xb, wg_ref[0], preferred_element_type=F32)
    up = jnp.dot(xb, wu_ref[0], preferred_element_type=F32)
    hid = (gate * jax.nn.sigmoid(gate)) * up * ce
    acc_ref[...] += jnp.dot(hid.astype(BF16), wd_ref[0], preferred_element_type=F32)

    @pl.when(e == pl.num_programs(1) - 1)
    def _():
        y = _layer_norm_rows(DEEPNORM_ALPHA * x_ref[...] + acc_ref[...], g_ref[...], b_ref[...])
        y_ref[...] = y
        yb_ref[...] = y.astype(BF16)


def _moe(xb, comb, w_gate, w_up, w_down, x, g, b):
    t, d = x.shape
    ne, _, f = w_gate.shape
    row = pl.BlockSpec((MOE_TM, d), lambda i, e: (i, 0))
    vec = pl.BlockSpec((1, d), lambda i, e: (0, 0))
    return pl.pallas_call(
        _moe_kernel,
        grid=(t // MOE_TM, ne),
        in_specs=[
            row,
            pl.BlockSpec((MOE_TM, LANES), lambda i, e: (i, 0)),
            pl.BlockSpec((1, d, f), lambda i, e: (e, 0, 0)),
            pl.BlockSpec((1, d, f), lambda i, e: (e, 0, 0)),
            pl.BlockSpec((1, f, d), lambda i, e: (e, 0, 0)),
            row, vec, vec,
        ],
        out_specs=[row, row],
        out_shape=[jax.ShapeDtypeStruct((t, d), F32), jax.ShapeDtypeStruct((t, d), BF16)],
        scratch_shapes=[pltpu.VMEM((MOE_TM, d), F32)],
        compiler_params=_params("parallel", "arbitrary"),
        name="moe_ln",
    )(xb, comb, w_gate, w_up, w_down, x, g, b)


def _da_weight_layout(w_in):
    d = w_in.shape[0]
    w = DA_QK_WIDTH
    q1, q2, k1, k2 = (w_in[:, i * w:(i + 1) * w].reshape(d, DA_HEADS, DA_HEAD_DIM) for i in range(4))
    qk = jnp.concatenate([jnp.concatenate([q1, q2], axis=2).reshape(d, 2 * w),
                          jnp.concatenate([k1, k2], axis=2).reshape(d, 2 * w)], axis=1)
    return qk, w_in[:, 4 * w:]


def _trunk(xs, da_w_in, da_w_out, da_lambda_q1, da_lambda_k1, da_lambda_q2, da_lambda_k2, da_subln_g,
           dl_w_in, dl_w_out, ln1_g, ln1_b, ln2_g, ln2_b,
           moe_router_group, moe_router_expert, moe_w_gate, moe_w_up, moe_w_down):
    d = xs[0].shape[-1]
    segs = []
    off = 0
    for x in xs:
        segs.append((off, x.shape[0], x.shape[1]))
        off += x.shape[0] * x.shape[1]
    t_total = off
    x = jnp.concatenate([v.reshape(-1, d) for v in xs], axis=0)
    xb = x.astype(BF16)

    max_len = max(s for _, _, s in segs)
    rope = _rope_tables(max_len)
    pos_blocks = np.concatenate([
        np.tile(np.arange(s // PROJ_TM, dtype=np.int32), b) for _, b, s in segs])
    pos_blocks = jnp.asarray(pos_blocks)

    for i in range(DEPTH):
        j = i // N_MIXERS
        if i % N_MIXERS == 0:
            lambda_init = 0.8 - 0.6 * math.exp(-0.3 * i)
            w_qk, w_v = _da_weight_layout(da_w_in[j])
            qk = _proj(xb, w_qk.astype(BF16), rope, pos_blocks, v_period=3)
            vt = _proj_t(xb, w_v.T.astype(BF16))
            lam_params = jnp.stack([da_lambda_q1[j], da_lambda_k1[j], da_lambda_q2[j], da_lambda_k2[j]]).astype(F32)
            g_col = da_subln_g[j].astype(F32).reshape(DA_V_DIM, 1)
            o = jnp.concatenate([
                _da_attention(qk, vt, lam_params, g_col, lambda_init, ro, b, s) for ro, b, s in segs], axis=0)
            os_, lses = [o], []
            w_out = da_w_out[j]
        else:
            proj = _proj(xb, dl_w_in[j].astype(BF16), rope, pos_blocks, v_period=3)
            os_, lses = [], []
            for g, (window, dilation) in enumerate(DIL_PATTERNS):
                assert window // (2 * dilation) == DIL_HALF
                parts = [_dil_attention_group(proj, g, dilation, ro, b, s, t_total) for ro, b, s in segs]
                os_.append(jnp.concatenate([p[0] for p in parts], axis=0))
                lses.append(jnp.concatenate([p[1] for p in parts], axis=0))
            w_out = dl_w_out[j]
        x, xb = _post_attention(os_, lses, w_out.astype(BF16), x,
                                ln1_g[i].reshape(1, d).astype(F32), ln1_b[i].reshape(1, d).astype(F32))

        w_router = jnp.concatenate([
            moe_router_group[i].astype(F32),
            moe_router_expert[i].reshape(d, N_EXPERTS).astype(F32),
            jnp.zeros((d, LANES - MOE_GROUPS - N_EXPERTS), F32)], axis=1)
        comb = _router(x, w_router)
        x, xb = _moe(xb, comb,
                     moe_w_gate[i].reshape(N_EXPERTS, d, MOE_FF).astype(BF16),
                     moe_w_up[i].reshape(N_EXPERTS, d, MOE_FF).astype(BF16),
                     moe_w_down[i].reshape(N_EXPERTS, MOE_FF, d).astype(BF16),
                     x, ln2_g[i].reshape(1, d).astype(F32), ln2_b[i].reshape(1, d).astype(F32))

    outs = []
    for (ro, b, s), v in zip(segs, xs):
        outs.append(x[ro:ro + b * s].reshape(v.shape))
    return outs


def kernel(x_prompt, x_sample, da_w_in, da_w_out, da_lambda_q1, da_lambda_k1, da_lambda_q2, da_lambda_k2, da_subln_g, dl_w_in, dl_w_out, ln1_g, ln1_b, ln2_g, ln2_b, moe_router_group, moe_router_expert, moe_w_gate, moe_w_up, moe_w_down):
    y_prompt, y_sample = _trunk(
        [x_prompt, x_sample], da_w_in, da_w_out, da_lambda_q1, da_lambda_k1, da_lambda_q2, da_lambda_k2,
        da_subln_g, dl_w_in, dl_w_out, ln1_g, ln1_b, ln2_g, ln2_b,
        moe_router_group, moe_router_expert, moe_w_gate, moe_w_up, moe_w_down)
    return (y_prompt, y_sample)
```

```python
import functools
import math

import jax
import jax.numpy as jnp
import numpy as np
from jax import lax
from jax.experimental import pallas as pl
from jax.experimental.pallas import tpu as pltpu

F32 = jnp.float32
BF16 = jnp.bfloat16

LANES = 128
VMEM_LIMIT_BYTES = 56 * 1024 * 1024

D_MODEL = 1024
DEPTH = 4
N_MIXERS = 2

DA_HEADS = 8
DA_HEAD_DIM = 64
DA_V_DIM = 2 * DA_HEAD_DIM
DA_QK_WIDTH = DA_HEADS * DA_HEAD_DIM

DIL_PATTERNS = ((128, 1), (512, 4), (2048, 16))
DIL_GROUPS = len(DIL_PATTERNS)
DIL_HEADS = 16
DIL_HEAD_DIM = 64
DIL_GROUP_WIDTH = DIL_HEADS * DIL_HEAD_DIM
DIL_HALF = 64

ROPE_THETA = 500000.0
ROPE_ROT = DA_HEAD_DIM // 4
ROPE_HALF = ROPE_ROT // 2

LN_EPS = 1e-5
DEEPNORM_ALPHA = (2.0 * DEPTH) ** 0.25

MOE_GROUPS = 4
MOE_EXPERTS = 4
MOE_FF = 512
N_EXPERTS = MOE_GROUPS * MOE_EXPERTS

NEG_INF = -1e30
QK_SCALE = DA_HEAD_DIM ** -0.5


def _params(*sem):
    return pltpu.CompilerParams(dimension_semantics=sem, vmem_limit_bytes=VMEM_LIMIT_BYTES)


PROJ_TM = 1024
PROJ_TN = 1024


def _rope_tables(max_len):
    inv = ROPE_THETA ** (-jnp.arange(ROPE_HALF, dtype=F32) * (2.0 / ROPE_ROT))
    ang = jnp.arange(max_len, dtype=F32)[:, None] * inv[None, :]
    cos, sin = jnp.cos(ang), jnp.sin(ang)
    ones = jnp.ones((max_len, DA_HEAD_DIM - ROPE_ROT), F32)
    zeros_h = jnp.zeros((max_len, ROPE_HALF), F32)
    zeros_r = jnp.zeros((max_len, DA_HEAD_DIM - ROPE_ROT), F32)
    c = jnp.concatenate([cos, cos, ones], axis=1)
    s_up = jnp.concatenate([zeros_h, sin, zeros_r], axis=1)
    s_dn = jnp.concatenate([-sin, zeros_h, zeros_r], axis=1)
    reps = LANES // DA_HEAD_DIM
    return jnp.stack([jnp.tile(c, (1, reps)), jnp.tile(s_up, (1, reps)), jnp.tile(s_dn, (1, reps))])


def _proj_kernel(pos_ref, x_ref, w_ref, rope_ref, o_ref, *, v_period, q_scale):
    del pos_ref
    j = pl.program_id(1)
    y = jnp.dot(x_ref[...], w_ref[...], preferred_element_type=F32)
    if q_scale is not None:
        y = y * jnp.where(j % v_period == 0, q_scale, 1.0).astype(F32)
    is_v = (j % v_period) == (v_period - 1)

    @pl.when(is_v)
    def _():
        o_ref[...] = y.astype(o_ref.dtype)

    @pl.when(jnp.logical_not(is_v))
    def _():
        c = rope_ref[0]
        s_up = rope_ref[1]
        s_dn = rope_ref[2]
        for k in range(y.shape[1] // LANES):
            yc = y[:, k * LANES:(k + 1) * LANES]
            r = yc * c + pltpu.roll(yc, ROPE_HALF, 1) * s_up + pltpu.roll(yc, LANES - ROPE_HALF, 1) * s_dn
            o_ref[:, k * LANES:(k + 1) * LANES] = r.astype(o_ref.dtype)


def _proj(xb, w, rope, pos_blocks, v_period, q_scale=None):
    t, d = xb.shape
    n = w.shape[1]
    grid_spec = pltpu.PrefetchScalarGridSpec(
        num_scalar_prefetch=1,
        grid=(t // PROJ_TM, n // PROJ_TN),
        in_specs=[
            pl.BlockSpec((PROJ_TM, d), lambda i, j, pos: (i, 0)),
            pl.BlockSpec((d, PROJ_TN), lambda i, j, pos: (0, j)),
            pl.BlockSpec((3, PROJ_TM, LANES), lambda i, j, pos: (0, pos[i], 0)),
        ],
        out_specs=pl.BlockSpec((PROJ_TM, PROJ_TN), lambda i, j, pos: (i, j)),
    )
    return pl.pallas_call(
        functools.partial(_proj_kernel, v_period=v_period, q_scale=q_scale),
        grid_spec=grid_spec,
        out_shape=jax.ShapeDtypeStruct((t, n), BF16),
        compiler_params=_params("parallel", "arbitrary"),
        name="proj_rope",
    )(pos_blocks, xb, w, rope)


DA_TK = 512
DA_ONES_ROWS = 16
DA_VT_ROWS = DA_V_DIM + DA_ONES_ROWS


def _projt_kernel(x_ref, wt_ref, o_ref):
    y = lax.dot_general(wt_ref[...], x_ref[...], (((1,), (1,)), ((), ())), preferred_element_type=F32)
    h = o_ref.shape[0]
    o_ref[:, 0, 0:DA_V_DIM, :] = y.reshape(h, DA_V_DIM, DA_TK).astype(o_ref.dtype)
    o_ref[:, 0, DA_V_DIM:DA_VT_ROWS, :] = jnp.ones((h, DA_ONES_ROWS, DA_TK), o_ref.dtype)


def _proj_t(xb, wt):
    t, d = xb.shape
    n = wt.shape[0]
    h = n // DA_V_DIM
    return pl.pallas_call(
        _projt_kernel,
        grid=(t // DA_TK,),
        in_specs=[
            pl.BlockSpec((DA_TK, d), lambda i: (i, 0)),
            pl.BlockSpec((n, d), lambda i: (0, 0)),
        ],
        out_specs=pl.BlockSpec((h, 1, DA_VT_ROWS, DA_TK), lambda i: (0, i, 0, 0)),
        out_shape=jax.ShapeDtypeStruct((h, t // DA_TK, DA_VT_ROWS, DA_TK), BF16),
        compiler_params=_params("parallel"),
        name="proj_vt",
    )(xb, wt)


DA_TQ = 512
DA_Q_SCALE = QK_SCALE * math.log2(math.e)


def _da_attn_kernel(lam_ref, g_ref, q_ref, k_ref, vt_ref, o_ref,
                    q1_ref, q2_ref, sa1_ref, sa2_ref, sb1_ref, sb2_ref,
                    m1_ref, a1_ref, m2_ref, a2_ref, *, n_kblk, lambda_init):
    q = q_ref[...]
    lane = lax.broadcasted_iota(jnp.int32, q.shape, 1)
    zero = jnp.zeros_like(q)
    q1_ref[...] = jnp.where(lane < DA_HEAD_DIM, q, zero)
    q2_ref[...] = jnp.where(lane >= DA_HEAD_DIM, q, zero)
    for m_ref, a_ref in ((m1_ref, a1_ref), (m2_ref, a2_ref)):
        m_ref[...] = jnp.full(m_ref.shape, NEG_INF, F32)
        a_ref[...] = jnp.zeros(a_ref.shape, F32)

    def scores(j, s1_ref, s2_ref):
        k = k_ref[pl.ds(pl.multiple_of(j * DA_TK, DA_TK), DA_TK), :]
        nt = (((1,), (1,)), ((), ()))
        s1_ref[...] = lax.dot_general(k, q1_ref[...], nt, preferred_element_type=F32)
        s2_ref[...] = lax.dot_general(k, q2_ref[...], nt, preferred_element_type=F32)

    def softmax_pv(j, s1_ref, s2_ref):
        vt = vt_ref[0, j]
        for s_ref, m_ref, a_ref in ((s1_ref, m1_ref, a1_ref), (s2_ref, m2_ref, a2_ref)):
            s = s_ref[...]
            m_old = m_ref[...]
            m_new = jnp.maximum(m_old, jnp.max(s, axis=0, keepdims=True))
            p = jnp.exp2(s - m_new).astype(vt.dtype)
            a_ref[...] = jnp.exp2(m_old - m_new) * a_ref[...] + jnp.dot(vt, p, preferred_element_type=F32)
            m_ref[...] = m_new

    scores(0, sa1_ref, sa2_ref)

    def pair(jj, carry):
        j = 2 * jj
        scores(j + 1, sb1_ref, sb2_ref)
        softmax_pv(j, sa1_ref, sa2_ref)
        scores(j + 2, sa1_ref, sa2_ref)
        softmax_pv(j + 1, sb1_ref, sb2_ref)
        return carry

    lax.fori_loop(0, n_kblk // 2 - 1, pair, 0)
    scores(n_kblk - 1, sb1_ref, sb2_ref)
    softmax_pv(n_kblk - 2, sa1_ref, sa2_ref)
    softmax_pv(n_kblk - 1, sb1_ref, sb2_ref)

    lp = lam_ref[...]
    lam = (jnp.exp(jnp.sum(lp[0:1] * lp[1:2], axis=1, keepdims=True))
           - jnp.exp(jnp.sum(lp[2:3] * lp[3:4], axis=1, keepdims=True)) + lambda_init)
    a1 = a1_ref[...]
    a2 = a2_ref[...]
    o = (a1[0:DA_V_DIM] * (1.0 / a1[DA_V_DIM:DA_V_DIM + 1])
         - lam * (a2[0:DA_V_DIM] * (1.0 / a2[DA_V_DIM:DA_V_DIM + 1])))
    o = o * lax.rsqrt(jnp.mean(jnp.square(o), axis=0, keepdims=True) + LN_EPS)
    o = o * g_ref[...] * (1.0 - lambda_init)
    o_ref[...] = o.T.astype(o_ref.dtype)


def _da_attention(qk, vt, lam_params, subln_g, lambda_init, row_off, batch, seq):
    h = DA_HEADS
    nq = seq // DA_TQ
    nk = seq // DA_TK
    assert nk % 2 == 0 and nk >= 2
    qoff = row_off // DA_TQ
    soff = row_off // seq
    s_buf = pltpu.VMEM((DA_TK, DA_TQ), F32)
    return pl.pallas_call(
        functools.partial(_da_attn_kernel, n_kblk=nk, lambda_init=lambda_init),
        grid=(batch, h, nq),
        in_specs=[
            pl.BlockSpec((4, DA_HEAD_DIM), lambda b, hh, i: (0, 0)),
            pl.BlockSpec((DA_V_DIM, 1), lambda b, hh, i: (0, 0)),
            pl.BlockSpec((DA_TQ, LANES), lambda b, hh, i: (qoff + b * nq + i, hh)),
            pl.BlockSpec((seq, LANES), lambda b, hh, i: (soff + b, h + hh)),
            pl.BlockSpec((1, nk, DA_VT_ROWS, DA_TK), lambda b, hh, i: (hh, soff + b, 0, 0)),
        ],
        out_specs=pl.BlockSpec((DA_TQ, DA_V_DIM), lambda b, hh, i: (b * nq + i, hh)),
        out_shape=jax.ShapeDtypeStruct((batch * seq, h * DA_V_DIM), BF16),
        scratch_shapes=[
            pltpu.VMEM((DA_TQ, LANES), BF16), pltpu.VMEM((DA_TQ, LANES), BF16),
            s_buf, s_buf, s_buf, s_buf,
            pltpu.VMEM((1, DA_TQ), F32), pltpu.VMEM((DA_VT_ROWS, DA_TQ), F32),
            pltpu.VMEM((1, DA_TQ), F32), pltpu.VMEM((DA_VT_ROWS, DA_TQ), F32),
        ],
        compiler_params=_params("parallel", "parallel", "arbitrary"),
        name="da_attn",
    )(lam_params, subln_g, qk, qk, vt)


DIL_QSUB = 128
DIL_KWIN = DIL_QSUB + 2 * DIL_HALF
DIL_QBLK = 1024


def _dil_attn_kernel(q_ref, k_ref, v_ref, o_ref, lse_ref, *, sub_len, qblk):
    qi = pl.program_id(3)
    lane = lax.broadcasted_iota(jnp.int32, (DIL_QSUB, LANES), 1)
    lo = lane < DIL_HEAD_DIM
    row = lax.broadcasted_iota(jnp.int32, (DIL_QSUB, DIL_KWIN), 0)
    col = lax.broadcasted_iota(jnp.int32, (DIL_QSUB, DIL_KWIN), 1)
    scale = jnp.asarray(QK_SCALE, q_ref.dtype)

    for n in range(qblk // DIL_QSUB):
        q0 = n * DIL_QSUB
        gq0 = qi * qblk + q0
        ws = jnp.clip(gq0 - DIL_HALF, 0, sub_len - DIL_KWIN)
        ws = pl.multiple_of(ws, DIL_HALF)
        qb = q_ref[pl.ds(q0, DIL_QSUB), :]
        kw = k_ref[pl.ds(ws, DIL_KWIN), :]
        vw = v_ref[pl.ds(ws, DIL_KWIN), :]
        valid = jnp.abs((ws + col) - (gq0 + row)) <= DIL_HALF
        zero = jnp.zeros_like(qb)
        outs, lses = [], []
        for qm in (jnp.where(lo, qb, zero) * scale, jnp.where(lo, zero, qb) * scale):
            s = lax.dot_general(qm, kw, (((1,), (1,)), ((), ())), preferred_element_type=F32)
            s = jnp.where(valid, s, NEG_INF)
            m = jnp.max(s, axis=1, keepdims=True)
            p = jnp.exp(s - m)
            l = jnp.sum(p, axis=1, keepdims=True)
            o = jnp.dot(p.astype(vw.dtype), vw, preferred_element_type=F32) * (1.0 / l)
            outs.append(o)
            lses.append(jnp.broadcast_to(m + jnp.log(l), (DIL_QSUB, LANES)))
        o_ref[pl.ds(q0, DIL_QSUB), :] = jnp.where(lo, outs[0], outs[1]).astype(o_ref.dtype)
        lse_ref[pl.ds(q0, DIL_QSUB), :] = jnp.where(lo, lses[0], lses[1])


def _dil_attention_group(proj, g, dilation, row_off, batch, seq, t_total):
    n_cols = proj.shape[1]
    sub_len = seq // dilation
    qblk = min(DIL_QBLK, sub_len)
    nqb = sub_len // qblk
    cb = n_cols // LANES
    hp = DIL_GROUP_WIDTH // LANES
    view = proj.reshape(t_total // dilation, dilation * n_cols)
    soff = row_off // seq

    def qmap(b, r, p, i):
        return ((soff + b) * nqb + i, r * cb + p)

    def kmap(b, r, p, i):
        return (soff + b, r * cb + hp + p)

    def vmap(b, r, p, i):
        return (soff + b, r * cb + 2 * hp + p)

    def omap(b, r, p, i):
        return (b * nqb + i, r * hp + p)

    out_rows = batch * sub_len
    o, lse = pl.pallas_call(
        functools.partial(_dil_attn_kernel, sub_len=sub_len, qblk=qblk),
        grid=(batch, dilation, hp, nqb),
        in_specs=[
            pl.BlockSpec((qblk, LANES), qmap),
            pl.BlockSpec((sub_len, LANES), kmap),
            pl.BlockSpec((sub_len, LANES), vmap),
        ],
        out_specs=[
            pl.BlockSpec((qblk, LANES), omap),
            pl.BlockSpec((qblk, LANES), omap),
        ],
        out_shape=[
            jax.ShapeDtypeStruct((out_rows, dilation * DIL_GROUP_WIDTH), BF16),
            jax.ShapeDtypeStruct((out_rows, dilation * DIL_GROUP_WIDTH), F32),
        ],
        compiler_params=_params("parallel", "parallel", "parallel", "arbitrary"),
        name=f"dil_attn_g{g}",
    )(view, view, view)
    return (o.reshape(batch * seq, DIL_GROUP_WIDTH), lse.reshape(batch * seq, DIL_GROUP_WIDTH))


POST_TM = 512


def _layer_norm_rows(z, g, b):
    mu = jnp.mean(z, axis=-1, keepdims=True)
    zc = z - mu
    var = jnp.mean(jnp.square(zc), axis=-1, keepdims=True)
    return zc * lax.rsqrt(var + LN_EPS) * g + b


def _post_kernel(*refs, n_groups):
    o_refs = refs[:n_groups]
    lse_refs = refs[n_groups:2 * n_groups] if n_groups > 1 else ()
    w_ref, x_ref, g_ref, b_ref, y_ref, yb_ref = refs[-6:]
    if n_groups == 1:
        o = o_refs[0][...]
    else:
        lses = [r[...] for r in lse_refs]
        m = functools.reduce(jnp.maximum, lses)
        es = [jnp.exp(l - m) for l in lses]
        inv = 1.0 / functools.reduce(jnp.add, es)
        acc = None
        for e, o_ref in zip(es, o_refs):
            term = (e * inv) * o_ref[...].astype(F32)
            acc = term if acc is None else acc + term
        o = acc.astype(BF16)
    h = jnp.dot(o, w_ref[...], preferred_element_type=F32)
    y = _layer_norm_rows(DEEPNORM_ALPHA * x_ref[...] + h, g_ref[...], b_ref[...])
    y_ref[...] = y
    yb_ref[...] = y.astype(BF16)


def _post_attention(os_, lses, w_out, x, g, b):
    t, d = x.shape
    n_groups = len(os_)
    row = pl.BlockSpec((POST_TM, d), lambda i: (i, 0))
    full = pl.BlockSpec((d, d), lambda i: (0, 0))
    vec = pl.BlockSpec((1, d), lambda i: (0, 0))
    return pl.pallas_call(
        functools.partial(_post_kernel, n_groups=n_groups),
        grid=(t // POST_TM,),
        in_specs=[row] * (n_groups + len(lses)) + [full, row, vec, vec],
        out_specs=[row, row],
        out_shape=[jax.ShapeDtypeStruct((t, d), F32), jax.ShapeDtypeStruct((t, d), BF16)],
        compiler_params=_params("parallel"),
        name="post_attn_ln",
    )(*os_, *lses, w_out, x, g, b)


ROUTER_TM = 512
ROUTER_EXPERT_LANE0 = MOE_GROUPS


def _first_argmax(vals, vmax, lane_f):
    return jnp.min(jnp.where(vals == vmax, lane_f, float(LANES)), axis=1, keepdims=True)


def _router_kernel(x_ref, w_ref, comb_ref):
    logits = jnp.dot(x_ref[...], w_ref[...], preferred_element_type=F32, precision=lax.Precision.HIGHEST)
    lane_f = lax.broadcasted_iota(jnp.int32, logits.shape, 1).astype(F32)
    gl = jnp.where(lane_f < MOE_GROUPS, logits, NEG_INF)
    gmax = jnp.max(gl, axis=1, keepdims=True)
    gw = 1.0 / jnp.sum(jnp.exp(gl - gmax), axis=1, keepdims=True)
    gsel = _first_argmax(gl, gmax, lane_f)
    e0 = ROUTER_EXPERT_LANE0 + MOE_EXPERTS * gsel
    el = jnp.where((lane_f >= e0) & (lane_f < e0 + MOE_EXPERTS), logits, NEG_INF)
    v1 = jnp.max(el, axis=1, keepdims=True)
    i1 = _first_argmax(el, v1, lane_f)
    el2 = jnp.where(lane_f == i1, NEG_INF, el)
    v2 = jnp.max(el2, axis=1, keepdims=True)
    i2 = _first_argmax(el2, v2, lane_f)
    e2 = jnp.exp(v2 - v1)
    inv = gw / (1.0 + e2)
    comb_ref[...] = jnp.where(lane_f == i1, inv, 0.0) + jnp.where(lane_f == i2, e2 * inv, 0.0)


def _router(x, w_router):
    t, d = x.shape
    return pl.pallas_call(
        _router_kernel,
        grid=(t // ROUTER_TM,),
        in_specs=[pl.BlockSpec((ROUTER_TM, d), lambda i: (i, 0)), pl.BlockSpec((d, LANES), lambda i: (0, 0))],
        out_specs=pl.BlockSpec((ROUTER_TM, LANES), lambda i: (i, 0)),
        out_shape=jax.ShapeDtypeStruct((t, LANES), F32),
        compiler_params=_params("parallel"),
        name="router",
    )(x, w_router)


MOE_TM = 512


def _moe_kernel(xb_ref, comb_ref, wg_ref, wu_ref, wd_ref, x_ref, g_ref, b_ref, y_ref, yb_ref, acc_ref):
    e = pl.program_id(1)

    @pl.when(e == 0)
    def _():
        acc_ref[...] = jnp.zeros(acc_ref.shape, F32)

    xb = xb_ref[...]
    comb = comb_ref[...]
    lane = lax.broadcasted_iota(jnp.int32, comb.shape, 1)
    ce = jnp.sum(jnp.where(lane == e + ROUTER_EXPERT_LANE0, comb, 0.0), axis=1, keepdims=True)
    gate = jnp.dot(xb, wg_ref[0], preferred_element_type=F32)
    up = jnp.dot(xb, wu_ref[0], preferred_element_type=F32)
    hid = (gate * jax.nn.sigmoid(gate)) * up * ce
    acc_ref[...] += jnp.dot(hid.astype(BF16), wd_ref[0], preferred_element_type=F32)

    @pl.when(e == pl.num_programs(1) - 1)
    def _():
        y = _layer_norm_rows(DEEPNORM_ALPHA * x_ref[...] + acc_ref[...], g_ref[...], b_ref[...])
        y_ref[...] = y
        yb_ref[...] = y.astype(BF16)


def _moe(xb, comb, w_gate, w_up, w_down, x, g, b):
    t, d = x.shape
    ne, _, f = w_gate.shape
    row = pl.BlockSpec((MOE_TM, d), lambda i, e: (i, 0))
    vec = pl.BlockSpec((1, d), lambda i, e: (0, 0))
    return pl.pallas_call(
        _moe_kernel,
        grid=(t // MOE_TM, ne),
        in_specs=[
            row,
            pl.BlockSpec((MOE_TM, LANES), lambda i, e: (i, 0)),
            pl.BlockSpec((1, d, f), lambda i, e: (e, 0, 0)),
            pl.BlockSpec((1, d, f), lambda i, e: (e, 0, 0)),
            pl.BlockSpec((1, f, d), lambda i, e: (e, 0, 0)),
            row, vec, vec,
        ],
        out_specs=[row, row],
        out_shape=[jax.ShapeDtypeStruct((t, d), F32), jax.ShapeDtypeStruct((t, d), BF16)],
        scratch_shapes=[pltpu.VMEM((MOE_TM, d), F32)],
        compiler_params=_params("parallel", "arbitrary"),
        name="moe_ln",
    )(xb, comb, w_gate, w_up, w_down, x, g, b)


def _da_weight_layout(w_in):
    d = w_in.shape[0]
    w = DA_QK_WIDTH
    q1, q2, k1, k2 = (w_in[:, i * w:(i + 1) * w].reshape(d, DA_HEADS, DA_HEAD_DIM) for i in range(4))
    qk = jnp.concatenate([jnp.concatenate([q1, q2], axis=2).reshape(d, 2 * w),
                          jnp.concatenate([k1, k2], axis=2).reshape(d, 2 * w)], axis=1)
    return qk, w_in[:, 4 * w:]


def _trunk(xs, da_w_in, da_w_out, da_lambda_q1, da_lambda_k1, da_lambda_q2, da_lambda_k2, da_subln_g,
           dl_w_in, dl_w_out, ln1_g, ln1_b, ln2_g, ln2_b,
           moe_router_group, moe_router_expert, moe_w_gate, moe_w_up, moe_w_down):
    d = xs[0].shape[-1]
    segs = []
    off = 0
    for x in xs:
        segs.append((off, x.shape[0], x.shape[1]))
        off += x.shape[0] * x.shape[1]
    t_total = off
    x = jnp.concatenate([v.reshape(-1, d) for v in xs], axis=0)
    xb = x.astype(BF16)

    max_len = max(s for _, _, s in segs)
    rope = _rope_tables(max_len)
    pos_blocks = np.concatenate([
        np.tile(np.arange(s // PROJ_TM, dtype=np.int32), b) for _, b, s in segs])
    pos_blocks = jnp.asarray(pos_blocks)

    for i in range(DEPTH):
        j = i // N_MIXERS
        if i % N_MIXERS == 0:
            lambda_init = 0.8 - 0.6 * math.exp(-0.3 * i)
            w_qk, w_v = _da_weight_layout(da_w_in[j])
            qk = _proj(xb, w_qk.astype(BF16), rope, pos_blocks, v_period=3, q_scale=DA_Q_SCALE)
            vt = _proj_t(xb, w_v.T.astype(BF16))
            lam_params = jnp.stack([da_lambda_q1[j], da_lambda_k1[j], da_lambda_q2[j], da_lambda_k2[j]]).astype(F32)
            g_col = da_subln_g[j].astype(F32).reshape(DA_V_DIM, 1)
            o = jnp.concatenate([
                _da_attention(qk, vt, lam_params, g_col, lambda_init, ro, b, s) for ro, b, s in segs], axis=0)
            os_, lses = [o], []
            w_out = da_w_out[j]
        else:
            os_, lses = [], []
            gw = 3 * DIL_GROUP_WIDTH
            for g, (window, dilation) in enumerate(DIL_PATTERNS):
                assert window // (2 * dilation) == DIL_HALF
                proj = _proj(xb, dl_w_in[j][:, g * gw:(g + 1) * gw].astype(BF16), rope, pos_blocks, v_period=3)
                parts = [_dil_attention_group(proj, g, dilation, ro, b, s, t_total) for ro, b, s in segs]
                os_.append(jnp.concatenate([p[0] for p in parts], axis=0))
                lses.append(jnp.concatenate([p[1] for p in parts], axis=0))
            w_out = dl_w_out[j]
        x, xb = _post_attention(os_, lses, w_out.astype(BF16), x,
                                ln1_g[i].reshape(1, d).astype(F32), ln1_b[i].reshape(1, d).astype(F32))

        w_router = jnp.concatenate([
            moe_router_group[i].astype(F32),
            moe_router_expert[i].reshape(d, N_EXPERTS).astype(F32),
            jnp.zeros((d, LANES - MOE_GROUPS - N_EXPERTS), F32)], axis=1)
        comb = _router(x, w_router)
        x, xb = _moe(xb, comb,
                     moe_w_gate[i].reshape(N_EXPERTS, d, MOE_FF).astype(BF16),
                     moe_w_up[i].reshape(N_EXPERTS, d, MOE_FF).astype(BF16),
                     moe_w_down[i].reshape(N_EXPERTS, MOE_FF, d).astype(BF16),
                     x, ln2_g[i].reshape(1, d).astype(F32), ln2_b[i].reshape(1, d).astype(F32))

    outs = []
    for (ro, b, s), v in zip(segs, xs):
        outs.append(x[ro:ro + b * s].reshape(v.shape))
    return outs


def kernel(x_prompt, x_sample, da_w_in, da_w_out, da_lambda_q1, da_lambda_k1, da_lambda_q2, da_lambda_k2, da_subln_g, dl_w_in, dl_w_out, ln1_g, ln1_b, ln2_g, ln2_b, moe_router_group, moe_router_expert, moe_w_gate, moe_w_up, moe_w_down):
    y_prompt, y_sample = _trunk(
        [x_prompt, x_sample], da_w_in, da_w_out, da_lambda_q1, da_lambda_k1, da_lambda_q2, da_lambda_k2,
        da_subln_g, dl_w_in, dl_w_out, ln1_g, ln1_b, ln2_g, ln2_b,
        moe_router_group, moe_router_expert, moe_w_gate, moe_w_up, moe_w_down)
    return (y_prompt, y_sample)
```

```python
import functools
import math

import jax
import jax.numpy as jnp
import numpy as np
from jax import lax
from jax.experimental import pallas as pl
from jax.experimental.pallas import tpu as pltpu

F32 = jnp.float32
BF16 = jnp.bfloat16

LANES = 128
VMEM_LIMIT_BYTES = 56 * 1024 * 1024

D_MODEL = 1024
DEPTH = 4
N_MIXERS = 2

DA_HEADS = 8
DA_HEAD_DIM = 64
DA_V_DIM = 2 * DA_HEAD_DIM
DA_QK_WIDTH = DA_HEADS * DA_HEAD_DIM

DIL_PATTERNS = ((128, 1), (512, 4), (2048, 16))
DIL_GROUPS = len(DIL_PATTERNS)
DIL_HEADS = 16
DIL_HEAD_DIM = 64
DIL_GROUP_WIDTH = DIL_HEADS * DIL_HEAD_DIM
DIL_HALF = 64

ROPE_THETA = 500000.0
ROPE_ROT = DA_HEAD_DIM // 4
ROPE_HALF = ROPE_ROT // 2

LN_EPS = 1e-5
DEEPNORM_ALPHA = (2.0 * DEPTH) ** 0.25

MOE_GROUPS = 4
MOE_EXPERTS = 4
MOE_FF = 512
N_EXPERTS = MOE_GROUPS * MOE_EXPERTS

NEG_INF = -1e30
LOG2_E = math.log2(math.e)
QK_SCALE_LOG2 = DA_HEAD_DIM ** -0.5 * LOG2_E


def _params(*sem):
    return pltpu.CompilerParams(dimension_semantics=sem, vmem_limit_bytes=VMEM_LIMIT_BYTES)


PROJ_TM = 1024
PROJ_TN = 1024


def _rope_tables(max_len):
    inv = ROPE_THETA ** (-jnp.arange(ROPE_HALF, dtype=F32) * (2.0 / ROPE_ROT))
    ang = jnp.arange(max_len, dtype=F32)[:, None] * inv[None, :]
    cos, sin = jnp.cos(ang), jnp.sin(ang)
    ones = jnp.ones((max_len, DA_HEAD_DIM - ROPE_ROT), F32)
    zeros_h = jnp.zeros((max_len, ROPE_HALF), F32)
    zeros_r = jnp.zeros((max_len, DA_HEAD_DIM - ROPE_ROT), F32)
    c = jnp.concatenate([cos, cos, ones], axis=1)
    s_up = jnp.concatenate([zeros_h, sin, zeros_r], axis=1)
    s_dn = jnp.concatenate([-sin, zeros_h, zeros_r], axis=1)
    reps = LANES // DA_HEAD_DIM
    return jnp.stack([jnp.tile(c, (1, reps)), jnp.tile(s_up, (1, reps)), jnp.tile(s_dn, (1, reps))])


PROJ_CHUNK = 256


def _proj_kernel(pos_ref, x_ref, w_ref, rope_ref, o_ref, ybuf_ref, *, n_cols, q_cols, rope_cols, q_scale, dilation):
    del pos_ref
    x = x_ref[...]
    c = rope_ref[0]
    s_up = rope_ref[1]
    s_dn = rope_ref[2]
    rows = x.shape[0] // dilation
    for k in range(n_cols // PROJ_CHUNK):
        c0 = k * PROJ_CHUNK
        y = jnp.dot(x, w_ref[:, c0:c0 + PROJ_CHUNK], preferred_element_type=F32)
        if c0 < q_cols:
            y = y * q_scale
        parts = []
        for h in range(PROJ_CHUNK // LANES):
            yc = y[:, h * LANES:(h + 1) * LANES]
            if c0 < rope_cols:
                yc = yc * c + pltpu.roll(yc, ROPE_HALF, 1) * s_up + pltpu.roll(yc, LANES - ROPE_HALF, 1) * s_dn
            parts.append(yc)
        if dilation == 1:
            for h, yc in enumerate(parts):
                o_ref[0, :, c0 + h * LANES:c0 + (h + 1) * LANES] = yc.astype(o_ref.dtype)
        else:
            for h, yc in enumerate(parts):
                slot = (2 * k + h) % ybuf_ref.shape[0]
                ybuf_ref[slot] = yc
                for r in range(dilation):
                    o_ref[r, :, c0 + h * LANES:c0 + (h + 1) * LANES] = (
                        ybuf_ref[slot, pl.ds(r, rows, stride=dilation), :].astype(o_ref.dtype))


def _proj(xb, w, rope, pos_blocks, *, q_cols, rope_cols, q_scale, dilation=1):
    t, d = xb.shape
    n = w.shape[1]
    rows = PROJ_TM // dilation
    grid_spec = pltpu.PrefetchScalarGridSpec(
        num_scalar_prefetch=1,
        grid=(t // PROJ_TM,),
        in_specs=[
            pl.BlockSpec((PROJ_TM, d), lambda i, pos: (i, 0)),
            pl.BlockSpec((d, n), lambda i, pos: (0, 0)),
            pl.BlockSpec((3, PROJ_TM, LANES), lambda i, pos: (0, pos[i], 0)),
        ],
        out_specs=pl.BlockSpec((dilation, rows, n), lambda i, pos: (0, i, 0)),
        scratch_shapes=[pltpu.VMEM((4, PROJ_TM, LANES), F32)],
    )
    return pl.pallas_call(
        functools.partial(_proj_kernel, n_cols=n, q_cols=q_cols, rope_cols=rope_cols, q_scale=q_scale,
                          dilation=dilation),
        grid_spec=grid_spec,
        out_shape=jax.ShapeDtypeStruct((dilation, t // dilation, n), BF16),
        compiler_params=_params("parallel"),
        name="proj_rope",
    )(pos_blocks, xb, w, rope)


DA_TK = 512
DA_ONES_ROWS = 16
DA_VT_ROWS = DA_V_DIM + DA_ONES_ROWS


def _projt_kernel(x_ref, wt_ref, o_ref):
    y = lax.dot_general(wt_ref[...], x_ref[...], (((1,), (1,)), ((), ())), preferred_element_type=F32)
    h = o_ref.shape[0]
    o_ref[:, 0, 0:DA_V_DIM, :] = y.reshape(h, DA_V_DIM, DA_TK).astype(o_ref.dtype)
    o_ref[:, 0, DA_V_DIM:DA_VT_ROWS, :] = jnp.ones((h, DA_ONES_ROWS, DA_TK), o_ref.dtype)


def _proj_t(xb, wt):
    t, d = xb.shape
    n = wt.shape[0]
    h = n // DA_V_DIM
    return pl.pallas_call(
        _projt_kernel,
        grid=(t // DA_TK,),
        in_specs=[
            pl.BlockSpec((DA_TK, d), lambda i: (i, 0)),
            pl.BlockSpec((n, d), lambda i: (0, 0)),
        ],
        out_specs=pl.BlockSpec((h, 1, DA_VT_ROWS, DA_TK), lambda i: (0, i, 0, 0)),
        out_shape=jax.ShapeDtypeStruct((h, t // DA_TK, DA_VT_ROWS, DA_TK), BF16),
        compiler_params=_params("parallel"),
        name="proj_vt",
    )(xb, wt)


DA_TQ = 512


def _da_attn_kernel(lam_ref, g_ref, q_ref, k_ref, vt_ref, *rest, n_kblk, lambda_init, has_prev):
    if has_prev:
        rest = rest[1:]
    (o_ref, q1_ref, q2_ref, sa1_ref, sa2_ref, sb1_ref, sb2_ref, m1_ref, a1_ref, m2_ref, a2_ref) = rest
    q = q_ref[...]
    lane = lax.broadcasted_iota(jnp.int32, q.shape, 1)
    zero = jnp.zeros_like(q)
    q1_ref[...] = jnp.where(lane < DA_HEAD_DIM, q, zero)
    q2_ref[...] = jnp.where(lane >= DA_HEAD_DIM, q, zero)
    for m_ref, a_ref in ((m1_ref, a1_ref), (m2_ref, a2_ref)):
        m_ref[...] = jnp.full(m_ref.shape, NEG_INF, F32)
        a_ref[...] = jnp.zeros(a_ref.shape, F32)

    def scores(j, s1_ref, s2_ref):
        k = k_ref[pl.ds(pl.multiple_of(j * DA_TK, DA_TK), DA_TK), :]
        nt = (((1,), (1,)), ((), ()))
        s1_ref[...] = lax.dot_general(k, q1_ref[...], nt, preferred_element_type=F32)
        s2_ref[...] = lax.dot_general(k, q2_ref[...], nt, preferred_element_type=F32)

    def softmax_pv(j, s1_ref, s2_ref):
        vt = vt_ref[0, j]
        for s_ref, m_ref, a_ref in ((s1_ref, m1_ref, a1_ref), (s2_ref, m2_ref, a2_ref)):
            s = s_ref[...]
            m_old = m_ref[...]
            m_new = jnp.maximum(m_old, jnp.max(s, axis=0, keepdims=True))
            p = jnp.exp2(s - m_new).astype(vt.dtype)
            a_ref[...] = jnp.exp2(m_old - m_new) * a_ref[...] + jnp.dot(vt, p, preferred_element_type=F32)
            m_ref[...] = m_new

    scores(0, sa1_ref, sa2_ref)

    def pair(jj, carry):
        j = 2 * jj
        scores(j + 1, sb1_ref, sb2_ref)
        softmax_pv(j, sa1_ref, sa2_ref)
        scores(j + 2, sa1_ref, sa2_ref)
        softmax_pv(j + 1, sb1_ref, sb2_ref)
        return carry

    lax.fori_loop(0, n_kblk // 2 - 1, pair, 0)
    scores(n_kblk - 1, sb1_ref, sb2_ref)
    softmax_pv(n_kblk - 2, sa1_ref, sa2_ref)
    softmax_pv(n_kblk - 1, sb1_ref, sb2_ref)

    lp = lam_ref[...]
    lam = (jnp.exp(jnp.sum(lp[0:1] * lp[1:2], axis=1, keepdims=True))
           - jnp.exp(jnp.sum(lp[2:3] * lp[3:4], axis=1, keepdims=True)) + lambda_init)
    a1 = a1_ref[...]
    a2 = a2_ref[...]
    o = (a1[0:DA_V_DIM] * (1.0 / a1[DA_V_DIM:DA_V_DIM + 1])
         - lam * (a2[0:DA_V_DIM] * (1.0 / a2[DA_V_DIM:DA_V_DIM + 1])))
    o = o * lax.rsqrt(jnp.mean(jnp.square(o), axis=0, keepdims=True) + LN_EPS)
    o = o * g_ref[...] * (1.0 - lambda_init)
    o_ref[...] = o.T.astype(o_ref.dtype)


def _da_attention(qk, vt, lam_params, subln_g, lambda_init, row_off, batch, seq, prev_out):
    h = DA_HEADS
    t_total = qk.shape[1]
    nq = seq // DA_TQ
    nk = seq // DA_TK
    assert nk % 2 == 0 and nk >= 2
    qoff = row_off // DA_TQ
    soff = row_off // seq
    s_buf = pltpu.VMEM((DA_TK, DA_TQ), F32)
    in_specs = [
        pl.BlockSpec((4, DA_HEAD_DIM), lambda b, hh, i: (0, 0)),
        pl.BlockSpec((DA_V_DIM, 1), lambda b, hh, i: (0, 0)),
        pl.BlockSpec((None, DA_TQ, LANES), lambda b, hh, i: (0, qoff + b * nq + i, hh)),
        pl.BlockSpec((None, seq, LANES), lambda b, hh, i: (0, soff + b, h + hh)),
        pl.BlockSpec((1, nk, DA_VT_ROWS, DA_TK), lambda b, hh, i: (hh, soff + b, 0, 0)),
    ]
    operands = [lam_params, subln_g, qk, qk, vt]
    aliases = {}
    if prev_out is not None:
        in_specs.append(pl.BlockSpec(memory_space=pl.ANY))
        operands.append(prev_out)
        aliases = {5: 0}
    return pl.pallas_call(
        functools.partial(_da_attn_kernel, n_kblk=nk, lambda_init=lambda_init, has_prev=prev_out is not None),
        grid=(batch, h, nq),
        in_specs=in_specs,
        out_specs=pl.BlockSpec((DA_TQ, DA_V_DIM), lambda b, hh, i: (qoff + b * nq + i, hh)),
        out_shape=jax.ShapeDtypeStruct((t_total, h * DA_V_DIM), BF16),
        input_output_aliases=aliases,
        scratch_shapes=[
            pltpu.VMEM((DA_TQ, LANES), BF16), pltpu.VMEM((DA_TQ, LANES), BF16),
            s_buf, s_buf, s_buf, s_buf,
            pltpu.VMEM((1, DA_TQ), F32), pltpu.VMEM((DA_VT_ROWS, DA_TQ), F32),
            pltpu.VMEM((1, DA_TQ), F32), pltpu.VMEM((DA_VT_ROWS, DA_TQ), F32),
        ],
        compiler_params=_params("parallel", "parallel", "arbitrary"),
        name="da_attn",
    )(*operands)


DIL_QSUB = 128
DIL_KWIN = DIL_QSUB + 2 * DIL_HALF
DIL_QBLK = 1024


def _dil_attn_kernel(q_ref, k_ref, v_ref, *rest, sub_len, qblk, has_prev):
    o_ref, lse_ref = rest[2:] if has_prev else rest
    qi = pl.program_id(3)
    lane = lax.broadcasted_iota(jnp.int32, (DIL_QSUB, LANES), 1)
    lo = lane < DIL_HEAD_DIM
    row = lax.broadcasted_iota(jnp.int32, (DIL_QSUB, DIL_KWIN), 0)
    col = lax.broadcasted_iota(jnp.int32, (DIL_QSUB, DIL_KWIN), 1)

    for n in range(qblk // DIL_QSUB):
        q0 = n * DIL_QSUB
        gq0 = qi * qblk + q0
        ws = jnp.clip(gq0 - DIL_HALF, 0, sub_len - DIL_KWIN)
        ws = pl.multiple_of(ws, DIL_HALF)
        qb = q_ref[pl.ds(q0, DIL_QSUB), :]
        kw = k_ref[pl.ds(ws, DIL_KWIN), :]
        vw = v_ref[pl.ds(ws, DIL_KWIN), :]
        valid = jnp.abs((ws + col) - (gq0 + row)) <= DIL_HALF
        zero = jnp.zeros_like(qb)
        outs, lses = [], []
        for qm in (jnp.where(lo, qb, zero), jnp.where(lo, zero, qb)):
            s = lax.dot_general(qm, kw, (((1,), (1,)), ((), ())), preferred_element_type=F32)
            s = jnp.where(valid, s, NEG_INF)
            m = jnp.max(s, axis=1, keepdims=True)
            p = jnp.exp2(s - m)
            l = jnp.sum(p, axis=1, keepdims=True)
            o = jnp.dot(p.astype(vw.dtype), vw, preferred_element_type=F32) * (1.0 / l)
            outs.append(o)
            lses.append(jnp.broadcast_to(m + jnp.log(l) * LOG2_E, (DIL_QSUB, LANES)))
        o_ref[pl.ds(q0, DIL_QSUB), :] = jnp.where(lo, outs[0], outs[1]).astype(o_ref.dtype)
        lse_ref[pl.ds(q0, DIL_QSUB), :] = jnp.where(lo, lses[0], lses[1])


def _dil_attention_group(proj, g, dilation, row_off, batch, seq, prev):
    rows_total = proj.shape[1]
    sub_len = seq // dilation
    qblk = min(DIL_QBLK, sub_len)
    nqb = sub_len // qblk
    hp = DIL_GROUP_WIDTH // LANES
    soff = row_off // seq

    def qmap(b, r, p, i):
        return (r, (soff + b) * nqb + i, p)

    def kmap(b, r, p, i):
        return (r, soff + b, hp + p)

    def vmap(b, r, p, i):
        return (r, soff + b, 2 * hp + p)

    in_specs = [
        pl.BlockSpec((None, qblk, LANES), qmap),
        pl.BlockSpec((None, sub_len, LANES), kmap),
        pl.BlockSpec((None, sub_len, LANES), vmap),
    ]
    operands = [proj, proj, proj]
    aliases = {}
    if prev is not None:
        in_specs += [pl.BlockSpec(memory_space=pl.ANY)] * 2
        operands += list(prev)
        aliases = {3: 0, 4: 1}
    return pl.pallas_call(
        functools.partial(_dil_attn_kernel, sub_len=sub_len, qblk=qblk, has_prev=prev is not None),
        grid=(batch, dilation, hp, nqb),
        in_specs=in_specs,
        out_specs=[
            pl.BlockSpec((None, qblk, LANES), qmap),
            pl.BlockSpec((None, qblk, LANES), qmap),
        ],
        out_shape=[
            jax.ShapeDtypeStruct((dilation, rows_total, DIL_GROUP_WIDTH), BF16),
            jax.ShapeDtypeStruct((dilation, rows_total, DIL_GROUP_WIDTH), F32),
        ],
        input_output_aliases=aliases,
        compiler_params=_params("parallel", "parallel", "parallel", "arbitrary"),
        name=f"dil_attn_g{g}",
    )(*operands)


POST_TM = 512


def _layer_norm_rows(z, g, b):
    mu = jnp.mean(z, axis=-1, keepdims=True)
    zc = z - mu
    var = jnp.mean(jnp.square(zc), axis=-1, keepdims=True)
    return zc * lax.rsqrt(var + LN_EPS) * g + b


def _token_order(ref, scratch_ref):
    dil, rows, width = ref.shape
    if dil == 1:
        return ref[0].astype(F32)
    n_chunks = width // LANES
    for r in range(dil):
        v = ref[r].astype(F32)
        for c in range(n_chunks):
            scratch_ref[c, pl.ds(r, rows, stride=dil), :] = v[:, c * LANES:(c + 1) * LANES]
    return jnp.concatenate([scratch_ref[c] for c in range(n_chunks)], axis=1)


def _post_kernel(*refs, n_groups):
    o_refs = refs[:n_groups]
    lse_refs = refs[n_groups:2 * n_groups] if n_groups > 1 else ()
    n_in = n_groups + len(lse_refs)
    w_ref, x_ref, g_ref, b_ref, y_ref, yb_ref = refs[n_in:n_in + 6]
    scratch = refs[n_in + 6:]
    if n_groups == 1:
        o = o_refs[0][0]
    else:
        lses = [_token_order(r, s) for r, s in zip(lse_refs, scratch[:n_groups])]
        m = functools.reduce(jnp.maximum, lses)
        es = [jnp.exp2(l - m) for l in lses]
        inv = 1.0 / functools.reduce(jnp.add, es)
        acc = None
        for e, o_ref in zip(es, o_refs):
            term = (e * inv) * _token_order(o_ref, scratch[n_groups])
            acc = term if acc is None else acc + term
        o = acc.astype(BF16)
    h = jnp.dot(o, w_ref[...], preferred_element_type=F32)
    y = _layer_norm_rows(DEEPNORM_ALPHA * x_ref[...] + h, g_ref[...], b_ref[...])
    y_ref[...] = y
    yb_ref[...] = y.astype(BF16)


def _post_attention(os_, lses, w_out, x, g, b):
    t, d = x.shape
    n_groups = len(os_)
    row = pl.BlockSpec((POST_TM, d), lambda i: (i, 0))
    full = pl.BlockSpec((d, d), lambda i: (0, 0))
    vec = pl.BlockSpec((1, d), lambda i: (0, 0))
    grouped = [pl.BlockSpec((a.shape[0], POST_TM // a.shape[0], d), lambda i: (0, i, 0)) for a in (*os_, *lses)]
    n_scratch = n_groups + 1 if n_groups > 1 else 0
    return pl.pallas_call(
        functools.partial(_post_kernel, n_groups=n_groups),
        grid=(t // POST_TM,),
        in_specs=grouped + [full, row, vec, vec],
        out_specs=[row, row],
        out_shape=[jax.ShapeDtypeStruct((t, d), F32), jax.ShapeDtypeStruct((t, d), BF16)],
        scratch_shapes=[pltpu.VMEM((d // LANES, POST_TM, LANES), F32)] * n_scratch,
        compiler_params=_params("parallel"),
        name="post_attn_ln",
    )(*os_, *lses, w_out, x, g, b)


ROUTER_TM = 512
ROUTER_EXPERT_LANE0 = MOE_GROUPS


def _first_argmax(vals, vmax, lane_f):
    return jnp.min(jnp.where(vals == vmax, lane_f, float(LANES)), axis=1, keepdims=True)


def _router_kernel(x_ref, w_ref, comb_ref):
    logits = jnp.dot(x_ref[...], w_ref[...], preferred_element_type=F32, precision=lax.Precision.HIGHEST)
    lane_f = lax.broadcasted_iota(jnp.int32, logits.shape, 1).astype(F32)
    gl = jnp.where(lane_f < MOE_GROUPS, logits, NEG_INF)
    gmax = jnp.max(gl, axis=1, keepdims=True)
    gw = 1.0 / jnp.sum(jnp.exp(gl - gmax), axis=1, keepdims=True)
    gsel = _first_argmax(gl, gmax, lane_f)
    e0 = ROUTER_EXPERT_LANE0 + MOE_EXPERTS * gsel
    el = jnp.where((lane_f >= e0) & (lane_f < e0 + MOE_EXPERTS), logits, NEG_INF)
    v1 = jnp.max(el, axis=1, keepdims=True)
    i1 = _first_argmax(el, v1, lane_f)
    el2 = jnp.where(lane_f == i1, NEG_INF, el)
    v2 = jnp.max(el2, axis=1, keepdims=True)
    i2 = _first_argmax(el2, v2, lane_f)
    e2 = jnp.exp(v2 - v1)
    inv = gw / (1.0 + e2)
    comb_ref[...] = jnp.where(lane_f == i1, inv, 0.0) + jnp.where(lane_f == i2, e2 * inv, 0.0)


def _router(x, w_router):
    t, d = x.shape
    return pl.pallas_call(
        _router_kernel,
        grid=(t // ROUTER_TM,),
        in_specs=[pl.BlockSpec((ROUTER_TM, d), lambda i: (i, 0)), pl.BlockSpec((d, LANES), lambda i: (0, 0))],
        out_specs=pl.BlockSpec((ROUTER_TM, LANES), lambda i: (i, 0)),
        out_shape=jax.ShapeDtypeStruct((t, LANES), F32),
        compiler_params=_params("parallel"),
        name="router",
    )(x, w_router)


MOE_TM = 512


def _moe_kernel(xb_ref, comb_ref, wg_ref, wu_ref, wd_ref, x_ref, g_ref, b_ref, y_ref, yb_ref, acc_ref):
    e = pl.program_id(1)

    @pl.when(e == 0)
    def _():
        acc_ref[...] = jnp.zeros(acc_ref.shape, F32)

    xb = xb_ref[...]
    comb = comb_ref[...]
    lane = lax.broadcasted_iota(jnp.int32, comb.shape, 1)
    ce = jnp.sum(jnp.where(lane == e + ROUTER_EXPERT_LANE0, comb, 0.0), axis=1, keepdims=True)
    gate = jnp.dot(xb, wg_ref[0], preferred_element_type=F32)
    up = jnp.dot(xb, wu_ref[0], preferred_element_type=F32)
    hid = (gate * jax.nn.sigmoid(gate)) * up * ce
    acc_ref[...] += jnp.dot(hid.astype(BF16), wd_ref[0], preferred_element_type=F32)

    @pl.when(e == pl.num_programs(1) - 1)
    def _():
        y = _layer_norm_rows(DEEPNORM_ALPHA * x_ref[...] + acc_ref[...], g_ref[...], b_ref[...])
        y_ref[...] = y
        yb_ref[...] = y.astype(BF16)


def _moe(xb, comb, w_gate, w_up, w_down, x, g, b):
    t, d = x.shape
    ne, _, f = w_gate.shape
    row = pl.BlockSpec((MOE_TM, d), lambda i, e: (i, 0))
    vec = pl.BlockSpec((1, d), lambda i, e: (0, 0))
    return pl.pallas_call(
        _moe_kernel,
        grid=(t // MOE_TM, ne),
        in_specs=[
            row,
            pl.BlockSpec((MOE_TM, LANES), lambda i, e: (i, 0)),
            pl.BlockSpec((1, d, f), lambda i, e: (e, 0, 0)),
            pl.BlockSpec((1, d, f), lambda i, e: (e, 0, 0)),
            pl.BlockSpec((1, f, d), lambda i, e: (e, 0, 0)),
            row, vec, vec,
        ],
        out_specs=[row, row],
        out_shape=[jax.ShapeDtypeStruct((t, d), F32), jax.ShapeDtypeStruct((t, d), BF16)],
        scratch_shapes=[pltpu.VMEM((MOE_TM, d), F32)],
        compiler_params=_params("parallel", "arbitrary"),
        name="moe_ln",
    )(xb, comb, w_gate, w_up, w_down, x, g, b)


def _da_weight_layout(w_in):
    d = w_in.shape[0]
    w = DA_QK_WIDTH
    q1, q2, k1, k2 = (w_in[:, i * w:(i + 1) * w].reshape(d, DA_HEADS, DA_HEAD_DIM) for i in range(4))
    qk = jnp.concatenate([jnp.concatenate([q1, q2], axis=2).reshape(d, 2 * w),
                          jnp.concatenate([k1, k2], axis=2).reshape(d, 2 * w)], axis=1)
    return qk, w_in[:, 4 * w:]


def _trunk(xs, da_w_in, da_w_out, da_lambda_q1, da_lambda_k1, da_lambda_q2, da_lambda_k2, da_subln_g,
           dl_w_in, dl_w_out, ln1_g, ln1_b, ln2_g, ln2_b,
           moe_router_group, moe_router_expert, moe_w_gate, moe_w_up, moe_w_down):
    d = xs[0].shape[-1]
    segs = []
    off = 0
    for x in xs:
        segs.append((off, x.shape[0], x.shape[1]))
        off += x.shape[0] * x.shape[1]
    t_total = off
    x = jnp.concatenate([v.reshape(-1, d) for v in xs], axis=0)
    xb = x.astype(BF16)

    max_len = max(s for _, _, s in segs)
    rope = _rope_tables(max_len)
    pos_blocks = np.concatenate([
        np.tile(np.arange(s // PROJ_TM, dtype=np.int32), b) for _, b, s in segs])
    pos_blocks = jnp.asarray(pos_blocks)

    for i in range(DEPTH):
        j = i // N_MIXERS
        if i % N_MIXERS == 0:
            lambda_init = 0.8 - 0.6 * math.exp(-0.3 * i)
            w_qk, w_v = _da_weight_layout(da_w_in[j])
            qk = _proj(xb, w_qk.astype(BF16), rope, pos_blocks,
                       q_cols=2 * DA_QK_WIDTH, rope_cols=4 * DA_QK_WIDTH, q_scale=QK_SCALE_LOG2)
            vt = _proj_t(xb, w_v.T.astype(BF16))
            lam_params = jnp.stack([da_lambda_q1[j], da_lambda_k1[j], da_lambda_q2[j], da_lambda_k2[j]]).astype(F32)
            g_col = da_subln_g[j].astype(F32).reshape(DA_V_DIM, 1)
            o = None
            for ro, b, s in segs:
                o = _da_attention(qk, vt, lam_params, g_col, lambda_init, ro, b, s, o)
            os_, lses = [o.reshape(1, t_total, d)], []
            w_out = da_w_out[j]
        else:
            os_, lses = [], []
            gw = 3 * DIL_GROUP_WIDTH
            for g, (window, dilation) in enumerate(DIL_PATTERNS):
                assert window // (2 * dilation) == DIL_HALF
                proj = _proj(xb, dl_w_in[j][:, g * gw:(g + 1) * gw].astype(BF16), rope, pos_blocks,
                             q_cols=DIL_GROUP_WIDTH, rope_cols=2 * DIL_GROUP_WIDTH, q_scale=QK_SCALE_LOG2,
                             dilation=dilation)
                pair = None
                for ro, b, s in segs:
                    pair = _dil_attention_group(proj, g, dilation, ro, b, s, pair)
                os_.append(pair[0])
                lses.append(pair[1])
            w_out = dl_w_out[j]
        x, xb = _post_attention(os_, lses, w_out.astype(BF16), x,
                                ln1_g[i].reshape(1, d).astype(F32), ln1_b[i].reshape(1, d).astype(F32))

        w_router = jnp.concatenate([
            moe_router_group[i].astype(F32),
            moe_router_expert[i].reshape(d, N_EXPERTS).astype(F32),
            jnp.zeros((d, LANES - MOE_GROUPS - N_EXPERTS), F32)], axis=1)
        comb = _router(x, w_router)
        x, xb = _moe(xb, comb,
                     moe_w_gate[i].reshape(N_EXPERTS, d, MOE_FF).astype(BF16),
                     moe_w_up[i].reshape(N_EXPERTS, d, MOE_FF).astype(BF16),
                     moe_w_down[i].reshape(N_EXPERTS, MOE_FF, d).astype(BF16),
                     x, ln2_g[i].reshape(1, d).astype(F32), ln2_b[i].reshape(1, d).astype(F32))

    outs = []
    for (ro, b, s), v in zip(segs, xs):
        outs.append(x[ro:ro + b * s].reshape(v.shape))
    return outs


def kernel(x_prompt, x_sample, da_w_in, da_w_out, da_lambda_q1, da_lambda_k1, da_lambda_q2, da_lambda_k2, da_subln_g, dl_w_in, dl_w_out, ln1_g, ln1_b, ln2_g, ln2_b, moe_router_group, moe_router_expert, moe_w_gate, moe_w_up, moe_w_down):
    y_prompt, y_sample = _trunk(
        [x_prompt, x_sample], da_w_in, da_w_out, da_lambda_q1, da_lambda_k1, da_lambda_q2, da_lambda_k2,
        da_subln_g, dl_w_in, dl_w_out, ln1_g, ln1_b, ln2_g, ln2_b,
        moe_router_group, moe_router_expert, moe_w_gate, moe_w_up, moe_w_down)
    return (y_prompt, y_sample)
```

```python
import functools
import math

import jax
import jax.numpy as jnp
import numpy as np
from jax import lax
from jax.experimental import pallas as pl
from jax.experimental.pallas import tpu as pltpu

F32 = jnp.float32
BF16 = jnp.bfloat16

LANES = 128
VMEM_LIMIT_BYTES = 56 * 1024 * 1024

D_MODEL = 1024
DEPTH = 4
N_MIXERS = 2

DA_HEADS = 8
DA_HEAD_DIM = 64
DA_V_DIM = 2 * DA_HEAD_DIM
DA_QK_WIDTH = DA_HEADS * DA_HEAD_DIM

DIL_PATTERNS = ((128, 1), (512, 4), (2048, 16))
DIL_GROUPS = len(DIL_PATTERNS)
DIL_HEADS = 16
DIL_HEAD_DIM = 64
DIL_GROUP_WIDTH = DIL_HEADS * DIL_HEAD_DIM
DIL_HALF = 64

ROPE_THETA = 500000.0
ROPE_ROT = DA_HEAD_DIM // 4
ROPE_HALF = ROPE_ROT // 2

LN_EPS = 1e-5
DEEPNORM_ALPHA = (2.0 * DEPTH) ** 0.25

MOE_GROUPS = 4
MOE_EXPERTS = 4
MOE_FF = 512
N_EXPERTS = MOE_GROUPS * MOE_EXPERTS

NEG_INF = -1e30
LOG2_E = math.log2(math.e)
QK_SCALE_LOG2 = DA_HEAD_DIM ** -0.5 * LOG2_E


def _params(*sem):
    return pltpu.CompilerParams(dimension_semantics=sem, vmem_limit_bytes=VMEM_LIMIT_BYTES)


PROJ_TM = 1024
PROJ_TN = 1024


def _rope_tables(max_len):
    inv = ROPE_THETA ** (-jnp.arange(ROPE_HALF, dtype=F32) * (2.0 / ROPE_ROT))
    ang = jnp.arange(max_len, dtype=F32)[:, None] * inv[None, :]
    cos, sin = jnp.cos(ang), jnp.sin(ang)
    ones = jnp.ones((max_len, DA_HEAD_DIM - ROPE_ROT), F32)
    zeros_h = jnp.zeros((max_len, ROPE_HALF), F32)
    zeros_r = jnp.zeros((max_len, DA_HEAD_DIM - ROPE_ROT), F32)
    c = jnp.concatenate([cos, cos, ones], axis=1)
    s_up = jnp.concatenate([zeros_h, sin, zeros_r], axis=1)
    s_dn = jnp.concatenate([-sin, zeros_h, zeros_r], axis=1)
    reps = LANES // DA_HEAD_DIM
    return jnp.stack([jnp.tile(c, (1, reps)), jnp.tile(s_up, (1, reps)), jnp.tile(s_dn, (1, reps))])


PROJ_CHUNK = 256


def _proj_kernel(pos_ref, x_ref, w_ref, rope_ref, o_ref, ybuf_ref, *, n_cols, q_cols, rope_cols, q_scale, dilation):
    del pos_ref
    x = x_ref[...]
    c = rope_ref[0]
    s_up = rope_ref[1]
    s_dn = rope_ref[2]
    rows = x.shape[0] // dilation
    for k in range(n_cols // PROJ_CHUNK):
        c0 = k * PROJ_CHUNK
        y = jnp.dot(x, w_ref[:, c0:c0 + PROJ_CHUNK], preferred_element_type=F32)
        if c0 < q_cols:
            y = y * q_scale
        parts = []
        for h in range(PROJ_CHUNK // LANES):
            yc = y[:, h * LANES:(h + 1) * LANES]
            if c0 < rope_cols:
                yc = yc * c + pltpu.roll(yc, ROPE_HALF, 1) * s_up + pltpu.roll(yc, LANES - ROPE_HALF, 1) * s_dn
            parts.append(yc)
        if dilation == 1:
            for h, yc in enumerate(parts):
                o_ref[0, :, c0 + h * LANES:c0 + (h + 1) * LANES] = yc.astype(o_ref.dtype)
        else:
            for h, yc in enumerate(parts):
                slot = (2 * k + h) % ybuf_ref.shape[0]
                ybuf_ref[slot] = yc
                for r in range(dilation):
                    o_ref[r, :, c0 + h * LANES:c0 + (h + 1) * LANES] = (
                        ybuf_ref[slot, pl.ds(r, rows, stride=dilation), :].astype(o_ref.dtype))


def _proj(xb, w, rope, pos_blocks, *, q_cols, rope_cols, q_scale, dilation=1):
    t, d = xb.shape
    n = w.shape[1]
    rows = PROJ_TM // dilation
    grid_spec = pltpu.PrefetchScalarGridSpec(
        num_scalar_prefetch=1,
        grid=(t // PROJ_TM,),
        in_specs=[
            pl.BlockSpec((PROJ_TM, d), lambda i, pos: (i, 0)),
            pl.BlockSpec((d, n), lambda i, pos: (0, 0)),
            pl.BlockSpec((3, PROJ_TM, LANES), lambda i, pos: (0, pos[i], 0)),
        ],
        out_specs=pl.BlockSpec((dilation, rows, n), lambda i, pos: (0, i, 0)),
        scratch_shapes=[pltpu.VMEM((4, PROJ_TM, LANES), F32)],
    )
    return pl.pallas_call(
        functools.partial(_proj_kernel, n_cols=n, q_cols=q_cols, rope_cols=rope_cols, q_scale=q_scale,
                          dilation=dilation),
        grid_spec=grid_spec,
        out_shape=jax.ShapeDtypeStruct((dilation, t // dilation, n), BF16),
        compiler_params=_params("parallel"),
        name="proj_rope",
    )(pos_blocks, xb, w, rope)


DA_TK = 512
DA_ONES_ROWS = 16
DA_VT_ROWS = DA_V_DIM + DA_ONES_ROWS


def _projt_kernel(x_ref, wt_ref, o_ref):
    y = lax.dot_general(wt_ref[...], x_ref[...], (((1,), (1,)), ((), ())), preferred_element_type=F32)
    h = o_ref.shape[0]
    o_ref[:, 0, 0:DA_V_DIM, :] = y.reshape(h, DA_V_DIM, DA_TK).astype(o_ref.dtype)
    o_ref[:, 0, DA_V_DIM:DA_VT_ROWS, :] = jnp.ones((h, DA_ONES_ROWS, DA_TK), o_ref.dtype)


def _proj_t(xb, wt):
    t, d = xb.shape
    n = wt.shape[0]
    h = n // DA_V_DIM
    return pl.pallas_call(
        _projt_kernel,
        grid=(t // DA_TK,),
        in_specs=[
            pl.BlockSpec((DA_TK, d), lambda i: (i, 0)),
            pl.BlockSpec((n, d), lambda i: (0, 0)),
        ],
        out_specs=pl.BlockSpec((h, 1, DA_VT_ROWS, DA_TK), lambda i: (0, i, 0, 0)),
        out_shape=jax.ShapeDtypeStruct((h, t // DA_TK, DA_VT_ROWS, DA_TK), BF16),
        compiler_params=_params("parallel"),
        name="proj_vt",
    )(xb, wt)


DA_TQ = 512


def _da_attn_kernel(lam_ref, g_ref, q_ref, k_ref, vt_ref, *rest, n_kblk, lambda_init, has_prev):
    if has_prev:
        rest = rest[1:]
    (o_ref, q1_ref, q2_ref, sa1_ref, sa2_ref, sb1_ref, sb2_ref, m1_ref, a1_ref, m2_ref, a2_ref) = rest
    q = q_ref[...]
    lane = lax.broadcasted_iota(jnp.int32, q.shape, 1)
    zero = jnp.zeros_like(q)
    q1_ref[...] = jnp.where(lane < DA_HEAD_DIM, q, zero)
    q2_ref[...] = jnp.where(lane >= DA_HEAD_DIM, q, zero)
    for m_ref, a_ref in ((m1_ref, a1_ref), (m2_ref, a2_ref)):
        m_ref[...] = jnp.full(m_ref.shape, NEG_INF, F32)
        a_ref[...] = jnp.zeros(a_ref.shape, F32)

    def scores(j, s1_ref, s2_ref):
        k = k_ref[pl.ds(pl.multiple_of(j * DA_TK, DA_TK), DA_TK), :]
        nt = (((1,), (1,)), ((), ()))
        s1_ref[...] = lax.dot_general(k, q1_ref[...], nt, preferred_element_type=F32)
        s2_ref[...] = lax.dot_general(k, q2_ref[...], nt, preferred_element_type=F32)

    def softmax_pv(j, s1_ref, s2_ref):
        vt = vt_ref[0, j]
        for s_ref, m_ref, a_ref in ((s1_ref, m1_ref, a1_ref), (s2_ref, m2_ref, a2_ref)):
            s = s_ref[...]
            m_old = m_ref[...]
            m_new = jnp.maximum(m_old, jnp.max(s, axis=0, keepdims=True))
            p = jnp.exp2(s - m_new).astype(vt.dtype)
            a_ref[...] = jnp.exp2(m_old - m_new) * a_ref[...] + jnp.dot(vt, p, preferred_element_type=F32)
            m_ref[...] = m_new

    scores(0, sa1_ref, sa2_ref)

    def pair(jj, carry):
        j = 2 * jj
        scores(j + 1, sb1_ref, sb2_ref)
        softmax_pv(j, sa1_ref, sa2_ref)
        scores(j + 2, sa1_ref, sa2_ref)
        softmax_pv(j + 1, sb1_ref, sb2_ref)
        return carry

    lax.fori_loop(0, n_kblk // 2 - 1, pair, 0)
    scores(n_kblk - 1, sb1_ref, sb2_ref)
    softmax_pv(n_kblk - 2, sa1_ref, sa2_ref)
    softmax_pv(n_kblk - 1, sb1_ref, sb2_ref)

    lp = lam_ref[...]
    lam = (jnp.exp(jnp.sum(lp[0:1] * lp[1:2], axis=1, keepdims=True))
           - jnp.exp(jnp.sum(lp[2:3] * lp[3:4], axis=1, keepdims=True)) + lambda_init)
    a1 = a1_ref[...]
    a2 = a2_ref[...]
    o = (a1[0:DA_V_DIM] * (1.0 / a1[DA_V_DIM:DA_V_DIM + 1])
         - lam * (a2[0:DA_V_DIM] * (1.0 / a2[DA_V_DIM:DA_V_DIM + 1])))
    o = o * lax.rsqrt(jnp.mean(jnp.square(o), axis=0, keepdims=True) + LN_EPS)
    o = o * g_ref[...] * (1.0 - lambda_init)
    o_ref[...] = o.T.astype(o_ref.dtype)


def _da_attention(qk, vt, lam_params, subln_g, lambda_init, row_off, batch, seq, prev_out):
    h = DA_HEADS
    t_total = qk.shape[1]
    nq = seq // DA_TQ
    nk = seq // DA_TK
    assert nk % 2 == 0 and nk >= 2
    qoff = row_off // DA_TQ
    soff = row_off // seq
    s_buf = pltpu.VMEM((DA_TK, DA_TQ), F32)
    in_specs = [
        pl.BlockSpec((4, DA_HEAD_DIM), lambda b, hh, i: (0, 0)),
        pl.BlockSpec((DA_V_DIM, 1), lambda b, hh, i: (0, 0)),
        pl.BlockSpec((None, DA_TQ, LANES), lambda b, hh, i: (0, qoff + b * nq + i, hh)),
        pl.BlockSpec((None, seq, LANES), lambda b, hh, i: (0, soff + b, h + hh)),
        pl.BlockSpec((1, nk, DA_VT_ROWS, DA_TK), lambda b, hh, i: (hh, soff + b, 0, 0)),
    ]
    operands = [lam_params, subln_g, qk, qk, vt]
    aliases = {}
    if prev_out is not None:
        in_specs.append(pl.BlockSpec(memory_space=pl.ANY))
        operands.append(prev_out)
        aliases = {5: 0}
    return pl.pallas_call(
        functools.partial(_da_attn_kernel, n_kblk=nk, lambda_init=lambda_init, has_prev=prev_out is not None),
        grid=(batch, h, nq),
        in_specs=in_specs,
        out_specs=pl.BlockSpec((DA_TQ, DA_V_DIM), lambda b, hh, i: (qoff + b * nq + i, hh)),
        out_shape=jax.ShapeDtypeStruct((t_total, h * DA_V_DIM), BF16),
        input_output_aliases=aliases,
        scratch_shapes=[
            pltpu.VMEM((DA_TQ, LANES), BF16), pltpu.VMEM((DA_TQ, LANES), BF16),
            s_buf, s_buf, s_buf, s_buf,
            pltpu.VMEM((1, DA_TQ), F32), pltpu.VMEM((DA_VT_ROWS, DA_TQ), F32),
            pltpu.VMEM((1, DA_TQ), F32), pltpu.VMEM((DA_VT_ROWS, DA_TQ), F32),
        ],
        compiler_params=_params("parallel", "parallel", "arbitrary"),
        name="da_attn",
    )(*operands)


DIL_QSUB = 128
DIL_KWIN = DIL_QSUB + 2 * DIL_HALF
DIL_QBLK = 1024


def _dil_attn_kernel(q_ref, k_ref, v_ref, *rest, sub_len, qblk, has_prev):
    o_ref, lse_ref = rest[2:] if has_prev else rest
    qi = pl.program_id(3)
    lane = lax.broadcasted_iota(jnp.int32, (DIL_QSUB, LANES), 1)
    lo = lane < DIL_HEAD_DIM
    row = lax.broadcasted_iota(jnp.int32, (DIL_QSUB, DIL_KWIN), 0)
    col = lax.broadcasted_iota(jnp.int32, (DIL_QSUB, DIL_KWIN), 1)

    for n in range(qblk // DIL_QSUB):
        q0 = n * DIL_QSUB
        gq0 = qi * qblk + q0
        ws = jnp.clip(gq0 - DIL_HALF, 0, sub_len - DIL_KWIN)
        ws = pl.multiple_of(ws, DIL_HALF)
        qb = q_ref[pl.ds(q0, DIL_QSUB), :]
        kw = k_ref[pl.ds(ws, DIL_KWIN), :]
        vw = v_ref[pl.ds(ws, DIL_KWIN), :]
        valid = jnp.abs((ws + col) - (gq0 + row)) <= DIL_HALF
        zero = jnp.zeros_like(qb)
        outs, lses = [], []
        for qm in (jnp.where(lo, qb, zero), jnp.where(lo, zero, qb)):
            s = lax.dot_general(qm, kw, (((1,), (1,)), ((), ())), preferred_element_type=F32)
            s = jnp.where(valid, s, NEG_INF)
            m = jnp.max(s, axis=1, keepdims=True)
            p = jnp.exp2(s - m)
            l = jnp.sum(p, axis=1, keepdims=True)
            o = jnp.dot(p.astype(vw.dtype), vw, preferred_element_type=F32) * (1.0 / l)
            outs.append(o)
            lses.append(jnp.broadcast_to(m + jnp.log(l) * LOG2_E, (DIL_QSUB, LANES)))
        o_ref[pl.ds(q0, DIL_QSUB), :] = jnp.where(lo, outs[0], outs[1]).astype(o_ref.dtype)
        lse_ref[pl.ds(q0, DIL_QSUB), :] = jnp.where(lo, lses[0], lses[1])


def _dil_attention_group(proj, g, dilation, row_off, batch, seq, prev):
    rows_total = proj.shape[1]
    sub_len = seq // dilation
    qblk = min(DIL_QBLK, sub_len)
    nqb = sub_len // qblk
    hp = DIL_GROUP_WIDTH // LANES
    soff = row_off // seq

    def qmap(b, r, p, i):
        return (r, (soff + b) * nqb + i, p)

    def kmap(b, r, p, i):
        return (r, soff + b, hp + p)

    def vmap(b, r, p, i):
        return (r, soff + b, 2 * hp + p)

    in_specs = [
        pl.BlockSpec((None, qblk, LANES), qmap),
        pl.BlockSpec((None, sub_len, LANES), kmap),
        pl.BlockSpec((None, sub_len, LANES), vmap),
    ]
    operands = [proj, proj, proj]
    aliases = {}
    if prev is not None:
        in_specs += [pl.BlockSpec(memory_space=pl.ANY)] * 2
        operands += list(prev)
        aliases = {3: 0, 4: 1}
    return pl.pallas_call(
        functools.partial(_dil_attn_kernel, sub_len=sub_len, qblk=qblk, has_prev=prev is not None),
        grid=(batch, dilation, hp, nqb),
        in_specs=in_specs,
        out_specs=[
            pl.BlockSpec((None, qblk, LANES), qmap),
            pl.BlockSpec((None, qblk, LANES), qmap),
        ],
        out_shape=[
            jax.ShapeDtypeStruct((dilation, rows_total, DIL_GROUP_WIDTH), BF16),
            jax.ShapeDtypeStruct((dilation, rows_total, DIL_GROUP_WIDTH), F32),
        ],
        input_output_aliases=aliases,
        compiler_params=_params("parallel", "parallel", "parallel", "arbitrary"),
        name=f"dil_attn_g{g}",
    )(*operands)


POST_TM = 512


def _layer_norm_rows(z, g, b):
    mu = jnp.mean(z, axis=-1, keepdims=True)
    zc = z - mu
    var = jnp.mean(jnp.square(zc), axis=-1, keepdims=True)
    return zc * lax.rsqrt(var + LN_EPS) * g + b


def _token_order(ref, scratch_ref):
    dil, rows, width = ref.shape
    if dil == 1:
        return ref[0].astype(F32)
    n_chunks = width // LANES
    for r in range(dil):
        v = ref[r].astype(F32)
        for c in range(n_chunks):
            scratch_ref[c, pl.ds(r, rows, stride=dil), :] = v[:, c * LANES:(c + 1) * LANES]
    return jnp.concatenate([scratch_ref[c] for c in range(n_chunks)], axis=1)


def _post_kernel(*refs, n_groups):
    o_refs = refs[:n_groups]
    lse_refs = refs[n_groups:2 * n_groups] if n_groups > 1 else ()
    n_in = n_groups + len(lse_refs)
    w_ref, x_ref, g_ref, b_ref, y_ref, yb_ref = refs[n_in:n_in + 6]
    scratch = refs[n_in + 6:]
    if n_groups == 1:
        o = o_refs[0][0]
    else:
        lses = [_token_order(r, s) for r, s in zip(lse_refs, scratch[:n_groups])]
        m = functools.reduce(jnp.maximum, lses)
        es = [jnp.exp2(l - m) for l in lses]
        inv = 1.0 / functools.reduce(jnp.add, es)
        acc = None
        for e, o_ref in zip(es, o_refs):
            term = (e * inv) * _token_order(o_ref, scratch[n_groups])
            acc = term if acc is None else acc + term
        o = acc.astype(BF16)
    h = jnp.dot(o, w_ref[...], preferred_element_type=F32)
    y = _layer_norm_rows(DEEPNORM_ALPHA * x_ref[...] + h, g_ref[...], b_ref[...])
    y_ref[...] = y
    yb_ref[...] = y.astype(BF16)


def _post_attention(os_, lses, w_out, x, g, b):
    t, d = x.shape
    n_groups = len(os_)
    row = pl.BlockSpec((POST_TM, d), lambda i: (i, 0))
    full = pl.BlockSpec((d, d), lambda i: (0, 0))
    vec = pl.BlockSpec((1, d), lambda i: (0, 0))
    grouped = [pl.BlockSpec((a.shape[0], POST_TM // a.shape[0], d), lambda i: (0, i, 0)) for a in (*os_, *lses)]
    n_scratch = n_groups + 1 if n_groups > 1 else 0
    return pl.pallas_call(
        functools.partial(_post_kernel, n_groups=n_groups),
        grid=(t // POST_TM,),
        in_specs=grouped + [full, row, vec, vec],
        out_specs=[row, row],
        out_shape=[jax.ShapeDtypeStruct((t, d), F32), jax.ShapeDtypeStruct((t, d), BF16)],
        scratch_shapes=[pltpu.VMEM((d // LANES, POST_TM, LANES), F32)] * n_scratch,
        compiler_params=_params("parallel"),
        name="post_attn_ln",
    )(*os_, *lses, w_out, x, g, b)


ROUTER_TM = 512
ROUTER_EXPERT_LANE0 = MOE_GROUPS
ROUTER_GROUP_LANE = 0


def _first_argmax(vals, vmax, lane_f):
    return jnp.min(jnp.where(vals == vmax, lane_f, float(LANES)), axis=1, keepdims=True)


def _router_kernel(x_ref, w_ref, xc_ref):
    x = x_ref[...]
    d = x.shape[1]
    logits = jnp.dot(x, w_ref[...], preferred_element_type=F32, precision=lax.Precision.HIGHEST)
    lane_f = lax.broadcasted_iota(jnp.int32, logits.shape, 1).astype(F32)
    gl = jnp.where(lane_f < MOE_GROUPS, logits, NEG_INF)
    gmax = jnp.max(gl, axis=1, keepdims=True)
    gw = 1.0 / jnp.sum(jnp.exp(gl - gmax), axis=1, keepdims=True)
    gsel = _first_argmax(gl, gmax, lane_f)
    e0 = ROUTER_EXPERT_LANE0 + MOE_EXPERTS * gsel
    el = jnp.where((lane_f >= e0) & (lane_f < e0 + MOE_EXPERTS), logits, NEG_INF)
    v1 = jnp.max(el, axis=1, keepdims=True)
    i1 = _first_argmax(el, v1, lane_f)
    el2 = jnp.where(lane_f == i1, NEG_INF, el)
    v2 = jnp.max(el2, axis=1, keepdims=True)
    i2 = _first_argmax(el2, v2, lane_f)
    e2 = jnp.exp(v2 - v1)
    inv = gw / (1.0 + e2)
    comb = jnp.where(lane_f == i1, inv, 0.0) + jnp.where(lane_f == i2, e2 * inv, 0.0)
    xc_ref[:, 0:d] = x
    xc_ref[:, d:d + LANES] = comb + jnp.where(lane_f == ROUTER_GROUP_LANE, gsel, 0.0)


def _router(x, w_router):
    t, d = x.shape
    return pl.pallas_call(
        _router_kernel,
        grid=(t // ROUTER_TM,),
        in_specs=[pl.BlockSpec((ROUTER_TM, d), lambda i: (i, 0)), pl.BlockSpec((d, LANES), lambda i: (0, 0))],
        out_specs=pl.BlockSpec((ROUTER_TM, d + LANES), lambda i: (i, 0)),
        out_shape=jax.ShapeDtypeStruct((t, d + LANES), F32),
        compiler_params=_params("parallel"),
        name="router",
    )(x, w_router)


MOE_TM = 512
GATHER_UNROLL = 8


def _issue_row_gather(idx_ref, src_hbm, dst_ref, sem, n_rows):
    def body(r, carry):
        pltpu.make_async_copy(src_hbm.at[pl.ds(idx_ref[0, r], 1), :], dst_ref.at[pl.ds(r, 1), :], sem).start()
        return carry
    lax.fori_loop(0, n_rows, body, 0, unroll=GATHER_UNROLL)


def _wait_row_gather(src_hbm, dst_ref, sem, n_rows):
    pltpu.make_async_copy(src_hbm.at[pl.ds(0, n_rows), :], dst_ref, sem).wait()


def _sort_by_group(gid, t, n_pad_rows):
    groups = jnp.arange(MOE_GROUPS, dtype=jnp.int32)
    onehot = (gid[:, None] == groups[None, :]).astype(jnp.int32)
    csum = jnp.cumsum(onehot, axis=0)
    counts = csum[-1]
    rank = jnp.sum(onehot * csum, axis=1) - 1
    tiles_g = (counts + MOE_TM - 1) // MOE_TM
    tile_end = jnp.cumsum(tiles_g)
    row_start = (tile_end - tiles_g) * MOE_TM
    pos = jnp.sum(onehot * row_start[None, :], axis=1) + rank
    src = jnp.zeros((n_pad_rows,), jnp.int32).at[pos].set(jnp.arange(t, dtype=jnp.int32))
    tile_ids = jnp.arange(n_pad_rows // MOE_TM, dtype=jnp.int32)
    tile_group = jnp.minimum(jnp.sum((tile_ids[:, None] >= tile_end[None, :]).astype(jnp.int32), axis=1),
                             MOE_GROUPS - 1)
    return pos, src, tile_group, tile_end[-1:]


def _moe_ffn_kernel(tg_ref, nv_ref, idx_cur_ref, idx_nxt_ref, xc_hbm, wg_ref, wu_ref, wd_ref, f_ref,
                    xbuf_ref, sem_ref):
    i = pl.program_id(0)
    slot = i % 2
    n_valid = nv_ref[0]
    d = f_ref.shape[1]

    @pl.when(i == 0)
    def _():
        _issue_row_gather(idx_cur_ref, xc_hbm, xbuf_ref.at[0], sem_ref.at[0], MOE_TM)

    @pl.when(i + 1 < n_valid)
    def _():
        _issue_row_gather(idx_nxt_ref, xc_hbm, xbuf_ref.at[1 - slot], sem_ref.at[1 - slot], MOE_TM)

    @pl.when(i < n_valid)
    def _():
        _wait_row_gather(xc_hbm, xbuf_ref.at[slot], sem_ref.at[slot], MOE_TM)
        xb = xbuf_ref[slot, :, 0:d].astype(BF16)
        comb = xbuf_ref[slot, :, d:d + LANES]
        lane = lax.broadcasted_iota(jnp.int32, comb.shape, 1)
        lane0 = ROUTER_EXPERT_LANE0 + MOE_EXPERTS * tg_ref[i]
        acc = None
        for e in range(MOE_EXPERTS):
            ce = jnp.sum(jnp.where(lane == lane0 + e, comb, 0.0), axis=1, keepdims=True)
            gate = jnp.dot(xb, wg_ref[0, e], preferred_element_type=F32)
            up = jnp.dot(xb, wu_ref[0, e], preferred_element_type=F32)
            hid = (gate * jax.nn.sigmoid(gate)) * up * ce
            part = jnp.dot(hid.astype(BF16), wd_ref[0, e], preferred_element_type=F32)
            acc = part if acc is None else acc + part
        f_ref[...] = acc

    @pl.when(i >= n_valid)
    def _():
        f_ref[...] = jnp.zeros(f_ref.shape, F32)


def _moe_ffn(xc, src, tile_group, n_valid, w_gate, w_up, w_down):
    d = xc.shape[1] - LANES
    n_tiles = tile_group.shape[0]
    _, ne, _, f = w_gate.shape
    idx = src.reshape(n_tiles, 1, MOE_TM)
    grid_spec = pltpu.PrefetchScalarGridSpec(
        num_scalar_prefetch=2,
        grid=(n_tiles,),
        in_specs=[
            pl.BlockSpec((None, 1, MOE_TM), lambda i, tg, nv: (i, 0, 0), memory_space=pltpu.SMEM),
            pl.BlockSpec((None, 1, MOE_TM), lambda i, tg, nv: (jnp.minimum(i + 1, n_tiles - 1), 0, 0),
                         memory_space=pltpu.SMEM),
            pl.BlockSpec(memory_space=pl.ANY),
            pl.BlockSpec((1, ne, d, f), lambda i, tg, nv: (tg[i], 0, 0, 0)),
            pl.BlockSpec((1, ne, d, f), lambda i, tg, nv: (tg[i], 0, 0, 0)),
            pl.BlockSpec((1, ne, f, d), lambda i, tg, nv: (tg[i], 0, 0, 0)),
        ],
        out_specs=pl.BlockSpec((MOE_TM, d), lambda i, tg, nv: (i, 0)),
        scratch_shapes=[pltpu.VMEM((2, MOE_TM, d + LANES), F32), pltpu.SemaphoreType.DMA((2,))],
    )
    return pl.pallas_call(
        _moe_ffn_kernel,
        grid_spec=grid_spec,
        out_shape=jax.ShapeDtypeStruct((n_tiles * MOE_TM, d), F32),
        compiler_params=_params("arbitrary"),
        name="moe_ffn",
    )(tile_group, n_valid, idx, idx, xc, w_gate, w_up, w_down)


LN2_TM = 512


def _ln2_kernel(idx_cur_ref, idx_nxt_ref, f_hbm, x_ref, g_ref, b_ref, y_ref, yb_ref, fbuf_ref, sem_ref):
    i = pl.program_id(0)
    slot = i % 2

    @pl.when(i == 0)
    def _():
        _issue_row_gather(idx_cur_ref, f_hbm, fbuf_ref.at[0], sem_ref.at[0], LN2_TM)

    @pl.when(i + 1 < pl.num_programs(0))
    def _():
        _issue_row_gather(idx_nxt_ref, f_hbm, fbuf_ref.at[1 - slot], sem_ref.at[1 - slot], LN2_TM)

    _wait_row_gather(f_hbm, fbuf_ref.at[slot], sem_ref.at[slot], LN2_TM)
    y = _layer_norm_rows(DEEPNORM_ALPHA * x_ref[...] + fbuf_ref[slot], g_ref[...], b_ref[...])
    y_ref[...] = y
    yb_ref[...] = y.astype(BF16)


def _ln2_gather(f_sorted, pos, x, g, b):
    t, d = x.shape
    n_tiles = t // LN2_TM
    idx = pos.reshape(n_tiles, 1, LN2_TM)
    row = pl.BlockSpec((LN2_TM, d), lambda i: (i, 0))
    vec = pl.BlockSpec((1, d), lambda i: (0, 0))
    return pl.pallas_call(
        _ln2_kernel,
        grid=(n_tiles,),
        in_specs=[
            pl.BlockSpec((None, 1, LN2_TM), lambda i: (i, 0, 0), memory_space=pltpu.SMEM),
            pl.BlockSpec((None, 1, LN2_TM), lambda i: (jnp.minimum(i + 1, n_tiles - 1), 0, 0),
                         memory_space=pltpu.SMEM),
            pl.BlockSpec(memory_space=pl.ANY),
            row, vec, vec,
        ],
        out_specs=[row, row],
        out_shape=[jax.ShapeDtypeStruct((t, d), F32), jax.ShapeDtypeStruct((t, d), BF16)],
        scratch_shapes=[pltpu.VMEM((2, LN2_TM, d), F32), pltpu.SemaphoreType.DMA((2,))],
        compiler_params=_params("arbitrary"),
        name="ln2_gather",
    )(idx, idx, f_sorted, x, g, b)


def _da_weight_layout(w_in):
    d = w_in.shape[0]
    w = DA_QK_WIDTH
    q1, q2, k1, k2 = (w_in[:, i * w:(i + 1) * w].reshape(d, DA_HEADS, DA_HEAD_DIM) for i in range(4))
    qk = jnp.concatenate([jnp.concatenate([q1, q2], axis=2).reshape(d, 2 * w),
                          jnp.concatenate([k1, k2], axis=2).reshape(d, 2 * w)], axis=1)
    return qk, w_in[:, 4 * w:]


def _trunk(xs, da_w_in, da_w_out, da_lambda_q1, da_lambda_k1, da_lambda_q2, da_lambda_k2, da_subln_g,
           dl_w_in, dl_w_out, ln1_g, ln1_b, ln2_g, ln2_b,
           moe_router_group, moe_router_expert, moe_w_gate, moe_w_up, moe_w_down):
    d = xs[0].shape[-1]
    segs = []
    off = 0
    for x in xs:
        segs.append((off, x.shape[0], x.shape[1]))
        off += x.shape[0] * x.shape[1]
    t_total = off
    x = jnp.concatenate([v.reshape(-1, d) for v in xs], axis=0)
    xb = x.astype(BF16)

    max_len = max(s for _, _, s in segs)
    rope = _rope_tables(max_len)
    pos_blocks = np.concatenate([
        np.tile(np.arange(s // PROJ_TM, dtype=np.int32), b) for _, b, s in segs])
    pos_blocks = jnp.asarray(pos_blocks)

    for i in range(DEPTH):
        j = i // N_MIXERS
        if i % N_MIXERS == 0:
            lambda_init = 0.8 - 0.6 * math.exp(-0.3 * i)
            w_qk, w_v = _da_weight_layout(da_w_in[j])
            qk = _proj(xb, w_qk.astype(BF16), rope, pos_blocks,
                       q_cols=2 * DA_QK_WIDTH, rope_cols=4 * DA_QK_WIDTH, q_scale=QK_SCALE_LOG2)
            vt = _proj_t(xb, w_v.T.astype(BF16))
            lam_params = jnp.stack([da_lambda_q1[j], da_lambda_k1[j], da_lambda_q2[j], da_lambda_k2[j]]).astype(F32)
            g_col = da_subln_g[j].astype(F32).reshape(DA_V_DIM, 1)
            o = jnp.zeros((t_total, d), BF16) if len(segs) > 1 else None
            for ro, b, s in segs:
                o = _da_attention(qk, vt, lam_params, g_col, lambda_init, ro, b, s, o)
            os_, lses = [o.reshape(1, t_total, d)], []
            w_out = da_w_out[j]
        else:
            os_, lses = [], []
            gw = 3 * DIL_GROUP_WIDTH
            for g, (window, dilation) in enumerate(DIL_PATTERNS):
                assert window // (2 * dilation) == DIL_HALF
                proj = _proj(xb, dl_w_in[j][:, g * gw:(g + 1) * gw].astype(BF16), rope, pos_blocks,
                             q_cols=DIL_GROUP_WIDTH, rope_cols=2 * DIL_GROUP_WIDTH, q_scale=QK_SCALE_LOG2,
                             dilation=dilation)
                pair = None
                if len(segs) > 1:
                    shape = (dilation, t_total // dilation, DIL_GROUP_WIDTH)
                    pair = (jnp.zeros(shape, BF16), jnp.zeros(shape, F32))
                for ro, b, s in segs:
                    pair = _dil_attention_group(proj, g, dilation, ro, b, s, pair)
                os_.append(pair[0])
                lses.append(pair[1])
            w_out = dl_w_out[j]
        x, xb = _post_attention(os_, lses, w_out.astype(BF16), x,
                                ln1_g[i].reshape(1, d).astype(F32), ln1_b[i].reshape(1, d).astype(F32))

        w_router = jnp.concatenate([
            moe_router_group[i].astype(F32),
            moe_router_expert[i].reshape(d, N_EXPERTS).astype(F32),
            jnp.zeros((d, LANES - MOE_GROUPS - N_EXPERTS), F32)], axis=1)
        xc = _router(x, w_router)
        gid = xc[:, d + ROUTER_GROUP_LANE].astype(jnp.int32)
        pos, src, tile_group, n_valid = _sort_by_group(gid, t_total, t_total + MOE_GROUPS * MOE_TM)
        f_sorted = _moe_ffn(xc, src, tile_group, n_valid,
                            moe_w_gate[i].astype(BF16), moe_w_up[i].astype(BF16), moe_w_down[i].astype(BF16))
        x, xb = _ln2_gather(f_sorted, pos, x,
                            ln2_g[i].reshape(1, d).astype(F32), ln2_b[i].reshape(1, d).astype(F32))

    outs = []
    for (ro, b, s), v in zip(segs, xs):
        outs.append(x[ro:ro + b * s].reshape(v.shape))
    return outs


def kernel(x_prompt, x_sample, da_w_in, da_w_out, da_lambda_q1, da_lambda_k1, da_lambda_q2, da_lambda_k2, da_subln_g, dl_w_in, dl_w_out, ln1_g, ln1_b, ln2_g, ln2_b, moe_router_group, moe_router_expert, moe_w_gate, moe_w_up, moe_w_down):
    y_prompt, y_sample = _trunk(
        [x_prompt, x_sample], da_w_in, da_w_out, da_lambda_q1, da_lambda_k1, da_lambda_q2, da_lambda_k2,
        da_subln_g, dl_w_in, dl_w_out, ln1_g, ln1_b, ln2_g, ln2_b,
        moe_router_group, moe_router_expert, moe_w_gate, moe_w_up, moe_w_down)
    return (y_prompt, y_sample)
```

```python
import functools
import math

import jax
import jax.numpy as jnp
import numpy as np
from jax import lax
from jax.experimental import pallas as pl
from jax.experimental.pallas import tpu as pltpu

F32 = jnp.float32
BF16 = jnp.bfloat16

LANES = 128
VMEM_LIMIT_BYTES = 56 * 1024 * 1024

D_MODEL = 1024
DEPTH = 4
N_MIXERS = 2

DA_HEADS = 8
DA_HEAD_DIM = 64
DA_V_DIM = 2 * DA_HEAD_DIM
DA_QK_WIDTH = DA_HEADS * DA_HEAD_DIM

DIL_PATTERNS = ((128, 1), (512, 4), (2048, 16))
DIL_GROUPS = len(DIL_PATTERNS)
DIL_HEADS = 16
DIL_HEAD_DIM = 64
DIL_GROUP_WIDTH = DIL_HEADS * DIL_HEAD_DIM
DIL_HALF = 64

ROPE_THETA = 500000.0
ROPE_ROT = DA_HEAD_DIM // 4
ROPE_HALF = ROPE_ROT // 2

LN_EPS = 1e-5
DEEPNORM_ALPHA = (2.0 * DEPTH) ** 0.25

MOE_GROUPS = 4
MOE_EXPERTS = 4
MOE_FF = 512
N_EXPERTS = MOE_GROUPS * MOE_EXPERTS

NEG_INF = -1e30
LOG2_E = math.log2(math.e)
QK_SCALE_LOG2 = DA_HEAD_DIM ** -0.5 * LOG2_E


def _params(*sem):
    return pltpu.CompilerParams(dimension_semantics=sem, vmem_limit_bytes=VMEM_LIMIT_BYTES)


PROJ_TM = 1024
PROJ_TN = 1024


def _rope_tables(max_len):
    inv = ROPE_THETA ** (-jnp.arange(ROPE_HALF, dtype=F32) * (2.0 / ROPE_ROT))
    ang = jnp.arange(max_len, dtype=F32)[:, None] * inv[None, :]
    cos, sin = jnp.cos(ang), jnp.sin(ang)
    ones = jnp.ones((max_len, DA_HEAD_DIM - ROPE_ROT), F32)
    zeros_h = jnp.zeros((max_len, ROPE_HALF), F32)
    zeros_r = jnp.zeros((max_len, DA_HEAD_DIM - ROPE_ROT), F32)
    c = jnp.concatenate([cos, cos, ones], axis=1)
    s_up = jnp.concatenate([zeros_h, sin, zeros_r], axis=1)
    s_dn = jnp.concatenate([-sin, zeros_h, zeros_r], axis=1)
    reps = LANES // DA_HEAD_DIM
    return jnp.stack([jnp.tile(c, (1, reps)), jnp.tile(s_up, (1, reps)), jnp.tile(s_dn, (1, reps))])


PROJ_CHUNK = 256


def _proj_kernel(pos_ref, x_ref, w_ref, rope_ref, o_ref, ybuf_ref, *, n_cols, q_cols, rope_cols, q_scale, dilation):
    del pos_ref
    x = x_ref[...]
    c = rope_ref[0]
    s_up = rope_ref[1]
    s_dn = rope_ref[2]
    rows = x.shape[0] // dilation
    for k in range(n_cols // PROJ_CHUNK):
        c0 = k * PROJ_CHUNK
        y = jnp.dot(x, w_ref[:, c0:c0 + PROJ_CHUNK], preferred_element_type=F32)
        if c0 < q_cols:
            y = y * q_scale
        parts = []
        for h in range(PROJ_CHUNK // LANES):
            yc = y[:, h * LANES:(h + 1) * LANES]
            if c0 < rope_cols:
                yc = yc * c + pltpu.roll(yc, ROPE_HALF, 1) * s_up + pltpu.roll(yc, LANES - ROPE_HALF, 1) * s_dn
            parts.append(yc)
        if dilation == 1:
            for h, yc in enumerate(parts):
                o_ref[0, :, c0 + h * LANES:c0 + (h + 1) * LANES] = yc.astype(o_ref.dtype)
        else:
            for h, yc in enumerate(parts):
                slot = (2 * k + h) % ybuf_ref.shape[0]
                ybuf_ref[slot] = yc
                for r in range(dilation):
                    o_ref[r, :, c0 + h * LANES:c0 + (h + 1) * LANES] = (
                        ybuf_ref[slot, pl.ds(r, rows, stride=dilation), :].astype(o_ref.dtype))


def _proj(xb, w, rope, pos_blocks, *, q_cols, rope_cols, q_scale, dilation=1):
    t, d = xb.shape
    n = w.shape[1]
    rows = PROJ_TM // dilation
    grid_spec = pltpu.PrefetchScalarGridSpec(
        num_scalar_prefetch=1,
        grid=(t // PROJ_TM,),
        in_specs=[
            pl.BlockSpec((PROJ_TM, d), lambda i, pos: (i, 0)),
            pl.BlockSpec((d, n), lambda i, pos: (0, 0)),
            pl.BlockSpec((3, PROJ_TM, LANES), lambda i, pos: (0, pos[i], 0)),
        ],
        out_specs=pl.BlockSpec((dilation, rows, n), lambda i, pos: (0, i, 0)),
        scratch_shapes=[pltpu.VMEM((4, PROJ_TM, LANES), F32)],
    )
    return pl.pallas_call(
        functools.partial(_proj_kernel, n_cols=n, q_cols=q_cols, rope_cols=rope_cols, q_scale=q_scale,
                          dilation=dilation),
        grid_spec=grid_spec,
        out_shape=jax.ShapeDtypeStruct((dilation, t // dilation, n), BF16),
        compiler_params=_params("parallel"),
        name="proj_rope",
    )(pos_blocks, xb, w, rope)


DA_TK = 512
DA_ONES_ROWS = 16
DA_VT_ROWS = DA_V_DIM + DA_ONES_ROWS


def _projt_kernel(x_ref, wt_ref, o_ref):
    y = lax.dot_general(wt_ref[...], x_ref[...], (((1,), (1,)), ((), ())), preferred_element_type=F32)
    h = o_ref.shape[0]
    o_ref[:, 0, 0:DA_V_DIM, :] = y.reshape(h, DA_V_DIM, DA_TK).astype(o_ref.dtype)
    o_ref[:, 0, DA_V_DIM:DA_VT_ROWS, :] = jnp.ones((h, DA_ONES_ROWS, DA_TK), o_ref.dtype)


def _proj_t(xb, wt):
    t, d = xb.shape
    n = wt.shape[0]
    h = n // DA_V_DIM
    return pl.pallas_call(
        _projt_kernel,
        grid=(t // DA_TK,),
        in_specs=[
            pl.BlockSpec((DA_TK, d), lambda i: (i, 0)),
            pl.BlockSpec((n, d), lambda i: (0, 0)),
        ],
        out_specs=pl.BlockSpec((h, 1, DA_VT_ROWS, DA_TK), lambda i: (0, i, 0, 0)),
        out_shape=jax.ShapeDtypeStruct((h, t // DA_TK, DA_VT_ROWS, DA_TK), BF16),
        compiler_params=_params("parallel"),
        name="proj_vt",
    )(xb, wt)


DA_TQ = 512


def _da_attn_kernel(lam_ref, g_ref, q_ref, k_ref, vt_ref, *rest, n_kblk, lambda_init, has_prev):
    if has_prev:
        rest = rest[1:]
    (o_ref, q1_ref, q2_ref, sa1_ref, sa2_ref, sb1_ref, sb2_ref, xa1_ref, xa2_ref, xb1_ref, xb2_ref,
     m1_ref, a1_ref, m2_ref, a2_ref) = rest
    buf_a = ((sa1_ref, xa1_ref), (sa2_ref, xa2_ref))
    buf_b = ((sb1_ref, xb1_ref), (sb2_ref, xb2_ref))
    qt = q_ref[...].astype(F32).T
    row = lax.broadcasted_iota(jnp.int32, qt.shape, 0)
    q1_ref[...] = jnp.where(row < DA_HEAD_DIM, qt, 0.0).astype(q1_ref.dtype)
    q2_ref[...] = jnp.where(row >= DA_HEAD_DIM, qt, 0.0).astype(q2_ref.dtype)
    for m_ref, a_ref in ((m1_ref, a1_ref), (m2_ref, a2_ref)):
        m_ref[...] = jnp.full(m_ref.shape, NEG_INF, F32)
        a_ref[...] = jnp.zeros(a_ref.shape, F32)

    def scores(j, buf):
        k = k_ref[pl.ds(pl.multiple_of(j * DA_TK, DA_TK), DA_TK), :]
        for qm_ref, (s_ref, x_ref) in zip((q1_ref, q2_ref), buf):
            s = jnp.dot(k, qm_ref[...], preferred_element_type=F32)
            s_ref[...] = s
            x_ref[...] = jnp.max(s.reshape(DA_TK // 8, 8, DA_TQ), axis=0)

    def softmax_pv(j, buf):
        vt = vt_ref[0, j]
        for (s_ref, x_ref), m_ref, a_ref in zip(buf, (m1_ref, m2_ref), (a1_ref, a2_ref)):
            m_old = m_ref[...]
            m_new = jnp.maximum(m_old, jnp.max(x_ref[...], axis=0, keepdims=True))
            p = jnp.exp2(s_ref[...] - m_new).astype(vt.dtype)
            a_ref[...] = jnp.exp2(m_old - m_new) * a_ref[...] + jnp.dot(vt, p, preferred_element_type=F32)
            m_ref[...] = m_new

    scores(0, buf_a)

    def pair(jj, carry):
        j = 2 * jj
        scores(j + 1, buf_b)
        softmax_pv(j, buf_a)
        scores(j + 2, buf_a)
        softmax_pv(j + 1, buf_b)
        return carry

    lax.fori_loop(0, n_kblk // 2 - 1, pair, 0)
    scores(n_kblk - 1, buf_b)
    softmax_pv(n_kblk - 2, buf_a)
    softmax_pv(n_kblk - 1, buf_b)

    lp = lam_ref[...]
    lam = (jnp.exp(jnp.sum(lp[0:1] * lp[1:2], axis=1, keepdims=True))
           - jnp.exp(jnp.sum(lp[2:3] * lp[3:4], axis=1, keepdims=True)) + lambda_init)
    a1 = a1_ref[...]
    a2 = a2_ref[...]
    o = (a1[0:DA_V_DIM] * (1.0 / a1[DA_V_DIM:DA_V_DIM + 1])
         - lam * (a2[0:DA_V_DIM] * (1.0 / a2[DA_V_DIM:DA_V_DIM + 1])))
    o = o * lax.rsqrt(jnp.mean(jnp.square(o), axis=0, keepdims=True) + LN_EPS)
    o = o * g_ref[...] * (1.0 - lambda_init)
    o_ref[...] = o.T.astype(o_ref.dtype)


def _da_attention(qk, vt, lam_params, subln_g, lambda_init, row_off, batch, seq, prev_out):
    h = DA_HEADS
    t_total = qk.shape[1]
    nq = seq // DA_TQ
    nk = seq // DA_TK
    assert nk % 2 == 0 and nk >= 2
    qoff = row_off // DA_TQ
    soff = row_off // seq
    s_buf = pltpu.VMEM((DA_TK, DA_TQ), F32)
    x_buf = pltpu.VMEM((8, DA_TQ), F32)
    in_specs = [
        pl.BlockSpec((4, DA_HEAD_DIM), lambda b, hh, i: (0, 0)),
        pl.BlockSpec((DA_V_DIM, 1), lambda b, hh, i: (0, 0)),
        pl.BlockSpec((None, DA_TQ, LANES), lambda b, hh, i: (0, qoff + b * nq + i, hh)),
        pl.BlockSpec((None, seq, LANES), lambda b, hh, i: (0, soff + b, h + hh)),
        pl.BlockSpec((1, nk, DA_VT_ROWS, DA_TK), lambda b, hh, i: (hh, soff + b, 0, 0)),
    ]
    operands = [lam_params, subln_g, qk, qk, vt]
    aliases = {}
    if prev_out is not None:
        in_specs.append(pl.BlockSpec(memory_space=pl.ANY))
        operands.append(prev_out)
        aliases = {5: 0}
    return pl.pallas_call(
        functools.partial(_da_attn_kernel, n_kblk=nk, lambda_init=lambda_init, has_prev=prev_out is not None),
        grid=(batch, h, nq),
        in_specs=in_specs,
        out_specs=pl.BlockSpec((DA_TQ, DA_V_DIM), lambda b, hh, i: (qoff + b * nq + i, hh)),
        out_shape=jax.ShapeDtypeStruct((t_total, h * DA_V_DIM), BF16),
        input_output_aliases=aliases,
        scratch_shapes=[
            pltpu.VMEM((LANES, DA_TQ), BF16), pltpu.VMEM((LANES, DA_TQ), BF16),
            s_buf, s_buf, s_buf, s_buf,
            x_buf, x_buf, x_buf, x_buf,
            pltpu.VMEM((1, DA_TQ), F32), pltpu.VMEM((DA_VT_ROWS, DA_TQ), F32),
            pltpu.VMEM((1, DA_TQ), F32), pltpu.VMEM((DA_VT_ROWS, DA_TQ), F32),
        ],
        compiler_params=_params("parallel", "parallel", "arbitrary"),
        name="da_attn",
    )(*operands)


DIL_QSUB = 128
DIL_KWIN = DIL_QSUB + 2 * DIL_HALF
DIL_QBLK = 1024


def _dil_attn_kernel(q_ref, k_ref, v_ref, *rest, sub_len, qblk, has_prev):
    o_ref, lse_ref = rest[2:] if has_prev else rest
    qi = pl.program_id(3)
    lane = lax.broadcasted_iota(jnp.int32, (DIL_QSUB, LANES), 1)
    lo = lane < DIL_HEAD_DIM
    row = lax.broadcasted_iota(jnp.int32, (DIL_QSUB, DIL_KWIN), 0)
    col = lax.broadcasted_iota(jnp.int32, (DIL_QSUB, DIL_KWIN), 1)

    interior = jnp.abs(col - DIL_HALF - row) <= DIL_HALF
    n_sub = qblk // DIL_QSUB

    for n in range(n_sub):
        q0 = n * DIL_QSUB
        gq0 = qi * qblk + q0
        ws = jnp.clip(gq0 - DIL_HALF, 0, sub_len - DIL_KWIN)
        ws = pl.multiple_of(ws, DIL_HALF)
        qb = q_ref[pl.ds(q0, DIL_QSUB), :]
        kw = k_ref[pl.ds(ws, DIL_KWIN), :]
        vw = v_ref[pl.ds(ws, DIL_KWIN), :]
        if 0 < n < n_sub - 1:
            valid = interior
        else:
            valid = jnp.abs((ws + col) - (gq0 + row)) <= DIL_HALF
        zero = jnp.zeros_like(qb)
        outs, lses = [], []
        for qm in (jnp.where(lo, qb, zero), jnp.where(lo, zero, qb)):
            s = lax.dot_general(qm, kw, (((1,), (1,)), ((), ())), preferred_element_type=F32)
            s = jnp.where(valid, s, NEG_INF)
            m = jnp.max(s, axis=1, keepdims=True)
            p = jnp.exp2(s - m)
            l = jnp.sum(p, axis=1, keepdims=True)
            o = jnp.dot(p.astype(vw.dtype), vw, preferred_element_type=F32) * (1.0 / l)
            outs.append(o)
            lses.append(jnp.broadcast_to(m + jnp.log(l) * LOG2_E, (DIL_QSUB, LANES)))
        o_ref[pl.ds(q0, DIL_QSUB), :] = jnp.where(lo, outs[0], outs[1]).astype(o_ref.dtype)
        lse_ref[pl.ds(q0, DIL_QSUB), :] = jnp.where(lo, lses[0], lses[1])


def _dil_attention_group(proj, g, dilation, row_off, batch, seq, prev):
    rows_total = proj.shape[1]
    sub_len = seq // dilation
    qblk = min(DIL_QBLK, sub_len)
    nqb = sub_len // qblk
    hp = DIL_GROUP_WIDTH // LANES
    soff = row_off // seq

    def qmap(b, r, p, i):
        return (r, (soff + b) * nqb + i, p)

    def kmap(b, r, p, i):
        return (r, soff + b, hp + p)

    def vmap(b, r, p, i):
        return (r, soff + b, 2 * hp + p)

    in_specs = [
        pl.BlockSpec((None, qblk, LANES), qmap),
        pl.BlockSpec((None, sub_len, LANES), kmap),
        pl.BlockSpec((None, sub_len, LANES), vmap),
    ]
    operands = [proj, proj, proj]
    aliases = {}
    if prev is not None:
        in_specs += [pl.BlockSpec(memory_space=pl.ANY)] * 2
        operands += list(prev)
        aliases = {3: 0, 4: 1}
    return pl.pallas_call(
        functools.partial(_dil_attn_kernel, sub_len=sub_len, qblk=qblk, has_prev=prev is not None),
        grid=(batch, dilation, hp, nqb),
        in_specs=in_specs,
        out_specs=[
            pl.BlockSpec((None, qblk, LANES), qmap),
            pl.BlockSpec((None, qblk, LANES), qmap),
        ],
        out_shape=[
            jax.ShapeDtypeStruct((dilation, rows_total, DIL_GROUP_WIDTH), BF16),
            jax.ShapeDtypeStruct((dilation, rows_total, DIL_GROUP_WIDTH), F32),
        ],
        input_output_aliases=aliases,
        compiler_params=_params("parallel", "parallel", "parallel", "arbitrary"),
        name=f"dil_attn_g{g}",
    )(*operands)


POST_TM = 512


def _layer_norm_rows(z, g, b):
    mu = jnp.mean(z, axis=-1, keepdims=True)
    zc = z - mu
    var = jnp.mean(jnp.square(zc), axis=-1, keepdims=True)
    return zc * lax.rsqrt(var + LN_EPS) * g + b


def _token_order(ref, scratch_ref):
    dil, rows, width = ref.shape
    if dil == 1:
        return ref[0].astype(F32)
    n_chunks = width // LANES
    for r in range(dil):
        v = ref[r].astype(F32)
        for c in range(n_chunks):
            scratch_ref[c, pl.ds(r, rows, stride=dil), :] = v[:, c * LANES:(c + 1) * LANES]
    return jnp.concatenate([scratch_ref[c] for c in range(n_chunks)], axis=1)


def _post_kernel(*refs, n_groups):
    o_refs = refs[:n_groups]
    lse_refs = refs[n_groups:2 * n_groups] if n_groups > 1 else ()
    n_in = n_groups + len(lse_refs)
    w_ref, x_ref, g_ref, b_ref, wr_ref, xc_ref = refs[n_in:n_in + 6]
    scratch = refs[n_in + 6:]
    d = x_ref.shape[1]
    if n_groups == 1:
        o = o_refs[0][0]
    else:
        lses = [_token_order(r, s) for r, s in zip(lse_refs, scratch[:n_groups])]
        m = functools.reduce(jnp.maximum, lses)
        es = [jnp.exp2(l - m) for l in lses]
        inv = 1.0 / functools.reduce(jnp.add, es)
        acc = None
        for e, o_ref in zip(es, o_refs):
            term = (e * inv) * _token_order(o_ref, scratch[n_groups])
            acc = term if acc is None else acc + term
        o = acc.astype(BF16)
    h = jnp.dot(o, w_ref[...], preferred_element_type=F32)
    y = _layer_norm_rows(DEEPNORM_ALPHA * x_ref[...] + h, g_ref[...], b_ref[...])
    xc_ref[:, 0:d] = y
    xc_ref[:, d:d + LANES] = _routing_lanes(y, wr_ref[...])


def _post_attention(os_, lses, w_out, x, g, b, w_router):
    t, d = x.shape
    n_groups = len(os_)
    row = pl.BlockSpec((POST_TM, d), lambda i: (i, 0))
    full = pl.BlockSpec((d, d), lambda i: (0, 0))
    vec = pl.BlockSpec((1, d), lambda i: (0, 0))
    grouped = [pl.BlockSpec((a.shape[0], POST_TM // a.shape[0], d), lambda i: (0, i, 0)) for a in (*os_, *lses)]
    n_scratch = n_groups + 1 if n_groups > 1 else 0
    return pl.pallas_call(
        functools.partial(_post_kernel, n_groups=n_groups),
        grid=(t // POST_TM,),
        in_specs=grouped + [full, row, vec, vec, pl.BlockSpec((d, LANES), lambda i: (0, 0))],
        out_specs=pl.BlockSpec((POST_TM, d + LANES), lambda i: (i, 0)),
        out_shape=jax.ShapeDtypeStruct((t, d + LANES), F32),
        scratch_shapes=[pltpu.VMEM((d // LANES, POST_TM, LANES), F32)] * n_scratch,
        compiler_params=_params("parallel"),
        name="post_attn_ln",
    )(*os_, *lses, w_out, x, g, b, w_router)


ROUTER_TM = 512
ROUTER_EXPERT_LANE0 = MOE_GROUPS
ROUTER_GROUP_LANE = 0


def _first_argmax(vals, vmax, lane_f):
    return jnp.min(jnp.where(vals == vmax, lane_f, float(LANES)), axis=1, keepdims=True)


def _routing_lanes(x, w_router):
    logits = jnp.dot(x, w_router, preferred_element_type=F32, precision=lax.Precision.HIGHEST)
    lane_f = lax.broadcasted_iota(jnp.int32, logits.shape, 1).astype(F32)
    gl = jnp.where(lane_f < MOE_GROUPS, logits, NEG_INF)
    gmax = jnp.max(gl, axis=1, keepdims=True)
    gw = 1.0 / jnp.sum(jnp.exp(gl - gmax), axis=1, keepdims=True)
    gsel = _first_argmax(gl, gmax, lane_f)
    e0 = ROUTER_EXPERT_LANE0 + MOE_EXPERTS * gsel
    el = jnp.where((lane_f >= e0) & (lane_f < e0 + MOE_EXPERTS), logits, NEG_INF)
    v1 = jnp.max(el, axis=1, keepdims=True)
    i1 = _first_argmax(el, v1, lane_f)
    el2 = jnp.where(lane_f == i1, NEG_INF, el)
    v2 = jnp.max(el2, axis=1, keepdims=True)
    i2 = _first_argmax(el2, v2, lane_f)
    e2 = jnp.exp(v2 - v1)
    inv = gw / (1.0 + e2)
    comb = jnp.where(lane_f == i1, inv, 0.0) + jnp.where(lane_f == i2, e2 * inv, 0.0)
    return comb + jnp.where(lane_f == ROUTER_GROUP_LANE, gsel, 0.0)


MOE_TM = 512
GATHER_UNROLL = 8


def _issue_row_gather(idx_ref, src_hbm, dst_ref, sem, n_rows):
    def body(r, carry):
        pltpu.make_async_copy(src_hbm.at[pl.ds(idx_ref[0, r], 1), :], dst_ref.at[pl.ds(r, 1), :], sem).start()
        return carry
    lax.fori_loop(0, n_rows, body, 0, unroll=GATHER_UNROLL)


def _wait_row_gather(src_hbm, dst_ref, sem, n_rows):
    pltpu.make_async_copy(src_hbm.at[pl.ds(0, n_rows), :], dst_ref, sem).wait()


def _sort_by_group(gid, t, n_pad_rows):
    groups = jnp.arange(MOE_GROUPS, dtype=jnp.int32)
    onehot = (gid[:, None] == groups[None, :]).astype(jnp.int32)
    csum = jnp.cumsum(onehot, axis=0)
    counts = csum[-1]
    rank = jnp.sum(onehot * csum, axis=1) - 1
    tiles_g = (counts + MOE_TM - 1) // MOE_TM
    tile_end = jnp.cumsum(tiles_g)
    row_start = (tile_end - tiles_g) * MOE_TM
    pos = jnp.sum(onehot * row_start[None, :], axis=1) + rank
    src = jnp.zeros((n_pad_rows,), jnp.int32).at[pos].set(jnp.arange(t, dtype=jnp.int32))
    tile_ids = jnp.arange(n_pad_rows // MOE_TM, dtype=jnp.int32)
    tile_group = jnp.minimum(jnp.sum((tile_ids[:, None] >= tile_end[None, :]).astype(jnp.int32), axis=1),
                             MOE_GROUPS - 1)
    return pos, src, tile_group, tile_end[-1:]


def _moe_ffn_kernel(tg_ref, nv_ref, idx_cur_ref, idx_nxt_ref, xc_hbm, wg_ref, wu_ref, wd_ref, f_ref,
                    xbuf_ref, sem_ref):
    i = pl.program_id(0)
    slot = i % 2
    n_valid = nv_ref[0]
    d = f_ref.shape[1]

    @pl.when(i == 0)
    def _():
        _issue_row_gather(idx_cur_ref, xc_hbm, xbuf_ref.at[0], sem_ref.at[0], MOE_TM)

    @pl.when(i + 1 < n_valid)
    def _():
        _issue_row_gather(idx_nxt_ref, xc_hbm, xbuf_ref.at[1 - slot], sem_ref.at[1 - slot], MOE_TM)

    @pl.when(i < n_valid)
    def _():
        _wait_row_gather(xc_hbm, xbuf_ref.at[slot], sem_ref.at[slot], MOE_TM)
        xb = xbuf_ref[slot, :, 0:d].astype(BF16)
        comb = xbuf_ref[slot, :, d:d + LANES]
        lane = lax.broadcasted_iota(jnp.int32, comb.shape, 1)
        lane0 = ROUTER_EXPERT_LANE0 + MOE_EXPERTS * tg_ref[i]
        acc = None
        for e in range(MOE_EXPERTS):
            ce = jnp.sum(jnp.where(lane == lane0 + e, comb, 0.0), axis=1, keepdims=True)
            gate = jnp.dot(xb, wg_ref[0, e], preferred_element_type=F32)
            up = jnp.dot(xb, wu_ref[0, e], preferred_element_type=F32)
            hid = (gate * jax.nn.sigmoid(gate)) * up * ce
            part = jnp.dot(hid.astype(BF16), wd_ref[0, e], preferred_element_type=F32)
            acc = part if acc is None else acc + part
        f_ref[...] = acc

    @pl.when(i >= n_valid)
    def _():
        f_ref[...] = jnp.zeros(f_ref.shape, F32)


def _moe_ffn(xc, src, tile_group, n_valid, w_gate, w_up, w_down):
    d = xc.shape[1] - LANES
    n_tiles = tile_group.shape[0]
    _, ne, _, f = w_gate.shape
    idx = src.reshape(n_tiles, 1, MOE_TM)
    grid_spec = pltpu.PrefetchScalarGridSpec(
        num_scalar_prefetch=2,
        grid=(n_tiles,),
        in_specs=[
            pl.BlockSpec((None, 1, MOE_TM), lambda i, tg, nv: (i, 0, 0), memory_space=pltpu.SMEM),
            pl.BlockSpec((None, 1, MOE_TM), lambda i, tg, nv: (jnp.minimum(i + 1, n_tiles - 1), 0, 0),
                         memory_space=pltpu.SMEM),
            pl.BlockSpec(memory_space=pl.ANY),
            pl.BlockSpec((1, ne, d, f), lambda i, tg, nv: (tg[i], 0, 0, 0)),
            pl.BlockSpec((1, ne, d, f), lambda i, tg, nv: (tg[i], 0, 0, 0)),
            pl.BlockSpec((1, ne, f, d), lambda i, tg, nv: (tg[i], 0, 0, 0)),
        ],
        out_specs=pl.BlockSpec((MOE_TM, d), lambda i, tg, nv: (i, 0)),
        scratch_shapes=[pltpu.VMEM((2, MOE_TM, d + LANES), F32), pltpu.SemaphoreType.DMA((2,))],
    )
    return pl.pallas_call(
        _moe_ffn_kernel,
        grid_spec=grid_spec,
        out_shape=jax.ShapeDtypeStruct((n_tiles * MOE_TM, d), F32),
        compiler_params=_params("arbitrary"),
        name="moe_ffn",
    )(tile_group, n_valid, idx, idx, xc, w_gate, w_up, w_down)


LN2_TM = 512


def _ln2_kernel(idx_cur_ref, idx_nxt_ref, f_hbm, x_ref, g_ref, b_ref, y_ref, yb_ref, fbuf_ref, sem_ref):
    i = pl.program_id(0)
    slot = i % 2

    @pl.when(i == 0)
    def _():
        _issue_row_gather(idx_cur_ref, f_hbm, fbuf_ref.at[0], sem_ref.at[0], LN2_TM)

    @pl.when(i + 1 < pl.num_programs(0))
    def _():
        _issue_row_gather(idx_nxt_ref, f_hbm, fbuf_ref.at[1 - slot], sem_ref.at[1 - slot], LN2_TM)

    _wait_row_gather(f_hbm, fbuf_ref.at[slot], sem_ref.at[slot], LN2_TM)
    y = _layer_norm_rows(DEEPNORM_ALPHA * x_ref[...] + fbuf_ref[slot], g_ref[...], b_ref[...])
    y_ref[...] = y
    yb_ref[...] = y.astype(BF16)


def _ln2_gather(f_sorted, pos, x, g, b):
    t = x.shape[0]
    d = f_sorted.shape[1]
    n_tiles = t // LN2_TM
    idx = pos.reshape(n_tiles, 1, LN2_TM)
    row = pl.BlockSpec((LN2_TM, d), lambda i: (i, 0))
    vec = pl.BlockSpec((1, d), lambda i: (0, 0))
    return pl.pallas_call(
        _ln2_kernel,
        grid=(n_tiles,),
        in_specs=[
            pl.BlockSpec((None, 1, LN2_TM), lambda i: (i, 0, 0), memory_space=pltpu.SMEM),
            pl.BlockSpec((None, 1, LN2_TM), lambda i: (jnp.minimum(i + 1, n_tiles - 1), 0, 0),
                         memory_space=pltpu.SMEM),
            pl.BlockSpec(memory_space=pl.ANY),
            row, vec, vec,
        ],
        out_specs=[row, row],
        out_shape=[jax.ShapeDtypeStruct((t, d), F32), jax.ShapeDtypeStruct((t, d), BF16)],
        scratch_shapes=[pltpu.VMEM((2, LN2_TM, d), F32), pltpu.SemaphoreType.DMA((2,))],
        compiler_params=_params("arbitrary"),
        name="ln2_gather",
    )(idx, idx, f_sorted, x, g, b)


def _da_weight_layout(w_in):
    d = w_in.shape[0]
    w = DA_QK_WIDTH
    q1, q2, k1, k2 = (w_in[:, i * w:(i + 1) * w].reshape(d, DA_HEADS, DA_HEAD_DIM) for i in range(4))
    qk = jnp.concatenate([jnp.concatenate([q1, q2], axis=2).reshape(d, 2 * w),
                          jnp.concatenate([k1, k2], axis=2).reshape(d, 2 * w)], axis=1)
    return qk, w_in[:, 4 * w:]


def _trunk(xs, da_w_in, da_w_out, da_lambda_q1, da_lambda_k1, da_lambda_q2, da_lambda_k2, da_subln_g,
           dl_w_in, dl_w_out, ln1_g, ln1_b, ln2_g, ln2_b,
           moe_router_group, moe_router_expert, moe_w_gate, moe_w_up, moe_w_down):
    d = xs[0].shape[-1]
    segs = []
    off = 0
    for x in xs:
        segs.append((off, x.shape[0], x.shape[1]))
        off += x.shape[0] * x.shape[1]
    t_total = off
    x = jnp.concatenate([v.reshape(-1, d) for v in xs], axis=0)
    xb = x.astype(BF16)

    max_len = max(s for _, _, s in segs)
    rope = _rope_tables(max_len)
    pos_blocks = np.concatenate([
        np.tile(np.arange(s // PROJ_TM, dtype=np.int32), b) for _, b, s in segs])
    pos_blocks = jnp.asarray(pos_blocks)

    for i in range(DEPTH):
        j = i // N_MIXERS
        if i % N_MIXERS == 0:
            lambda_init = 0.8 - 0.6 * math.exp(-0.3 * i)
            w_qk, w_v = _da_weight_layout(da_w_in[j])
            qk = _proj(xb, w_qk.astype(BF16), rope, pos_blocks,
                       q_cols=2 * DA_QK_WIDTH, rope_cols=4 * DA_QK_WIDTH, q_scale=QK_SCALE_LOG2)
            vt = _proj_t(xb, w_v.T.astype(BF16))
            lam_params = jnp.stack([da_lambda_q1[j], da_lambda_k1[j], da_lambda_q2[j], da_lambda_k2[j]]).astype(F32)
            g_col = da_subln_g[j].astype(F32).reshape(DA_V_DIM, 1)
            o = jnp.zeros((t_total, d), BF16) if len(segs) > 1 else None
            for ro, b, s in segs:
                o = _da_attention(qk, vt, lam_params, g_col, lambda_init, ro, b, s, o)
            os_, lses = [o.reshape(1, t_total, d)], []
            w_out = da_w_out[j]
        else:
            os_, lses = [], []
            gw = 3 * DIL_GROUP_WIDTH
            for g, (window, dilation) in enumerate(DIL_PATTERNS):
                assert window // (2 * dilation) == DIL_HALF
                proj = _proj(xb, dl_w_in[j][:, g * gw:(g + 1) * gw].astype(BF16), rope, pos_blocks,
                             q_cols=DIL_GROUP_WIDTH, rope_cols=2 * DIL_GROUP_WIDTH, q_scale=QK_SCALE_LOG2,
                             dilation=dilation)
                pair = None
                if len(segs) > 1:
                    shape = (dilation, t_total // dilation, DIL_GROUP_WIDTH)
                    pair = (jnp.zeros(shape, BF16), jnp.zeros(shape, F32))
                for ro, b, s in segs:
                    pair = _dil_attention_group(proj, g, dilation, ro, b, s, pair)
                os_.append(pair[0])
                lses.append(pair[1])
            w_out = dl_w_out[j]
        w_router = jnp.concatenate([
            moe_router_group[i].astype(F32),
            moe_router_expert[i].reshape(d, N_EXPERTS).astype(F32),
            jnp.zeros((d, LANES - MOE_GROUPS - N_EXPERTS), F32)], axis=1)
        xc = _post_attention(os_, lses, w_out.astype(BF16), x,
                             ln1_g[i].reshape(1, d).astype(F32), ln1_b[i].reshape(1, d).astype(F32), w_router)
        gid = xc[:, d + ROUTER_GROUP_LANE].astype(jnp.int32)
        pos, src, tile_group, n_valid = _sort_by_group(gid, t_total, t_total + MOE_GROUPS * MOE_TM)
        f_sorted = _moe_ffn(xc, src, tile_group, n_valid,
                            moe_w_gate[i].astype(BF16), moe_w_up[i].astype(BF16), moe_w_down[i].astype(BF16))
        x, xb = _ln2_gather(f_sorted, pos, xc,
                            ln2_g[i].reshape(1, d).astype(F32), ln2_b[i].reshape(1, d).astype(F32))

    outs = []
    for (ro, b, s), v in zip(segs, xs):
        outs.append(x[ro:ro + b * s].reshape(v.shape))
    return outs


def kernel(x_prompt, x_sample, da_w_in, da_w_out, da_lambda_q1, da_lambda_k1, da_lambda_q2, da_lambda_k2, da_subln_g, dl_w_in, dl_w_out, ln1_g, ln1_b, ln2_g, ln2_b, moe_router_group, moe_router_expert, moe_w_gate, moe_w_up, moe_w_down):
    y_prompt, y_sample = _trunk(
        [x_prompt, x_sample], da_w_in, da_w_out, da_lambda_q1, da_lambda_k1, da_lambda_q2, da_lambda_k2,
        da_subln_g, dl_w_in, dl_w_out, ln1_g, ln1_b, ln2_g, ln2_b,
        moe_router_group, moe_router_expert, moe_w_gate, moe_w_up, moe_w_down)
    return (y_prompt, y_sample)
```

```python
import functools
import math

import jax
import jax.numpy as jnp
import numpy as np
from jax import lax
from jax.experimental import pallas as pl
from jax.experimental.pallas import tpu as pltpu

F32 = jnp.float32
BF16 = jnp.bfloat16

LANES = 128
VMEM_LIMIT_BYTES = 56 * 1024 * 1024

D_MODEL = 1024
DEPTH = 4
N_MIXERS = 2

DA_HEADS = 8
DA_HEAD_DIM = 64
DA_V_DIM = 2 * DA_HEAD_DIM
DA_QK_WIDTH = DA_HEADS * DA_HEAD_DIM

DIL_PATTERNS = ((128, 1), (512, 4), (2048, 16))
DIL_GROUPS = len(DIL_PATTERNS)
DIL_HEADS = 16
DIL_HEAD_DIM = 64
DIL_GROUP_WIDTH = DIL_HEADS * DIL_HEAD_DIM
DIL_HALF = 64

ROPE_THETA = 500000.0
ROPE_ROT = DA_HEAD_DIM // 4
ROPE_HALF = ROPE_ROT // 2

LN_EPS = 1e-5
DEEPNORM_ALPHA = (2.0 * DEPTH) ** 0.25

MOE_GROUPS = 4
MOE_EXPERTS = 4
MOE_FF = 512
N_EXPERTS = MOE_GROUPS * MOE_EXPERTS

NEG_INF = -1e30
LOG2_E = math.log2(math.e)
QK_SCALE_LOG2 = DA_HEAD_DIM ** -0.5 * LOG2_E


def _params(*sem):
    return pltpu.CompilerParams(dimension_semantics=sem, vmem_limit_bytes=VMEM_LIMIT_BYTES)


PROJ_TM = 1024
PROJ_TN = 1024


def _rope_tables(max_len):
    inv = ROPE_THETA ** (-jnp.arange(ROPE_HALF, dtype=F32) * (2.0 / ROPE_ROT))
    ang = jnp.arange(max_len, dtype=F32)[:, None] * inv[None, :]
    cos, sin = jnp.cos(ang), jnp.sin(ang)
    ones = jnp.ones((max_len, DA_HEAD_DIM - ROPE_ROT), F32)
    zeros_h = jnp.zeros((max_len, ROPE_HALF), F32)
    zeros_r = jnp.zeros((max_len, DA_HEAD_DIM - ROPE_ROT), F32)
    c = jnp.concatenate([cos, cos, ones], axis=1)
    s_up = jnp.concatenate([zeros_h, sin, zeros_r], axis=1)
    s_dn = jnp.concatenate([-sin, zeros_h, zeros_r], axis=1)
    reps = LANES // DA_HEAD_DIM
    return jnp.stack([jnp.tile(c, (1, reps)), jnp.tile(s_up, (1, reps)), jnp.tile(s_dn, (1, reps))])


PROJ_CHUNK = 256


def _proj_kernel(pos_ref, x_ref, w_ref, rope_ref, o_ref, ybuf_ref, *, n_cols, q_cols, rope_cols, q_scale, dilation):
    del pos_ref
    x = x_ref[...]
    c = rope_ref[0]
    s_up = rope_ref[1]
    s_dn = rope_ref[2]
    rows = x.shape[0] // dilation
    for k in range(n_cols // PROJ_CHUNK):
        c0 = k * PROJ_CHUNK
        y = jnp.dot(x, w_ref[:, c0:c0 + PROJ_CHUNK], preferred_element_type=F32)
        if c0 < q_cols:
            y = y * q_scale
        parts = []
        for h in range(PROJ_CHUNK // LANES):
            yc = y[:, h * LANES:(h + 1) * LANES]
            if c0 < rope_cols:
                yc = yc * c + pltpu.roll(yc, ROPE_HALF, 1) * s_up + pltpu.roll(yc, LANES - ROPE_HALF, 1) * s_dn
            parts.append(yc)
        if dilation == 1:
            for h, yc in enumerate(parts):
                o_ref[0, :, c0 + h * LANES:c0 + (h + 1) * LANES] = yc.astype(o_ref.dtype)
        else:
            for h, yc in enumerate(parts):
                slot = (2 * k + h) % ybuf_ref.shape[0]
                ybuf_ref[slot] = yc
                for r in range(dilation):
                    o_ref[r, :, c0 + h * LANES:c0 + (h + 1) * LANES] = (
                        ybuf_ref[slot, pl.ds(r, rows, stride=dilation), :].astype(o_ref.dtype))


def _proj(xb, w, rope, pos_blocks, *, q_cols, rope_cols, q_scale, dilation=1):
    t, d = xb.shape
    n = w.shape[1]
    rows = PROJ_TM // dilation
    grid_spec = pltpu.PrefetchScalarGridSpec(
        num_scalar_prefetch=1,
        grid=(t // PROJ_TM,),
        in_specs=[
            pl.BlockSpec((PROJ_TM, d), lambda i, pos: (i, 0)),
            pl.BlockSpec((d, n), lambda i, pos: (0, 0)),
            pl.BlockSpec((3, PROJ_TM, LANES), lambda i, pos: (0, pos[i], 0)),
        ],
        out_specs=pl.BlockSpec((dilation, rows, n), lambda i, pos: (0, i, 0)),
        scratch_shapes=[pltpu.VMEM((4, PROJ_TM, LANES), F32)],
    )
    return pl.pallas_call(
        functools.partial(_proj_kernel, n_cols=n, q_cols=q_cols, rope_cols=rope_cols, q_scale=q_scale,
                          dilation=dilation),
        grid_spec=grid_spec,
        out_shape=jax.ShapeDtypeStruct((dilation, t // dilation, n), BF16),
        compiler_params=_params("parallel"),
        name="proj_rope",
    )(pos_blocks, xb, w, rope)


DA_TK = 512
DA_ONES_ROWS = 16
DA_VT_ROWS = DA_V_DIM + DA_ONES_ROWS


def _projt_kernel(x_ref, wt_ref, o_ref):
    y = lax.dot_general(wt_ref[...], x_ref[...], (((1,), (1,)), ((), ())), preferred_element_type=F32)
    h = o_ref.shape[0]
    o_ref[:, 0, 0:DA_V_DIM, :] = y.reshape(h, DA_V_DIM, DA_TK).astype(o_ref.dtype)
    o_ref[:, 0, DA_V_DIM:DA_VT_ROWS, :] = jnp.ones((h, DA_ONES_ROWS, DA_TK), o_ref.dtype)


def _proj_t(xb, wt):
    t, d = xb.shape
    n = wt.shape[0]
    h = n // DA_V_DIM
    return pl.pallas_call(
        _projt_kernel,
        grid=(t // DA_TK,),
        in_specs=[
            pl.BlockSpec((DA_TK, d), lambda i: (i, 0)),
            pl.BlockSpec((n, d), lambda i: (0, 0)),
        ],
        out_specs=pl.BlockSpec((h, 1, DA_VT_ROWS, DA_TK), lambda i: (0, i, 0, 0)),
        out_shape=jax.ShapeDtypeStruct((h, t // DA_TK, DA_VT_ROWS, DA_TK), BF16),
        compiler_params=_params("parallel"),
        name="proj_vt",
    )(xb, wt)


DA_TQ = 512


def _da_attn_kernel(lam_ref, g_ref, q_ref, k_ref, vt_ref, *rest, n_kblk, lambda_init, has_prev):
    if has_prev:
        rest = rest[1:]
    (o_ref, q1_ref, q2_ref, sa1_ref, sa2_ref, sb1_ref, sb2_ref, xa1_ref, xa2_ref, xb1_ref, xb2_ref,
     m1_ref, a1_ref, m2_ref, a2_ref) = rest
    buf_a = ((sa1_ref, xa1_ref), (sa2_ref, xa2_ref))
    buf_b = ((sb1_ref, xb1_ref), (sb2_ref, xb2_ref))
    qt = q_ref[...].astype(F32).T
    row = lax.broadcasted_iota(jnp.int32, qt.shape, 0)
    q1_ref[...] = jnp.where(row < DA_HEAD_DIM, qt, 0.0).astype(q1_ref.dtype)
    q2_ref[...] = jnp.where(row >= DA_HEAD_DIM, qt, 0.0).astype(q2_ref.dtype)
    for m_ref, a_ref in ((m1_ref, a1_ref), (m2_ref, a2_ref)):
        m_ref[...] = jnp.full(m_ref.shape, NEG_INF, F32)
        a_ref[...] = jnp.zeros(a_ref.shape, F32)

    def scores(j, buf):
        k = k_ref[pl.ds(pl.multiple_of(j * DA_TK, DA_TK), DA_TK), :]
        for qm_ref, (s_ref, x_ref) in zip((q1_ref, q2_ref), buf):
            s = jnp.dot(k, qm_ref[...], preferred_element_type=F32)
            s_ref[...] = s
            x_ref[...] = jnp.max(s.reshape(DA_TK // 8, 8, DA_TQ), axis=0)

    def softmax_pv(j, buf):
        vt = vt_ref[0, j]
        for (s_ref, x_ref), m_ref, a_ref in zip(buf, (m1_ref, m2_ref), (a1_ref, a2_ref)):
            m_old = m_ref[...]
            m_new = jnp.maximum(m_old, jnp.max(x_ref[...], axis=0, keepdims=True))
            p = jnp.exp2(s_ref[...] - m_new).astype(vt.dtype)
            a_ref[...] = jnp.exp2(m_old - m_new) * a_ref[...] + jnp.dot(vt, p, preferred_element_type=F32)
            m_ref[...] = m_new

    scores(0, buf_a)

    def pair(jj, carry):
        j = 2 * jj
        scores(j + 1, buf_b)
        softmax_pv(j, buf_a)
        scores(j + 2, buf_a)
        softmax_pv(j + 1, buf_b)
        return carry

    lax.fori_loop(0, n_kblk // 2 - 1, pair, 0)
    scores(n_kblk - 1, buf_b)
    softmax_pv(n_kblk - 2, buf_a)
    softmax_pv(n_kblk - 1, buf_b)

    lp = lam_ref[...]
    lam = (jnp.exp(jnp.sum(lp[0:1] * lp[1:2], axis=1, keepdims=True))
           - jnp.exp(jnp.sum(lp[2:3] * lp[3:4], axis=1, keepdims=True)) + lambda_init)
    a1 = a1_ref[...]
    a2 = a2_ref[...]
    o = (a1[0:DA_V_DIM] * (1.0 / a1[DA_V_DIM:DA_V_DIM + 1])
         - lam * (a2[0:DA_V_DIM] * (1.0 / a2[DA_V_DIM:DA_V_DIM + 1])))
    o = o * lax.rsqrt(jnp.mean(jnp.square(o), axis=0, keepdims=True) + LN_EPS)
    o = o * g_ref[...] * (1.0 - lambda_init)
    o_ref[...] = o.T.astype(o_ref.dtype)


def _da_attention(qk, vt, lam_params, subln_g, lambda_init, row_off, batch, seq, prev_out):
    h = DA_HEADS
    t_total = qk.shape[1]
    nq = seq // DA_TQ
    nk = seq // DA_TK
    assert nk % 2 == 0 and nk >= 2
    qoff = row_off // DA_TQ
    soff = row_off // seq
    s_buf = pltpu.VMEM((DA_TK, DA_TQ), F32)
    x_buf = pltpu.VMEM((8, DA_TQ), F32)
    in_specs = [
        pl.BlockSpec((4, DA_HEAD_DIM), lambda b, hh, i: (0, 0)),
        pl.BlockSpec((DA_V_DIM, 1), lambda b, hh, i: (0, 0)),
        pl.BlockSpec((None, DA_TQ, LANES), lambda b, hh, i: (0, qoff + b * nq + i, hh)),
        pl.BlockSpec((None, seq, LANES), lambda b, hh, i: (0, soff + b, h + hh)),
        pl.BlockSpec((1, nk, DA_VT_ROWS, DA_TK), lambda b, hh, i: (hh, soff + b, 0, 0)),
    ]
    operands = [lam_params, subln_g, qk, qk, vt]
    aliases = {}
    if prev_out is not None:
        in_specs.append(pl.BlockSpec(memory_space=pl.ANY))
        operands.append(prev_out)
        aliases = {5: 0}
    return pl.pallas_call(
        functools.partial(_da_attn_kernel, n_kblk=nk, lambda_init=lambda_init, has_prev=prev_out is not None),
        grid=(batch, h, nq),
        in_specs=in_specs,
        out_specs=pl.BlockSpec((DA_TQ, DA_V_DIM), lambda b, hh, i: (qoff + b * nq + i, hh)),
        out_shape=jax.ShapeDtypeStruct((t_total, h * DA_V_DIM), BF16),
        input_output_aliases=aliases,
        scratch_shapes=[
            pltpu.VMEM((LANES, DA_TQ), BF16), pltpu.VMEM((LANES, DA_TQ), BF16),
            s_buf, s_buf, s_buf, s_buf,
            x_buf, x_buf, x_buf, x_buf,
            pltpu.VMEM((1, DA_TQ), F32), pltpu.VMEM((DA_VT_ROWS, DA_TQ), F32),
            pltpu.VMEM((1, DA_TQ), F32), pltpu.VMEM((DA_VT_ROWS, DA_TQ), F32),
        ],
        compiler_params=_params("parallel", "parallel", "arbitrary"),
        name="da_attn",
    )(*operands)


DIL_QSUB = 128
DIL_KWIN = DIL_QSUB + 2 * DIL_HALF
DIL_QBLK = 1024


def _dil_attn_kernel(q_ref, k_ref, v_ref, *rest, sub_len, qblk, has_prev):
    o_ref, lse_ref = rest[2:] if has_prev else rest
    qi = pl.program_id(3)
    lane = lax.broadcasted_iota(jnp.int32, (DIL_QSUB, LANES), 1)
    lo = lane < DIL_HEAD_DIM
    row = lax.broadcasted_iota(jnp.int32, (DIL_QSUB, DIL_KWIN), 0)
    col = lax.broadcasted_iota(jnp.int32, (DIL_QSUB, DIL_KWIN), 1)

    interior = jnp.abs(col - DIL_HALF - row) <= DIL_HALF
    n_sub = qblk // DIL_QSUB

    for n in range(n_sub):
        q0 = n * DIL_QSUB
        gq0 = qi * qblk + q0
        ws = jnp.clip(gq0 - DIL_HALF, 0, sub_len - DIL_KWIN)
        ws = pl.multiple_of(ws, DIL_HALF)
        qb = q_ref[pl.ds(q0, DIL_QSUB), :]
        kw = k_ref[pl.ds(ws, DIL_KWIN), :]
        vw = v_ref[pl.ds(ws, DIL_KWIN), :]
        if 0 < n < n_sub - 1:
            valid = interior
        else:
            valid = jnp.abs((ws + col) - (gq0 + row)) <= DIL_HALF
        zero = jnp.zeros_like(qb)
        outs, lses = [], []
        for qm in (jnp.where(lo, qb, zero), jnp.where(lo, zero, qb)):
            s = lax.dot_general(qm, kw, (((1,), (1,)), ((), ())), preferred_element_type=F32)
            s = jnp.where(valid, s, NEG_INF)
            m = jnp.max(s, axis=1, keepdims=True)
            p = jnp.exp2(s - m)
            l = jnp.sum(p, axis=1, keepdims=True)
            o = jnp.dot(p.astype(vw.dtype), vw, preferred_element_type=F32) * (1.0 / l)
            outs.append(o)
            lses.append(jnp.broadcast_to(m + jnp.log(l) * LOG2_E, (DIL_QSUB, LANES)))
        o_ref[pl.ds(q0, DIL_QSUB), :] = jnp.where(lo, outs[0], outs[1]).astype(o_ref.dtype)
        lse_ref[pl.ds(q0, DIL_QSUB), :] = jnp.where(lo, lses[0], lses[1])


def _dil_attention_group(proj, g, dilation, row_off, batch, seq, prev):
    rows_total = proj.shape[1]
    sub_len = seq // dilation
    qblk = min(DIL_QBLK, sub_len)
    nqb = sub_len // qblk
    hp = DIL_GROUP_WIDTH // LANES
    soff = row_off // seq

    def qmap(b, r, p, i):
        return (r, (soff + b) * nqb + i, p)

    def kmap(b, r, p, i):
        return (r, soff + b, hp + p)

    def vmap(b, r, p, i):
        return (r, soff + b, 2 * hp + p)

    in_specs = [
        pl.BlockSpec((None, qblk, LANES), qmap),
        pl.BlockSpec((None, sub_len, LANES), kmap),
        pl.BlockSpec((None, sub_len, LANES), vmap),
    ]
    operands = [proj, proj, proj]
    aliases = {}
    if prev is not None:
        in_specs += [pl.BlockSpec(memory_space=pl.ANY)] * 2
        operands += list(prev)
        aliases = {3: 0, 4: 1}
    return pl.pallas_call(
        functools.partial(_dil_attn_kernel, sub_len=sub_len, qblk=qblk, has_prev=prev is not None),
        grid=(batch, dilation, hp, nqb),
        in_specs=in_specs,
        out_specs=[
            pl.BlockSpec((None, qblk, LANES), qmap),
            pl.BlockSpec((None, qblk, LANES), qmap),
        ],
        out_shape=[
            jax.ShapeDtypeStruct((dilation, rows_total, DIL_GROUP_WIDTH), BF16),
            jax.ShapeDtypeStruct((dilation, rows_total, DIL_GROUP_WIDTH), F32),
        ],
        input_output_aliases=aliases,
        compiler_params=_params("parallel", "parallel", "parallel", "arbitrary"),
        name=f"dil_attn_g{g}",
    )(*operands)


POST_TM = 512


def _layer_norm_rows(z, g, b):
    mu = jnp.mean(z, axis=-1, keepdims=True)
    zc = z - mu
    var = jnp.mean(jnp.square(zc), axis=-1, keepdims=True)
    return zc * lax.rsqrt(var + LN_EPS) * g + b


def _token_order(ref, scratch_ref):
    dil, rows, width = ref.shape
    if dil == 1:
        return ref[0].astype(F32)
    n_chunks = width // LANES
    for r in range(dil):
        v = ref[r].astype(F32)
        for c in range(n_chunks):
            scratch_ref[c, pl.ds(r, rows, stride=dil), :] = v[:, c * LANES:(c + 1) * LANES]
    return jnp.concatenate([scratch_ref[c] for c in range(n_chunks)], axis=1)


def _post_kernel(*refs, n_groups):
    o_refs = refs[:n_groups]
    lse_refs = refs[n_groups:2 * n_groups] if n_groups > 1 else ()
    n_in = n_groups + len(lse_refs)
    w_ref, x_ref, g_ref, b_ref, wr_ref, xc_ref = refs[n_in:n_in + 6]
    scratch = refs[n_in + 6:]
    d = x_ref.shape[1]
    if n_groups == 1:
        o = o_refs[0][0]
    else:
        lses = [_token_order(r, s) for r, s in zip(lse_refs, scratch[:n_groups])]
        m = functools.reduce(jnp.maximum, lses)
        es = [jnp.exp2(l - m) for l in lses]
        inv = 1.0 / functools.reduce(jnp.add, es)
        acc = None
        for e, o_ref in zip(es, o_refs):
            term = (e * inv) * _token_order(o_ref, scratch[n_groups])
            acc = term if acc is None else acc + term
        o = acc.astype(BF16)
    h = jnp.dot(o, w_ref[...], preferred_element_type=F32)
    y = _layer_norm_rows(DEEPNORM_ALPHA * x_ref[...] + h, g_ref[...], b_ref[...])
    xc_ref[:, 0:d] = y
    xc_ref[:, d:d + LANES] = _routing_lanes(y, wr_ref[...])


def _post_attention(os_, lses, w_out, x, g, b, w_router):
    t, d = x.shape
    n_groups = len(os_)
    row = pl.BlockSpec((POST_TM, d), lambda i: (i, 0))
    full = pl.BlockSpec((d, d), lambda i: (0, 0))
    vec = pl.BlockSpec((1, d), lambda i: (0, 0))
    grouped = [pl.BlockSpec((a.shape[0], POST_TM // a.shape[0], d), lambda i: (0, i, 0)) for a in (*os_, *lses)]
    n_scratch = n_groups + 1 if n_groups > 1 else 0
    return pl.pallas_call(
        functools.partial(_post_kernel, n_groups=n_groups),
        grid=(t // POST_TM,),
        in_specs=grouped + [full, row, vec, vec, pl.BlockSpec((d, LANES), lambda i: (0, 0))],
        out_specs=pl.BlockSpec((POST_TM, d + LANES), lambda i: (i, 0)),
        out_shape=jax.ShapeDtypeStruct((t, d + LANES), F32),
        scratch_shapes=[pltpu.VMEM((d // LANES, POST_TM, LANES), F32)] * n_scratch,
        compiler_params=_params("parallel"),
        name="post_attn_ln",
    )(*os_, *lses, w_out, x, g, b, w_router)


ROUTER_TM = 512
ROUTER_EXPERT_LANE0 = MOE_GROUPS
ROUTER_GROUP_LANE = 0


def _first_argmax(vals, vmax, lane_f):
    return jnp.min(jnp.where(vals == vmax, lane_f, float(LANES)), axis=1, keepdims=True)


def _routing_lanes(x, w_router):
    xh = x.astype(BF16)
    xl = (x - xh.astype(F32)).astype(BF16)
    wh = w_router.astype(BF16)
    wl = (w_router - wh.astype(F32)).astype(BF16)
    logits = (jnp.dot(xh, wh, preferred_element_type=F32) + jnp.dot(xh, wl, preferred_element_type=F32)
              + jnp.dot(xl, wh, preferred_element_type=F32))
    lane_f = lax.broadcasted_iota(jnp.int32, logits.shape, 1).astype(F32)
    gl = jnp.where(lane_f < MOE_GROUPS, logits, NEG_INF)
    gmax = jnp.max(gl, axis=1, keepdims=True)
    gw = 1.0 / jnp.sum(jnp.exp(gl - gmax), axis=1, keepdims=True)
    gsel = _first_argmax(gl, gmax, lane_f)
    e0 = ROUTER_EXPERT_LANE0 + MOE_EXPERTS * gsel
    el = jnp.where((lane_f >= e0) & (lane_f < e0 + MOE_EXPERTS), logits, NEG_INF)
    v1 = jnp.max(el, axis=1, keepdims=True)
    i1 = _first_argmax(el, v1, lane_f)
    el2 = jnp.where(lane_f == i1, NEG_INF, el)
    v2 = jnp.max(el2, axis=1, keepdims=True)
    i2 = _first_argmax(el2, v2, lane_f)
    e2 = jnp.exp(v2 - v1)
    inv = gw / (1.0 + e2)
    comb = jnp.where(lane_f == i1, inv, 0.0) + jnp.where(lane_f == i2, e2 * inv, 0.0)
    return comb + jnp.where(lane_f == ROUTER_GROUP_LANE, gsel, 0.0)


MOE_TM = 512
GATHER_UNROLL = 8


def _issue_row_gather(idx_ref, src_hbm, dst_ref, sem, n_rows):
    def body(r, carry):
        pltpu.make_async_copy(src_hbm.at[pl.ds(idx_ref[0, r], 1), :], dst_ref.at[pl.ds(r, 1), :], sem).start()
        return carry
    lax.fori_loop(0, n_rows, body, 0, unroll=GATHER_UNROLL)


def _wait_row_gather(src_hbm, dst_ref, sem, n_rows):
    pltpu.make_async_copy(src_hbm.at[pl.ds(0, n_rows), :], dst_ref, sem).wait()


def _sort_by_group(gid, t, n_pad_rows):
    groups = jnp.arange(MOE_GROUPS, dtype=jnp.int32)
    onehot = (gid[:, None] == groups[None, :]).astype(jnp.int32)
    csum = jnp.cumsum(onehot, axis=0)
    counts = csum[-1]
    rank = jnp.sum(onehot * csum, axis=1) - 1
    tiles_g = (counts + MOE_TM - 1) // MOE_TM
    tile_end = jnp.cumsum(tiles_g)
    row_start = (tile_end - tiles_g) * MOE_TM
    pos = jnp.sum(onehot * row_start[None, :], axis=1) + rank
    src = jnp.zeros((n_pad_rows,), jnp.int32).at[pos].set(jnp.arange(t, dtype=jnp.int32))
    tile_ids = jnp.arange(n_pad_rows // MOE_TM, dtype=jnp.int32)
    tile_group = jnp.minimum(jnp.sum((tile_ids[:, None] >= tile_end[None, :]).astype(jnp.int32), axis=1),
                             MOE_GROUPS - 1)
    return pos, src, tile_group


def _moe_ffn_kernel(tg_ref, idx_cur_ref, idx_nxt_ref, xc_hbm, wg_ref, wu_ref, wd_ref, f_ref,
                    xbuf_ref, sem_ref, *, n_tiles):
    i = pl.program_id(0)
    d = f_ref.shape[1]
    rows_per_expert = MOE_TM // MOE_EXPERTS

    @pl.when(i == 0)
    def _():
        _issue_row_gather(idx_cur_ref, xc_hbm, xbuf_ref.at[0], sem_ref.at[0], MOE_TM)

    def tile(cur, nxt):
        _wait_row_gather(xc_hbm, xbuf_ref.at[cur], sem_ref.at[cur], MOE_TM)
        xb = xbuf_ref[cur, :, 0:d].astype(BF16)
        comb = xbuf_ref[cur, :, d:d + LANES]
        lane = lax.broadcasted_iota(jnp.int32, comb.shape, 1)
        lane0 = ROUTER_EXPERT_LANE0 + MOE_EXPERTS * tg_ref[i]
        acc = None
        for e in range(MOE_EXPERTS):
            for r in range(e * rows_per_expert, (e + 1) * rows_per_expert):
                pltpu.make_async_copy(xc_hbm.at[pl.ds(idx_nxt_ref[0, r], 1), :],
                                      xbuf_ref.at[nxt, pl.ds(r, 1), :], sem_ref.at[nxt]).start()
            ce = jnp.sum(jnp.where(lane == lane0 + e, comb, 0.0), axis=1, keepdims=True)
            gate = jnp.dot(xb, wg_ref[0, e], preferred_element_type=F32)
            up = jnp.dot(xb, wu_ref[0, e], preferred_element_type=F32)
            hid = (gate * jax.nn.sigmoid(gate)) * up * ce
            part = jnp.dot(hid.astype(BF16), wd_ref[0, e], preferred_element_type=F32)
            acc = part if acc is None else acc + part
        f_ref[...] = acc

    @pl.when(i % 2 == 0)
    def _():
        tile(0, 1)

    @pl.when(i % 2 == 1)
    def _():
        tile(1, 0)

    @pl.when(i == n_tiles - 1)
    def _():
        spare = n_tiles % 2
        _wait_row_gather(xc_hbm, xbuf_ref.at[spare], sem_ref.at[spare], MOE_TM)


def _moe_ffn(xc, src, tile_group, w_gate, w_up, w_down):
    d = xc.shape[1] - LANES
    n_tiles = tile_group.shape[0]
    _, ne, _, f = w_gate.shape
    idx = src.reshape(n_tiles, 1, MOE_TM)
    grid_spec = pltpu.PrefetchScalarGridSpec(
        num_scalar_prefetch=1,
        grid=(n_tiles,),
        in_specs=[
            pl.BlockSpec((None, 1, MOE_TM), lambda i, tg: (i, 0, 0), memory_space=pltpu.SMEM),
            pl.BlockSpec((None, 1, MOE_TM), lambda i, tg: (jnp.minimum(i + 1, n_tiles - 1), 0, 0),
                         memory_space=pltpu.SMEM),
            pl.BlockSpec(memory_space=pl.ANY),
            pl.BlockSpec((1, ne, d, f), lambda i, tg: (tg[i], 0, 0, 0)),
            pl.BlockSpec((1, ne, d, f), lambda i, tg: (tg[i], 0, 0, 0)),
            pl.BlockSpec((1, ne, f, d), lambda i, tg: (tg[i], 0, 0, 0)),
        ],
        out_specs=pl.BlockSpec((MOE_TM, d), lambda i, tg: (i, 0)),
        scratch_shapes=[pltpu.VMEM((2, MOE_TM, d + LANES), F32), pltpu.SemaphoreType.DMA((2,))],
    )
    return pl.pallas_call(
        functools.partial(_moe_ffn_kernel, n_tiles=n_tiles),
        grid_spec=grid_spec,
        out_shape=jax.ShapeDtypeStruct((n_tiles * MOE_TM, d), F32),
        compiler_params=_params("arbitrary"),
        name="moe_ffn",
    )(tile_group, idx, idx, xc, w_gate, w_up, w_down)


LN2_TM = 512


LN2_CHUNK = 64


def _ln2_kernel(idx_cur_ref, idx_nxt_ref, f_hbm, x_ref, g_ref, b_ref, y_ref, yb_ref, fbuf_ref, sem_ref, *, n_tiles):
    i = pl.program_id(0)

    @pl.when(i == 0)
    def _():
        _issue_row_gather(idx_cur_ref, f_hbm, fbuf_ref.at[0], sem_ref.at[0], LN2_TM)

    def tile(cur, nxt):
        _wait_row_gather(f_hbm, fbuf_ref.at[cur], sem_ref.at[cur], LN2_TM)
        g = g_ref[...]
        b = b_ref[...]
        for c in range(LN2_TM // LN2_CHUNK):
            rows = pl.ds(c * LN2_CHUNK, LN2_CHUNK)
            for r in range(c * LN2_CHUNK, (c + 1) * LN2_CHUNK):
                pltpu.make_async_copy(f_hbm.at[pl.ds(idx_nxt_ref[0, r], 1), :],
                                      fbuf_ref.at[nxt, pl.ds(r, 1), :], sem_ref.at[nxt]).start()
            y = _layer_norm_rows(DEEPNORM_ALPHA * x_ref[rows, :] + fbuf_ref[cur, rows, :], g, b)
            y_ref[rows, :] = y
            yb_ref[rows, :] = y.astype(BF16)

    @pl.when(i % 2 == 0)
    def _():
        tile(0, 1)

    @pl.when(i % 2 == 1)
    def _():
        tile(1, 0)

    @pl.when(i == n_tiles - 1)
    def _():
        spare = n_tiles % 2
        _wait_row_gather(f_hbm, fbuf_ref.at[spare], sem_ref.at[spare], LN2_TM)


def _ln2_gather(f_sorted, pos, x, g, b):
    t = x.shape[0]
    d = f_sorted.shape[1]
    n_tiles = t // LN2_TM
    idx = pos.reshape(n_tiles, 1, LN2_TM)
    row = pl.BlockSpec((LN2_TM, d), lambda i: (i, 0))
    vec = pl.BlockSpec((1, d), lambda i: (0, 0))
    return pl.pallas_call(
        functools.partial(_ln2_kernel, n_tiles=n_tiles),
        grid=(n_tiles,),
        in_specs=[
            pl.BlockSpec((None, 1, LN2_TM), lambda i: (i, 0, 0), memory_space=pltpu.SMEM),
            pl.BlockSpec((None, 1, LN2_TM), lambda i: (jnp.minimum(i + 1, n_tiles - 1), 0, 0),
                         memory_space=pltpu.SMEM),
            pl.BlockSpec(memory_space=pl.ANY),
            row, vec, vec,
        ],
        out_specs=[row, row],
        out_shape=[jax.ShapeDtypeStruct((t, d), F32), jax.ShapeDtypeStruct((t, d), BF16)],
        scratch_shapes=[pltpu.VMEM((2, LN2_TM, d), F32), pltpu.SemaphoreType.DMA((2,))],
        compiler_params=_params("arbitrary"),
        name="ln2_gather",
    )(idx, idx, f_sorted, x, g, b)


def _da_weight_layout(w_in):
    d = w_in.shape[0]
    w = DA_QK_WIDTH
    q1, q2, k1, k2 = (w_in[:, i * w:(i + 1) * w].reshape(d, DA_HEADS, DA_HEAD_DIM) for i in range(4))
    qk = jnp.concatenate([jnp.concatenate([q1, q2], axis=2).reshape(d, 2 * w),
                          jnp.concatenate([k1, k2], axis=2).reshape(d, 2 * w)], axis=1)
    return qk, w_in[:, 4 * w:]


def _trunk(xs, da_w_in, da_w_out, da_lambda_q1, da_lambda_k1, da_lambda_q2, da_lambda_k2, da_subln_g,
           dl_w_in, dl_w_out, ln1_g, ln1_b, ln2_g, ln2_b,
           moe_router_group, moe_router_expert, moe_w_gate, moe_w_up, moe_w_down):
    d = xs[0].shape[-1]
    segs = []
    off = 0
    for x in xs:
        segs.append((off, x.shape[0], x.shape[1]))
        off += x.shape[0] * x.shape[1]
    t_total = off
    x = jnp.concatenate([v.reshape(-1, d) for v in xs], axis=0)
    xb = x.astype(BF16)

    max_len = max(s for _, _, s in segs)
    rope = _rope_tables(max_len)
    pos_blocks = np.concatenate([
        np.tile(np.arange(s // PROJ_TM, dtype=np.int32), b) for _, b, s in segs])
    pos_blocks = jnp.asarray(pos_blocks)

    for i in range(DEPTH):
        j = i // N_MIXERS
        if i % N_MIXERS == 0:
            lambda_init = 0.8 - 0.6 * math.exp(-0.3 * i)
            w_qk, w_v = _da_weight_layout(da_w_in[j])
            qk = _proj(xb, w_qk.astype(BF16), rope, pos_blocks,
                       q_cols=2 * DA_QK_WIDTH, rope_cols=4 * DA_QK_WIDTH, q_scale=QK_SCALE_LOG2)
            vt = _proj_t(xb, w_v.T.astype(BF16))
            lam_params = jnp.stack([da_lambda_q1[j], da_lambda_k1[j], da_lambda_q2[j], da_lambda_k2[j]]).astype(F32)
            g_col = da_subln_g[j].astype(F32).reshape(DA_V_DIM, 1)
            o = jnp.zeros((t_total, d), BF16) if len(segs) > 1 else None
            for ro, b, s in segs:
                o = _da_attention(qk, vt, lam_params, g_col, lambda_init, ro, b, s, o)
            os_, lses = [o.reshape(1, t_total, d)], []
            w_out = da_w_out[j]
        else:
            os_, lses = [], []
            gw = 3 * DIL_GROUP_WIDTH
            for g, (window, dilation) in enumerate(DIL_PATTERNS):
                assert window // (2 * dilation) == DIL_HALF
                proj = _proj(xb, dl_w_in[j][:, g * gw:(g + 1) * gw].astype(BF16), rope, pos_blocks,
                             q_cols=DIL_GROUP_WIDTH, rope_cols=2 * DIL_GROUP_WIDTH, q_scale=QK_SCALE_LOG2,
                             dilation=dilation)
                pair = None
                if len(segs) > 1:
                    shape = (dilation, t_total // dilation, DIL_GROUP_WIDTH)
                    pair = (jnp.zeros(shape, BF16), jnp.zeros(shape, F32))
                for ro, b, s in segs:
                    pair = _dil_attention_group(proj, g, dilation, ro, b, s, pair)
                os_.append(pair[0])
                lses.append(pair[1])
            w_out = dl_w_out[j]
        w_router = jnp.concatenate([
            moe_router_group[i].astype(F32),
            moe_router_expert[i].reshape(d, N_EXPERTS).astype(F32),
            jnp.zeros((d, LANES - MOE_GROUPS - N_EXPERTS), F32)], axis=1)
        xc = _post_attention(os_, lses, w_out.astype(BF16), x,
                             ln1_g[i].reshape(1, d).astype(F32), ln1_b[i].reshape(1, d).astype(F32), w_router)
        gid = xc[:, d + ROUTER_GROUP_LANE].astype(jnp.int32)
        pos, src, tile_group = _sort_by_group(gid, t_total, t_total + MOE_GROUPS * MOE_TM)
        f_sorted = _moe_ffn(xc, src, tile_group,
                            moe_w_gate[i].astype(BF16), moe_w_up[i].astype(BF16), moe_w_down[i].astype(BF16))
        x, xb = _ln2_gather(f_sorted, pos, xc,
                            ln2_g[i].reshape(1, d).astype(F32), ln2_b[i].reshape(1, d).astype(F32))

    outs = []
    for (ro, b, s), v in zip(segs, xs):
        outs.append(x[ro:ro + b * s].reshape(v.shape))
    return outs


def kernel(x_prompt, x_sample, da_w_in, da_w_out, da_lambda_q1, da_lambda_k1, da_lambda_q2, da_lambda_k2, da_subln_g, dl_w_in, dl_w_out, ln1_g, ln1_b, ln2_g, ln2_b, moe_router_group, moe_router_expert, moe_w_gate, moe_w_up, moe_w_down):
    y_prompt, y_sample = _trunk(
        [x_prompt, x_sample], da_w_in, da_w_out, da_lambda_q1, da_lambda_k1, da_lambda_q2, da_lambda_k2,
        da_subln_g, dl_w_in, dl_w_out, ln1_g, ln1_b, ln2_g, ln2_b,
        moe_router_group, moe_router_expert, moe_w_gate, moe_w_up, moe_w_down)
    return (y_prompt, y_sample)
```

```python
import functools
import math

import jax
import jax.numpy as jnp
import numpy as np
from jax import lax
from jax.experimental import pallas as pl
from jax.experimental.pallas import tpu as pltpu

F32 = jnp.float32
BF16 = jnp.bfloat16

LANES = 128
VMEM_LIMIT_BYTES = 56 * 1024 * 1024

D_MODEL = 1024
DEPTH = 4
N_MIXERS = 2

DA_HEADS = 8
DA_HEAD_DIM = 64
DA_V_DIM = 2 * DA_HEAD_DIM
DA_QK_WIDTH = DA_HEADS * DA_HEAD_DIM

DIL_PATTERNS = ((128, 1), (512, 4), (2048, 16))
DIL_GROUPS = len(DIL_PATTERNS)
DIL_HEADS = 16
DIL_HEAD_DIM = 64
DIL_GROUP_WIDTH = DIL_HEADS * DIL_HEAD_DIM
DIL_HALF = 64

ROPE_THETA = 500000.0
ROPE_ROT = DA_HEAD_DIM // 4
ROPE_HALF = ROPE_ROT // 2

LN_EPS = 1e-5
DEEPNORM_ALPHA = (2.0 * DEPTH) ** 0.25

MOE_GROUPS = 4
MOE_EXPERTS = 4
MOE_FF = 512
N_EXPERTS = MOE_GROUPS * MOE_EXPERTS

NEG_INF = -1e30
LOG2_E = math.log2(math.e)
QK_SCALE_LOG2 = DA_HEAD_DIM ** -0.5 * LOG2_E


def _params(*sem):
    return pltpu.CompilerParams(dimension_semantics=sem, vmem_limit_bytes=VMEM_LIMIT_BYTES)


PROJ_TM = 1024
PROJ_TN = 1024


def _rope_tables(max_len):
    inv = ROPE_THETA ** (-jnp.arange(ROPE_HALF, dtype=F32) * (2.0 / ROPE_ROT))
    ang = jnp.arange(max_len, dtype=F32)[:, None] * inv[None, :]
    cos, sin = jnp.cos(ang), jnp.sin(ang)
    ones = jnp.ones((max_len, DA_HEAD_DIM - ROPE_ROT), F32)
    zeros_h = jnp.zeros((max_len, ROPE_HALF), F32)
    zeros_r = jnp.zeros((max_len, DA_HEAD_DIM - ROPE_ROT), F32)
    c = jnp.concatenate([cos, cos, ones], axis=1)
    s_up = jnp.concatenate([zeros_h, sin, zeros_r], axis=1)
    s_dn = jnp.concatenate([-sin, zeros_h, zeros_r], axis=1)
    reps = LANES // DA_HEAD_DIM
    return jnp.stack([jnp.tile(c, (1, reps)), jnp.tile(s_up, (1, reps)), jnp.tile(s_dn, (1, reps))])


PROJ_CHUNK = 256


def _proj_kernel(pos_ref, x_ref, w_ref, rope_ref, o_ref, ybuf_ref, *, n_cols, q_cols, rope_cols, q_scale, dilation):
    del pos_ref
    x = x_ref[...]
    c = rope_ref[0]
    s_up = rope_ref[1]
    s_dn = rope_ref[2]
    rows = x.shape[0] // dilation
    for k in range(n_cols // PROJ_CHUNK):
        c0 = k * PROJ_CHUNK
        y = jnp.dot(x, w_ref[:, c0:c0 + PROJ_CHUNK], preferred_element_type=F32)
        if c0 < q_cols:
            y = y * q_scale
        parts = []
        for h in range(PROJ_CHUNK // LANES):
            yc = y[:, h * LANES:(h + 1) * LANES]
            if c0 < rope_cols:
                yc = yc * c + pltpu.roll(yc, ROPE_HALF, 1) * s_up + pltpu.roll(yc, LANES - ROPE_HALF, 1) * s_dn
            parts.append(yc)
        if dilation == 1:
            for h, yc in enumerate(parts):
                o_ref[0, :, c0 + h * LANES:c0 + (h + 1) * LANES] = yc.astype(o_ref.dtype)
        else:
            for h, yc in enumerate(parts):
                slot = (2 * k + h) % ybuf_ref.shape[0]
                ybuf_ref[slot] = yc
                for r in range(dilation):
                    o_ref[r, :, c0 + h * LANES:c0 + (h + 1) * LANES] = (
                        ybuf_ref[slot, pl.ds(r, rows, stride=dilation), :].astype(o_ref.dtype))


def _proj(xb, w, rope, pos_blocks, *, q_cols, rope_cols, q_scale, dilation=1):
    t, d = xb.shape
    n = w.shape[1]
    rows = PROJ_TM // dilation
    grid_spec = pltpu.PrefetchScalarGridSpec(
        num_scalar_prefetch=1,
        grid=(t // PROJ_TM,),
        in_specs=[
            pl.BlockSpec((PROJ_TM, d), lambda i, pos: (i, 0)),
            pl.BlockSpec((d, n), lambda i, pos: (0, 0)),
            pl.BlockSpec((3, PROJ_TM, LANES), lambda i, pos: (0, pos[i], 0)),
        ],
        out_specs=pl.BlockSpec((dilation, rows, n), lambda i, pos: (0, i, 0)),
        scratch_shapes=[pltpu.VMEM((4, PROJ_TM, LANES), F32)],
    )
    return pl.pallas_call(
        functools.partial(_proj_kernel, n_cols=n, q_cols=q_cols, rope_cols=rope_cols, q_scale=q_scale,
                          dilation=dilation),
        grid_spec=grid_spec,
        out_shape=jax.ShapeDtypeStruct((dilation, t // dilation, n), BF16),
        compiler_params=_params("parallel"),
        name="proj_rope",
    )(pos_blocks, xb, w, rope)


DA_TK = 512
DA_ONES_ROWS = 16
DA_VT_ROWS = DA_V_DIM + DA_ONES_ROWS


def _projt_kernel(x_ref, wt_ref, o_ref):
    y = lax.dot_general(wt_ref[...], x_ref[...], (((1,), (1,)), ((), ())), preferred_element_type=F32)
    h = o_ref.shape[0]
    o_ref[:, 0, 0:DA_V_DIM, :] = y.reshape(h, DA_V_DIM, DA_TK).astype(o_ref.dtype)
    o_ref[:, 0, DA_V_DIM:DA_VT_ROWS, :] = jnp.ones((h, DA_ONES_ROWS, DA_TK), o_ref.dtype)


def _proj_t(xb, wt):
    t, d = xb.shape
    n = wt.shape[0]
    h = n // DA_V_DIM
    return pl.pallas_call(
        _projt_kernel,
        grid=(t // DA_TK,),
        in_specs=[
            pl.BlockSpec((DA_TK, d), lambda i: (i, 0)),
            pl.BlockSpec((n, d), lambda i: (0, 0)),
        ],
        out_specs=pl.BlockSpec((h, 1, DA_VT_ROWS, DA_TK), lambda i: (0, i, 0, 0)),
        out_shape=jax.ShapeDtypeStruct((h, t // DA_TK, DA_VT_ROWS, DA_TK), BF16),
        compiler_params=_params("parallel"),
        name="proj_vt",
    )(xb, wt)


DA_TQ = 512


def _da_attn_kernel(lam_ref, g_ref, q_ref, k_ref, vt_ref, o_ref, q1_ref, q2_ref,
                    sa1_ref, sa2_ref, sb1_ref, sb2_ref, xa1_ref, xa2_ref, xb1_ref, xb2_ref,
                    m1_ref, a1_ref, m2_ref, a2_ref, *, n_kblk, lambda_init):
    buf_a = ((sa1_ref, xa1_ref), (sa2_ref, xa2_ref))
    buf_b = ((sb1_ref, xb1_ref), (sb2_ref, xb2_ref))
    qt = q_ref[...].astype(F32).T
    row = lax.broadcasted_iota(jnp.int32, qt.shape, 0)
    q1_ref[...] = jnp.where(row < DA_HEAD_DIM, qt, 0.0).astype(q1_ref.dtype)
    q2_ref[...] = jnp.where(row >= DA_HEAD_DIM, qt, 0.0).astype(q2_ref.dtype)
    for m_ref, a_ref in ((m1_ref, a1_ref), (m2_ref, a2_ref)):
        m_ref[...] = jnp.full(m_ref.shape, NEG_INF, F32)
        a_ref[...] = jnp.zeros(a_ref.shape, F32)

    def scores(j, buf):
        k = k_ref[pl.ds(pl.multiple_of(j * DA_TK, DA_TK), DA_TK), :]
        for qm_ref, (s_ref, x_ref) in zip((q1_ref, q2_ref), buf):
            s = jnp.dot(k, qm_ref[...], preferred_element_type=F32)
            s_ref[...] = s
            x_ref[...] = jnp.max(s.reshape(DA_TK // 8, 8, DA_TQ), axis=0)

    def softmax_pv(j, buf):
        vt = vt_ref[0, j]
        for (s_ref, x_ref), m_ref, a_ref in zip(buf, (m1_ref, m2_ref), (a1_ref, a2_ref)):
            m_old = m_ref[...]
            m_new = jnp.maximum(m_old, jnp.max(x_ref[...], axis=0, keepdims=True))
            p = jnp.exp2(s_ref[...] - m_new).astype(vt.dtype)
            a_ref[...] = jnp.exp2(m_old - m_new) * a_ref[...] + jnp.dot(vt, p, preferred_element_type=F32)
            m_ref[...] = m_new

    scores(0, buf_a)

    def pair(jj, carry):
        j = 2 * jj
        scores(j + 1, buf_b)
        softmax_pv(j, buf_a)
        scores(j + 2, buf_a)
        softmax_pv(j + 1, buf_b)
        return carry

    lax.fori_loop(0, n_kblk // 2 - 1, pair, 0)
    scores(n_kblk - 1, buf_b)
    softmax_pv(n_kblk - 2, buf_a)
    softmax_pv(n_kblk - 1, buf_b)

    lp = lam_ref[...]
    lam = (jnp.exp(jnp.sum(lp[0:1] * lp[1:2], axis=1, keepdims=True))
           - jnp.exp(jnp.sum(lp[2:3] * lp[3:4], axis=1, keepdims=True)) + lambda_init)
    a1 = a1_ref[...]
    a2 = a2_ref[...]
    o = (a1[0:DA_V_DIM] * (1.0 / a1[DA_V_DIM:DA_V_DIM + 1])
         - lam * (a2[0:DA_V_DIM] * (1.0 / a2[DA_V_DIM:DA_V_DIM + 1])))
    o = o * lax.rsqrt(jnp.mean(jnp.square(o), axis=0, keepdims=True) + LN_EPS)
    o = o * g_ref[...] * (1.0 - lambda_init)
    o_ref[...] = o.T.astype(o_ref.dtype)


def _da_attention(qk, vt, lam_params, subln_g, lambda_init, row_off, batch, seq):
    h = DA_HEADS
    nq = seq // DA_TQ
    nk = seq // DA_TK
    assert nk % 2 == 0 and nk >= 2
    qoff = row_off // DA_TQ
    soff = row_off // seq
    s_buf = pltpu.VMEM((DA_TK, DA_TQ), F32)
    x_buf = pltpu.VMEM((8, DA_TQ), F32)
    in_specs = [
        pl.BlockSpec((4, DA_HEAD_DIM), lambda b, hh, i: (0, 0)),
        pl.BlockSpec((DA_V_DIM, 1), lambda b, hh, i: (0, 0)),
        pl.BlockSpec((None, DA_TQ, LANES), lambda b, hh, i: (0, qoff + b * nq + i, hh)),
        pl.BlockSpec((None, seq, LANES), lambda b, hh, i: (0, soff + b, h + hh)),
        pl.BlockSpec((1, nk, DA_VT_ROWS, DA_TK), lambda b, hh, i: (hh, soff + b, 0, 0)),
    ]
    return pl.pallas_call(
        functools.partial(_da_attn_kernel, n_kblk=nk, lambda_init=lambda_init),
        grid=(batch, h, nq),
        in_specs=in_specs,
        out_specs=pl.BlockSpec((None, DA_TQ, DA_V_DIM), lambda b, hh, i: (0, b * nq + i, hh)),
        out_shape=jax.ShapeDtypeStruct((1, batch * seq, h * DA_V_DIM), BF16),
        scratch_shapes=[
            pltpu.VMEM((LANES, DA_TQ), BF16), pltpu.VMEM((LANES, DA_TQ), BF16),
            s_buf, s_buf, s_buf, s_buf,
            x_buf, x_buf, x_buf, x_buf,
            pltpu.VMEM((1, DA_TQ), F32), pltpu.VMEM((DA_VT_ROWS, DA_TQ), F32),
            pltpu.VMEM((1, DA_TQ), F32), pltpu.VMEM((DA_VT_ROWS, DA_TQ), F32),
        ],
        compiler_params=_params("parallel", "parallel", "arbitrary"),
        name="da_attn",
    )(lam_params, subln_g, qk, qk, vt)


DIL_QSUB = 128
DIL_KWIN = DIL_QSUB + 2 * DIL_HALF
DIL_QBLK = 1024


def _dil_attn_kernel(q_ref, k_ref, v_ref, o_ref, lse_ref, *, sub_len, qblk):
    qi = pl.program_id(3)
    lane = lax.broadcasted_iota(jnp.int32, (DIL_QSUB, LANES), 1)
    lo = lane < DIL_HEAD_DIM
    row = lax.broadcasted_iota(jnp.int32, (DIL_QSUB, DIL_KWIN), 0)
    col = lax.broadcasted_iota(jnp.int32, (DIL_QSUB, DIL_KWIN), 1)

    interior = jnp.abs(col - DIL_HALF - row) <= DIL_HALF
    n_sub = qblk // DIL_QSUB

    for n in range(n_sub):
        q0 = n * DIL_QSUB
        gq0 = qi * qblk + q0
        ws = jnp.clip(gq0 - DIL_HALF, 0, sub_len - DIL_KWIN)
        ws = pl.multiple_of(ws, DIL_HALF)
        qb = q_ref[pl.ds(q0, DIL_QSUB), :]
        kw = k_ref[pl.ds(ws, DIL_KWIN), :]
        vw = v_ref[pl.ds(ws, DIL_KWIN), :]
        if 0 < n < n_sub - 1:
            valid = interior
        else:
            valid = jnp.abs((ws + col) - (gq0 + row)) <= DIL_HALF
        zero = jnp.zeros_like(qb)
        outs, lses = [], []
        for qm in (jnp.where(lo, qb, zero), jnp.where(lo, zero, qb)):
            s = lax.dot_general(qm, kw, (((1,), (1,)), ((), ())), preferred_element_type=F32)
            s = jnp.where(valid, s, NEG_INF)
            m = jnp.max(s, axis=1, keepdims=True)
            p = jnp.exp2(s - m)
            l = jnp.sum(p, axis=1, keepdims=True)
            o = jnp.dot(p.astype(vw.dtype), vw, preferred_element_type=F32) * (1.0 / l)
            outs.append(o)
            lses.append(jnp.broadcast_to(m + jnp.log(l) * LOG2_E, (DIL_QSUB, LANES)))
        o_ref[pl.ds(q0, DIL_QSUB), :] = jnp.where(lo, outs[0], outs[1]).astype(o_ref.dtype)
        lse_ref[pl.ds(q0, DIL_QSUB), :] = jnp.where(lo, lses[0], lses[1])


def _dil_attention_group(proj, g, dilation, row_off, batch, seq):
    sub_len = seq // dilation
    qblk = min(DIL_QBLK, sub_len)
    nqb = sub_len // qblk
    hp = DIL_GROUP_WIDTH // LANES
    soff = row_off // seq

    def qmap(b, r, p, i):
        return (r, (soff + b) * nqb + i, p)

    def kmap(b, r, p, i):
        return (r, soff + b, hp + p)

    def vmap(b, r, p, i):
        return (r, soff + b, 2 * hp + p)

    def omap(b, r, p, i):
        return (r, b * nqb + i, p)

    return pl.pallas_call(
        functools.partial(_dil_attn_kernel, sub_len=sub_len, qblk=qblk),
        grid=(batch, dilation, hp, nqb),
        in_specs=[
            pl.BlockSpec((None, qblk, LANES), qmap),
            pl.BlockSpec((None, sub_len, LANES), kmap),
            pl.BlockSpec((None, sub_len, LANES), vmap),
        ],
        out_specs=[
            pl.BlockSpec((None, qblk, LANES), omap),
            pl.BlockSpec((None, qblk, LANES), omap),
        ],
        out_shape=[
            jax.ShapeDtypeStruct((dilation, batch * sub_len, DIL_GROUP_WIDTH), BF16),
            jax.ShapeDtypeStruct((dilation, batch * sub_len, DIL_GROUP_WIDTH), F32),
        ],
        compiler_params=_params("parallel", "parallel", "parallel", "arbitrary"),
        name=f"dil_attn_g{g}",
    )(proj, proj, proj)


POST_TM = 256


def _layer_norm_rows(z, g, b):
    mu = jnp.mean(z, axis=-1, keepdims=True)
    zc = z - mu
    var = jnp.mean(jnp.square(zc), axis=-1, keepdims=True)
    return zc * lax.rsqrt(var + LN_EPS) * g + b


def _token_order(seg_refs, seg_ends, scratch_ref):
    i = pl.program_id(0)
    dil, rows, width = seg_refs[0].shape
    if dil == 1:
        val = seg_refs[-1][0]
        for ref, end in reversed(list(zip(seg_refs[:-1], seg_ends[:-1]))):
            val = jnp.where(i < end, ref[0], val)
        return val.astype(F32)
    n_chunks = width // LANES
    start = 0
    for ref, end in zip(seg_refs, seg_ends):
        @pl.when((i >= start) & (i < end))
        def _(ref=ref):
            for r in range(dil):
                v = ref[r].astype(F32)
                for c in range(n_chunks):
                    scratch_ref[c, pl.ds(r, rows, stride=dil), :] = v[:, c * LANES:(c + 1) * LANES]
        start = end
    return jnp.concatenate([scratch_ref[c] for c in range(n_chunks)], axis=1)


def _post_kernel(*refs, n_groups, seg_ends):
    n_seg = len(seg_ends)
    o_refs = [refs[g * n_seg:(g + 1) * n_seg] for g in range(n_groups)]
    n_in = n_groups * n_seg
    lse_refs = []
    if n_groups > 1:
        lse_refs = [refs[n_in + g * n_seg:n_in + (g + 1) * n_seg] for g in range(n_groups)]
        n_in *= 2
    w_ref, x_ref, g_ref, b_ref, wr_ref, xc_ref = refs[n_in:n_in + 6]
    scratch = refs[n_in + 6:]
    d = x_ref.shape[1]
    if n_groups == 1:
        o = _token_order(o_refs[0], seg_ends, None).astype(BF16)
    else:
        lses = [_token_order(r, seg_ends, s) for r, s in zip(lse_refs, scratch[:n_groups])]
        m = functools.reduce(jnp.maximum, lses)
        es = [jnp.exp2(l - m) for l in lses]
        inv = 1.0 / functools.reduce(jnp.add, es)
        acc = None
        for e, seg_refs in zip(es, o_refs):
            term = (e * inv) * _token_order(seg_refs, seg_ends, scratch[n_groups])
            acc = term if acc is None else acc + term
        o = acc.astype(BF16)
    h = jnp.dot(o, w_ref[...], preferred_element_type=F32)
    y = _layer_norm_rows(DEEPNORM_ALPHA * x_ref[...] + h, g_ref[...], b_ref[...])
    xc_ref[:, 0:d] = y
    xc_ref[:, d:d + LANES] = _routing_lanes(y, wr_ref[...])


def _post_attention(os_, lses, seg_rows, w_out, x, g, b, w_router):
    t, d = x.shape
    n_groups = len(os_)
    seg_tiles = [r // POST_TM for r in seg_rows]
    seg_ends = tuple(int(v) for v in np.cumsum(seg_tiles))
    row = pl.BlockSpec((POST_TM, d), lambda i: (i, 0))
    full = pl.BlockSpec((d, d), lambda i: (0, 0))
    vec = pl.BlockSpec((1, d), lambda i: (0, 0))

    def seg_spec(a, s):
        first, count = seg_ends[s] - seg_tiles[s], seg_tiles[s]
        return pl.BlockSpec((a.shape[0], POST_TM // a.shape[0], d),
                            lambda i: (0, jnp.clip(i - first, 0, count - 1), 0))

    arrays = [a for grp in (*os_, *lses) for a in grp]
    specs = [seg_spec(a, s) for grp in (*os_, *lses) for s, a in enumerate(grp)]
    n_scratch = n_groups + 1 if n_groups > 1 else 0
    return pl.pallas_call(
        functools.partial(_post_kernel, n_groups=n_groups, seg_ends=seg_ends),
        grid=(t // POST_TM,),
        in_specs=specs + [full, row, vec, vec, pl.BlockSpec((d, LANES), lambda i: (0, 0))],
        out_specs=pl.BlockSpec((POST_TM, d + LANES), lambda i: (i, 0)),
        out_shape=jax.ShapeDtypeStruct((t, d + LANES), F32),
        scratch_shapes=[pltpu.VMEM((d // LANES, POST_TM, LANES), F32)] * n_scratch,
        compiler_params=_params("parallel"),
        name="post_attn_ln",
    )(*arrays, w_out, x, g, b, w_router)


ROUTER_TM = 512
ROUTER_EXPERT_LANE0 = MOE_GROUPS
ROUTER_GROUP_LANE = 0


def _first_argmax(vals, vmax, lane_f):
    return jnp.min(jnp.where(vals == vmax, lane_f, float(LANES)), axis=1, keepdims=True)


def _routing_lanes(x, w_router):
    xh = x.astype(BF16)
    xl = (x - xh.astype(F32)).astype(BF16)
    wh = w_router.astype(BF16)
    wl = (w_router - wh.astype(F32)).astype(BF16)
    logits = (jnp.dot(xh, wh, preferred_element_type=F32) + jnp.dot(xh, wl, preferred_element_type=F32)
              + jnp.dot(xl, wh, preferred_element_type=F32))
    lane_f = lax.broadcasted_iota(jnp.int32, logits.shape, 1).astype(F32)
    gl = jnp.where(lane_f < MOE_GROUPS, logits, NEG_INF)
    gmax = jnp.max(gl, axis=1, keepdims=True)
    gw = 1.0 / jnp.sum(jnp.exp(gl - gmax), axis=1, keepdims=True)
    gsel = _first_argmax(gl, gmax, lane_f)
    e0 = ROUTER_EXPERT_LANE0 + MOE_EXPERTS * gsel
    el = jnp.where((lane_f >= e0) & (lane_f < e0 + MOE_EXPERTS), logits, NEG_INF)
    v1 = jnp.max(el, axis=1, keepdims=True)
    i1 = _first_argmax(el, v1, lane_f)
    el2 = jnp.where(lane_f == i1, NEG_INF, el)
    v2 = jnp.max(el2, axis=1, keepdims=True)
    i2 = _first_argmax(el2, v2, lane_f)
    e2 = jnp.exp(v2 - v1)
    inv = gw / (1.0 + e2)
    comb = jnp.where(lane_f == i1, inv, 0.0) + jnp.where(lane_f == i2, e2 * inv, 0.0)
    return comb + jnp.where(lane_f == ROUTER_GROUP_LANE, gsel, 0.0)


MOE_TM = 512
GATHER_UNROLL = 8


def _issue_row_gather(idx_ref, src_hbm, dst_ref, sem, n_rows):
    def body(r, carry):
        pltpu.make_async_copy(src_hbm.at[pl.ds(idx_ref[0, r], 1), :], dst_ref.at[pl.ds(r, 1), :], sem).start()
        return carry
    lax.fori_loop(0, n_rows, body, 0, unroll=GATHER_UNROLL)


def _wait_row_gather(src_hbm, dst_ref, sem, n_rows):
    pltpu.make_async_copy(src_hbm.at[pl.ds(0, n_rows), :], dst_ref, sem).wait()


def _sort_by_group(gid, t, n_pad_rows):
    groups = jnp.arange(MOE_GROUPS, dtype=jnp.int32)
    onehot = (gid[:, None] == groups[None, :]).astype(jnp.int32)
    csum = jnp.cumsum(onehot, axis=0)
    counts = csum[-1]
    rank = jnp.sum(onehot * csum, axis=1) - 1
    tiles_g = (counts + MOE_TM - 1) // MOE_TM
    tile_end = jnp.cumsum(tiles_g)
    row_start = (tile_end - tiles_g) * MOE_TM
    pos = jnp.sum(onehot * row_start[None, :], axis=1) + rank
    src = jnp.zeros((n_pad_rows,), jnp.int32).at[pos].set(jnp.arange(t, dtype=jnp.int32))
    tile_ids = jnp.arange(n_pad_rows // MOE_TM, dtype=jnp.int32)
    tile_group = jnp.minimum(jnp.sum((tile_ids[:, None] >= tile_end[None, :]).astype(jnp.int32), axis=1),
                             MOE_GROUPS - 1)
    return pos, src, tile_group, tile_end[-1:]


def _moe_ffn_kernel(tg_ref, nv_ref, idx_cur_ref, idx_nxt_ref, xc_hbm, wg_ref, wu_ref, wd_ref, f_ref,
                    xbuf_ref, sem_ref):
    i = pl.program_id(0)
    slot = i % 2
    n_valid = nv_ref[0]
    d = f_ref.shape[1]

    @pl.when(i == 0)
    def _():
        _issue_row_gather(idx_cur_ref, xc_hbm, xbuf_ref.at[0], sem_ref.at[0], MOE_TM)

    @pl.when(i + 1 < n_valid)
    def _():
        _issue_row_gather(idx_nxt_ref, xc_hbm, xbuf_ref.at[1 - slot], sem_ref.at[1 - slot], MOE_TM)

    @pl.when(i < n_valid)
    def _():
        _wait_row_gather(xc_hbm, xbuf_ref.at[slot], sem_ref.at[slot], MOE_TM)
        xb = xbuf_ref[slot, :, 0:d].astype(BF16)
        comb = xbuf_ref[slot, :, d:d + LANES]
        lane = lax.broadcasted_iota(jnp.int32, comb.shape, 1)
        lane0 = ROUTER_EXPERT_LANE0 + MOE_EXPERTS * tg_ref[i]
        acc = None
        for e in range(MOE_EXPERTS):
            ce = jnp.sum(jnp.where(lane == lane0 + e, comb, 0.0), axis=1, keepdims=True)
            gate = jnp.dot(xb, wg_ref[0, e], preferred_element_type=F32)
            up = jnp.dot(xb, wu_ref[0, e], preferred_element_type=F32)
            hid = (gate * jax.nn.sigmoid(gate)) * up * ce
            part = jnp.dot(hid.astype(BF16), wd_ref[0, e], preferred_element_type=F32)
            acc = part if acc is None else acc + part
        f_ref[...] = acc

    @pl.when(i >= n_valid)
    def _():
        f_ref[...] = jnp.zeros(f_ref.shape, F32)


def _moe_ffn(xc, src, tile_group, n_valid, w_gate, w_up, w_down):
    d = xc.shape[1] - LANES
    n_tiles = tile_group.shape[0]
    _, ne, _, f = w_gate.shape
    idx = src.reshape(n_tiles, 1, MOE_TM)
    grid_spec = pltpu.PrefetchScalarGridSpec(
        num_scalar_prefetch=2,
        grid=(n_tiles,),
        in_specs=[
            pl.BlockSpec((None, 1, MOE_TM), lambda i, tg, nv: (i, 0, 0), memory_space=pltpu.SMEM),
            pl.BlockSpec((None, 1, MOE_TM), lambda i, tg, nv: (jnp.minimum(i + 1, n_tiles - 1), 0, 0),
                         memory_space=pltpu.SMEM),
            pl.BlockSpec(memory_space=pl.ANY),
            pl.BlockSpec((1, ne, d, f), lambda i, tg, nv: (tg[i], 0, 0, 0)),
            pl.BlockSpec((1, ne, d, f), lambda i, tg, nv: (tg[i], 0, 0, 0)),
            pl.BlockSpec((1, ne, f, d), lambda i, tg, nv: (tg[i], 0, 0, 0)),
        ],
        out_specs=pl.BlockSpec((MOE_TM, d), lambda i, tg, nv: (i, 0)),
        scratch_shapes=[pltpu.VMEM((2, MOE_TM, d + LANES), F32), pltpu.SemaphoreType.DMA((2,))],
    )
    return pl.pallas_call(
        _moe_ffn_kernel,
        grid_spec=grid_spec,
        out_shape=jax.ShapeDtypeStruct((n_tiles * MOE_TM, d), F32),
        compiler_params=_params("arbitrary"),
        name="moe_ffn",
    )(tile_group, n_valid, idx, idx, xc, w_gate, w_up, w_down)


LN2_TM = 512


LN2_CHUNK = 64


def _ln2_kernel(idx_cur_ref, idx_nxt_ref, f_hbm, x_ref, g_ref, b_ref, *rest, n_tiles, seg_ends):
    i = pl.program_id(0)
    if seg_ends is None:
        y_ref, yb_ref, fbuf_ref, sem_ref = rest
    else:
        seg_refs = rest[:len(seg_ends)]
        fbuf_ref, sem_ref, y_ref = rest[len(seg_ends):]
        yb_ref = None

    @pl.when(i == 0)
    def _():
        _issue_row_gather(idx_cur_ref, f_hbm, fbuf_ref.at[0], sem_ref.at[0], LN2_TM)

    def tile(cur, nxt):
        _wait_row_gather(f_hbm, fbuf_ref.at[cur], sem_ref.at[cur], LN2_TM)
        g = g_ref[...]
        b = b_ref[...]
        for c in range(LN2_TM // LN2_CHUNK):
            rows = pl.ds(c * LN2_CHUNK, LN2_CHUNK)
            for r in range(c * LN2_CHUNK, (c + 1) * LN2_CHUNK):
                pltpu.make_async_copy(f_hbm.at[pl.ds(idx_nxt_ref[0, r], 1), :],
                                      fbuf_ref.at[nxt, pl.ds(r, 1), :], sem_ref.at[nxt]).start()
            y = _layer_norm_rows(DEEPNORM_ALPHA * x_ref[rows, :] + fbuf_ref[cur, rows, :], g, b)
            y_ref[rows, :] = y
            if yb_ref is not None:
                yb_ref[rows, :] = y.astype(BF16)

    @pl.when(i % 2 == 0)
    def _():
        tile(0, 1)

    @pl.when(i % 2 == 1)
    def _():
        tile(1, 0)

    if seg_ends is not None:
        start = 0
        for ref, end in zip(seg_refs, seg_ends):
            @pl.when((i >= start) & (i < end))
            def _(ref=ref):
                ref[...] = y_ref[...]
            start = end

    @pl.when(i == n_tiles - 1)
    def _():
        spare = n_tiles % 2
        _wait_row_gather(f_hbm, fbuf_ref.at[spare], sem_ref.at[spare], LN2_TM)


def _ln2_gather(f_sorted, pos, x, g, b, seg_rows=None):
    t = x.shape[0]
    d = f_sorted.shape[1]
    n_tiles = t // LN2_TM
    idx = pos.reshape(n_tiles, 1, LN2_TM)
    row = pl.BlockSpec((LN2_TM, d), lambda i: (i, 0))
    vec = pl.BlockSpec((1, d), lambda i: (0, 0))
    scratch = [pltpu.VMEM((2, LN2_TM, d), F32), pltpu.SemaphoreType.DMA((2,))]
    if seg_rows is None:
        seg_ends = None
        out_specs = [row, row]
        out_shape = [jax.ShapeDtypeStruct((t, d), F32), jax.ShapeDtypeStruct((t, d), BF16)]
    else:
        seg_tiles = [r // LN2_TM for r in seg_rows]
        seg_ends = tuple(int(v) for v in np.cumsum(seg_tiles))

        def seg_spec(s):
            first, count = seg_ends[s] - seg_tiles[s], seg_tiles[s]
            return pl.BlockSpec((LN2_TM, d), lambda i: (jnp.clip(i - first, 0, count - 1), 0))

        out_specs = [seg_spec(s) for s in range(len(seg_rows))]
        out_shape = [jax.ShapeDtypeStruct((r, d), F32) for r in seg_rows]
        scratch.append(pltpu.VMEM((LN2_TM, d), F32))
    return pl.pallas_call(
        functools.partial(_ln2_kernel, n_tiles=n_tiles, seg_ends=seg_ends),
        grid=(n_tiles,),
        in_specs=[
            pl.BlockSpec((None, 1, LN2_TM), lambda i: (i, 0, 0), memory_space=pltpu.SMEM),
            pl.BlockSpec((None, 1, LN2_TM), lambda i: (jnp.minimum(i + 1, n_tiles - 1), 0, 0),
                         memory_space=pltpu.SMEM),
            pl.BlockSpec(memory_space=pl.ANY),
            row, vec, vec,
        ],
        out_specs=out_specs,
        out_shape=out_shape,
        scratch_shapes=scratch,
        compiler_params=_params("arbitrary"),
        name="ln2_gather",
    )(idx, idx, f_sorted, x, g, b)


def _da_weight_layout(w_in):
    d = w_in.shape[0]
    w = DA_QK_WIDTH
    q1, q2, k1, k2 = (w_in[:, i * w:(i + 1) * w].reshape(d, DA_HEADS, DA_HEAD_DIM) for i in range(4))
    qk = jnp.concatenate([jnp.concatenate([q1, q2], axis=2).reshape(d, 2 * w),
                          jnp.concatenate([k1, k2], axis=2).reshape(d, 2 * w)], axis=1)
    return qk, w_in[:, 4 * w:]


def _trunk(xs, da_w_in, da_w_out, da_lambda_q1, da_lambda_k1, da_lambda_q2, da_lambda_k2, da_subln_g,
           dl_w_in, dl_w_out, ln1_g, ln1_b, ln2_g, ln2_b,
           moe_router_group, moe_router_expert, moe_w_gate, moe_w_up, moe_w_down):
    d = xs[0].shape[-1]
    segs = []
    off = 0
    for x in xs:
        segs.append((off, x.shape[0], x.shape[1]))
        off += x.shape[0] * x.shape[1]
    t_total = off
    x = jnp.concatenate([v.reshape(-1, d) for v in xs], axis=0)
    xb = x.astype(BF16)

    max_len = max(s for _, _, s in segs)
    rope = _rope_tables(max_len)
    pos_blocks = np.concatenate([
        np.tile(np.arange(s // PROJ_TM, dtype=np.int32), b) for _, b, s in segs])
    pos_blocks = jnp.asarray(pos_blocks)

    for i in range(DEPTH):
        j = i // N_MIXERS
        if i % N_MIXERS == 0:
            lambda_init = 0.8 - 0.6 * math.exp(-0.3 * i)
            w_qk, w_v = _da_weight_layout(da_w_in[j])
            qk = _proj(xb, w_qk.astype(BF16), rope, pos_blocks,
                       q_cols=2 * DA_QK_WIDTH, rope_cols=4 * DA_QK_WIDTH, q_scale=QK_SCALE_LOG2)
            vt = _proj_t(xb, w_v.T.astype(BF16))
            lam_params = jnp.stack([da_lambda_q1[j], da_lambda_k1[j], da_lambda_q2[j], da_lambda_k2[j]]).astype(F32)
            g_col = da_subln_g[j].astype(F32).reshape(DA_V_DIM, 1)
            os_ = [[_da_attention(qk, vt, lam_params, g_col, lambda_init, ro, b, s) for ro, b, s in segs]]
            lses = []
            w_out = da_w_out[j]
        else:
            os_, lses = [], []
            gw = 3 * DIL_GROUP_WIDTH
            for g, (window, dilation) in enumerate(DIL_PATTERNS):
                assert window // (2 * dilation) == DIL_HALF
                proj = _proj(xb, dl_w_in[j][:, g * gw:(g + 1) * gw].astype(BF16), rope, pos_blocks,
                             q_cols=DIL_GROUP_WIDTH, rope_cols=2 * DIL_GROUP_WIDTH, q_scale=QK_SCALE_LOG2,
                             dilation=dilation)
                pairs = [_dil_attention_group(proj, g, dilation, ro, b, s) for ro, b, s in segs]
                os_.append([p[0] for p in pairs])
                lses.append([p[1] for p in pairs])
            w_out = dl_w_out[j]
        w_router = jnp.concatenate([
            moe_router_group[i].astype(F32),
            moe_router_expert[i].reshape(d, N_EXPERTS).astype(F32),
            jnp.zeros((d, LANES - MOE_GROUPS - N_EXPERTS), F32)], axis=1)
        xc = _post_attention(os_, lses, [b * s for _, b, s in segs], w_out.astype(BF16), x,
                             ln1_g[i].reshape(1, d).astype(F32), ln1_b[i].reshape(1, d).astype(F32), w_router)
        gid = xc[:, d + ROUTER_GROUP_LANE].astype(jnp.int32)
        pos, src, tile_group, n_valid = _sort_by_group(gid, t_total, t_total + MOE_GROUPS * MOE_TM)
        f_sorted = _moe_ffn(xc, src, tile_group, n_valid,
                            moe_w_gate[i].astype(BF16), moe_w_up[i].astype(BF16), moe_w_down[i].astype(BF16))
        ln2 = (ln2_g[i].reshape(1, d).astype(F32), ln2_b[i].reshape(1, d).astype(F32))
        if i + 1 < DEPTH:
            x, xb = _ln2_gather(f_sorted, pos, xc, *ln2)
        else:
            ys = _ln2_gather(f_sorted, pos, xc, *ln2, seg_rows=[b * s for _, b, s in segs])
    return [y.reshape(v.shape) for y, v in zip(ys, xs)]


def kernel(x_prompt, x_sample, da_w_in, da_w_out, da_lambda_q1, da_lambda_k1, da_lambda_q2, da_lambda_k2, da_subln_g, dl_w_in, dl_w_out, ln1_g, ln1_b, ln2_g, ln2_b, moe_router_group, moe_router_expert, moe_w_gate, moe_w_up, moe_w_down):
    y_prompt, y_sample = _trunk(
        [x_prompt, x_sample], da_w_in, da_w_out, da_lambda_q1, da_lambda_k1, da_lambda_q2, da_lambda_k2,
        da_subln_g, dl_w_in, dl_w_out, ln1_g, ln1_b, ln2_g, ln2_b,
        moe_router_group, moe_router_expert, moe_w_gate, moe_w_up, moe_w_down)
    return (y_prompt, y_sample)
```

```python
import functools
import math

import jax
import jax.numpy as jnp
import numpy as np
from jax import lax
from jax.experimental import pallas as pl
from jax.experimental.pallas import tpu as pltpu

F32 = jnp.float32
BF16 = jnp.bfloat16

LANES = 128
VMEM_LIMIT_BYTES = 56 * 1024 * 1024

D_MODEL = 1024
DEPTH = 4
N_MIXERS = 2

DA_HEADS = 8
DA_HEAD_DIM = 64
DA_V_DIM = 2 * DA_HEAD_DIM
DA_QK_WIDTH = DA_HEADS * DA_HEAD_DIM

DIL_PATTERNS = ((128, 1), (512, 4), (2048, 16))
DIL_GROUPS = len(DIL_PATTERNS)
DIL_HEADS = 16
DIL_HEAD_DIM = 64
DIL_GROUP_WIDTH = DIL_HEADS * DIL_HEAD_DIM
DIL_HALF = 64

ROPE_THETA = 500000.0
ROPE_ROT = DA_HEAD_DIM // 4
ROPE_HALF = ROPE_ROT // 2

LN_EPS = 1e-5
DEEPNORM_ALPHA = (2.0 * DEPTH) ** 0.25

MOE_GROUPS = 4
MOE_EXPERTS = 4
MOE_FF = 512
N_EXPERTS = MOE_GROUPS * MOE_EXPERTS

NEG_INF = -1e30
LOG2_E = math.log2(math.e)
QK_SCALE_LOG2 = DA_HEAD_DIM ** -0.5 * LOG2_E


def _params(*sem):
    return pltpu.CompilerParams(dimension_semantics=sem, vmem_limit_bytes=VMEM_LIMIT_BYTES)


PROJ_TM = 1024
PROJ_TN = 1024


def _rope_tables(max_len):
    inv = ROPE_THETA ** (-jnp.arange(ROPE_HALF, dtype=F32) * (2.0 / ROPE_ROT))
    ang = jnp.arange(max_len, dtype=F32)[:, None] * inv[None, :]
    cos, sin = jnp.cos(ang), jnp.sin(ang)
    ones = jnp.ones((max_len, DA_HEAD_DIM - ROPE_ROT), F32)
    zeros_h = jnp.zeros((max_len, ROPE_HALF), F32)
    zeros_r = jnp.zeros((max_len, DA_HEAD_DIM - ROPE_ROT), F32)
    c = jnp.concatenate([cos, cos, ones], axis=1)
    s_up = jnp.concatenate([zeros_h, sin, zeros_r], axis=1)
    s_dn = jnp.concatenate([-sin, zeros_h, zeros_r], axis=1)
    reps = LANES // DA_HEAD_DIM
    return jnp.stack([jnp.tile(c, (1, reps)), jnp.tile(s_up, (1, reps)), jnp.tile(s_dn, (1, reps))])


PROJ_MAX_STRIDE = 4
PROJ_CHUNK = 256


def _proj_kernel(pos_ref, x_ref, w_ref, rope_ref, o_ref, ybuf_ref, zbuf_ref, *, n_cols, q_cols, rope_cols, q_scale, dilation):
    del pos_ref
    x = x_ref[...]
    c = rope_ref[0]
    s_up = rope_ref[1]
    s_dn = rope_ref[2]
    rows = x.shape[0] // dilation
    for k in range(n_cols // PROJ_CHUNK):
        c0 = k * PROJ_CHUNK
        y = jnp.dot(x, w_ref[:, c0:c0 + PROJ_CHUNK], preferred_element_type=F32)
        if c0 < q_cols:
            y = y * q_scale
        parts = []
        for h in range(PROJ_CHUNK // LANES):
            yc = y[:, h * LANES:(h + 1) * LANES]
            if c0 < rope_cols:
                yc = yc * c + pltpu.roll(yc, ROPE_HALF, 1) * s_up + pltpu.roll(yc, LANES - ROPE_HALF, 1) * s_dn
            parts.append(yc)
        if dilation == 1:
            for h, yc in enumerate(parts):
                o_ref[0, :, c0 + h * LANES:c0 + (h + 1) * LANES] = yc.astype(o_ref.dtype)
        else:
            for h, yc in enumerate(parts):
                slot = (2 * k + h) % ybuf_ref.shape[0]
                cols = slice(c0 + h * LANES, c0 + (h + 1) * LANES)
                ybuf_ref[slot] = yc
                if dilation <= PROJ_MAX_STRIDE:
                    for r in range(dilation):
                        o_ref[r, :, cols] = ybuf_ref[slot, pl.ds(r, rows, stride=dilation), :].astype(o_ref.dtype)
                else:
                    outer = dilation // PROJ_MAX_STRIDE
                    zslot = h % zbuf_ref.shape[0]
                    for r in range(PROJ_MAX_STRIDE):
                        zbuf_ref[zslot, r] = ybuf_ref[slot, pl.ds(r, rows * outer, stride=PROJ_MAX_STRIDE), :]
                        for a in range(outer):
                            o_ref[r + PROJ_MAX_STRIDE * a, :, cols] = (
                                zbuf_ref[zslot, r, pl.ds(a, rows, stride=outer), :].astype(o_ref.dtype))


def _proj(xb, w, rope, pos_blocks, *, q_cols, rope_cols, q_scale, dilation=1):
    t, d = xb.shape
    n = w.shape[1]
    rows = PROJ_TM // dilation
    grid_spec = pltpu.PrefetchScalarGridSpec(
        num_scalar_prefetch=1,
        grid=(t // PROJ_TM,),
        in_specs=[
            pl.BlockSpec((PROJ_TM, d), lambda i, pos: (i, 0)),
            pl.BlockSpec((d, n), lambda i, pos: (0, 0)),
            pl.BlockSpec((3, PROJ_TM, LANES), lambda i, pos: (0, pos[i], 0)),
        ],
        out_specs=pl.BlockSpec((dilation, rows, n), lambda i, pos: (0, i, 0)),
        scratch_shapes=[pltpu.VMEM((4, PROJ_TM, LANES), F32),
                        pltpu.VMEM((2, PROJ_MAX_STRIDE, PROJ_TM // PROJ_MAX_STRIDE, LANES), F32)],
    )
    return pl.pallas_call(
        functools.partial(_proj_kernel, n_cols=n, q_cols=q_cols, rope_cols=rope_cols, q_scale=q_scale,
                          dilation=dilation),
        grid_spec=grid_spec,
        out_shape=jax.ShapeDtypeStruct((dilation, t // dilation, n), BF16),
        compiler_params=_params("parallel"),
        name="proj_rope",
    )(pos_blocks, xb, w, rope)


DA_TK = 512
DA_ONES_ROWS = 16
DA_VT_ROWS = DA_V_DIM + DA_ONES_ROWS


def _projt_kernel(x_ref, wt_ref, o_ref):
    y = lax.dot_general(wt_ref[...], x_ref[...], (((1,), (1,)), ((), ())), preferred_element_type=F32)
    h = o_ref.shape[0]
    o_ref[:, 0, 0:DA_V_DIM, :] = y.reshape(h, DA_V_DIM, DA_TK).astype(o_ref.dtype)
    o_ref[:, 0, DA_V_DIM:DA_VT_ROWS, :] = jnp.ones((h, DA_ONES_ROWS, DA_TK), o_ref.dtype)


def _proj_t(xb, wt):
    t, d = xb.shape
    n = wt.shape[0]
    h = n // DA_V_DIM
    return pl.pallas_call(
        _projt_kernel,
        grid=(t // DA_TK,),
        in_specs=[
            pl.BlockSpec((DA_TK, d), lambda i: (i, 0)),
            pl.BlockSpec((n, d), lambda i: (0, 0)),
        ],
        out_specs=pl.BlockSpec((h, 1, DA_VT_ROWS, DA_TK), lambda i: (0, i, 0, 0)),
        out_shape=jax.ShapeDtypeStruct((h, t // DA_TK, DA_VT_ROWS, DA_TK), BF16),
        compiler_params=_params("parallel"),
        name="proj_vt",
    )(xb, wt)


DA_TQ = 512


def _da_attn_kernel(lam_ref, g_ref, q_ref, k_ref, vt_ref, o_ref, q1_ref, q2_ref,
                    sa1_ref, sa2_ref, sb1_ref, sb2_ref, xa1_ref, xa2_ref, xb1_ref, xb2_ref,
                    m1_ref, a1_ref, m2_ref, a2_ref, *, n_kblk, lambda_init):
    buf_a = ((sa1_ref, xa1_ref), (sa2_ref, xa2_ref))
    buf_b = ((sb1_ref, xb1_ref), (sb2_ref, xb2_ref))
    qt = q_ref[...].astype(F32).T
    row = lax.broadcasted_iota(jnp.int32, qt.shape, 0)
    q1_ref[...] = jnp.where(row < DA_HEAD_DIM, qt, 0.0).astype(q1_ref.dtype)
    q2_ref[...] = jnp.where(row >= DA_HEAD_DIM, qt, 0.0).astype(q2_ref.dtype)
    for m_ref, a_ref in ((m1_ref, a1_ref), (m2_ref, a2_ref)):
        m_ref[...] = jnp.full(m_ref.shape, NEG_INF, F32)
        a_ref[...] = jnp.zeros(a_ref.shape, F32)

    def scores(j, buf):
        k = k_ref[pl.ds(pl.multiple_of(j * DA_TK, DA_TK), DA_TK), :]
        for qm_ref, (s_ref, x_ref) in zip((q1_ref, q2_ref), buf):
            s = jnp.dot(k, qm_ref[...], preferred_element_type=F32)
            s_ref[...] = s
            x_ref[...] = jnp.max(s.reshape(DA_TK // 8, 8, DA_TQ), axis=0)

    def softmax_pv(j, buf):
        vt = vt_ref[0, j]
        for (s_ref, x_ref), m_ref, a_ref in zip(buf, (m1_ref, m2_ref), (a1_ref, a2_ref)):
            m_old = m_ref[...]
            m_new = jnp.maximum(m_old, jnp.max(x_ref[...], axis=0, keepdims=True))
            p = jnp.exp2(s_ref[...] - m_new).astype(vt.dtype)
            a_ref[...] = jnp.exp2(m_old - m_new) * a_ref[...] + jnp.dot(vt, p, preferred_element_type=F32)
            m_ref[...] = m_new

    scores(0, buf_a)

    def pair(jj, carry):
        j = 2 * jj
        scores(j + 1, buf_b)
        softmax_pv(j, buf_a)
        scores(j + 2, buf_a)
        softmax_pv(j + 1, buf_b)
        return carry

    lax.fori_loop(0, n_kblk // 2 - 1, pair, 0)
    scores(n_kblk - 1, buf_b)
    softmax_pv(n_kblk - 2, buf_a)
    softmax_pv(n_kblk - 1, buf_b)

    lp = lam_ref[...]
    lam = (jnp.exp(jnp.sum(lp[0:1] * lp[1:2], axis=1, keepdims=True))
           - jnp.exp(jnp.sum(lp[2:3] * lp[3:4], axis=1, keepdims=True)) + lambda_init)
    a1 = a1_ref[...]
    a2 = a2_ref[...]
    o = (a1[0:DA_V_DIM] * (1.0 / a1[DA_V_DIM:DA_V_DIM + 1])
         - lam * (a2[0:DA_V_DIM] * (1.0 / a2[DA_V_DIM:DA_V_DIM + 1])))
    o = o * lax.rsqrt(jnp.mean(jnp.square(o), axis=0, keepdims=True) + LN_EPS)
    o = o * g_ref[...] * (1.0 - lambda_init)
    o_ref[...] = o.T.astype(o_ref.dtype)


def _da_attention(qk, vt, lam_params, subln_g, lambda_init, row_off, batch, seq):
    h = DA_HEADS
    nq = seq // DA_TQ
    nk = seq // DA_TK
    assert nk % 2 == 0 and nk >= 2
    qoff = row_off // DA_TQ
    soff = row_off // seq
    s_buf = pltpu.VMEM((DA_TK, DA_TQ), F32)
    x_buf = pltpu.VMEM((8, DA_TQ), F32)
    in_specs = [
        pl.BlockSpec((4, DA_HEAD_DIM), lambda b, hh, i: (0, 0)),
        pl.BlockSpec((DA_V_DIM, 1), lambda b, hh, i: (0, 0)),
        pl.BlockSpec((None, DA_TQ, LANES), lambda b, hh, i: (0, qoff + b * nq + i, hh)),
        pl.BlockSpec((None, seq, LANES), lambda b, hh, i: (0, soff + b, h + hh)),
        pl.BlockSpec((1, nk, DA_VT_ROWS, DA_TK), lambda b, hh, i: (hh, soff + b, 0, 0)),
    ]
    return pl.pallas_call(
        functools.partial(_da_attn_kernel, n_kblk=nk, lambda_init=lambda_init),
        grid=(batch, h, nq),
        in_specs=in_specs,
        out_specs=pl.BlockSpec((None, DA_TQ, DA_V_DIM), lambda b, hh, i: (0, b * nq + i, hh)),
        out_shape=jax.ShapeDtypeStruct((1, batch * seq, h * DA_V_DIM), BF16),
        scratch_shapes=[
            pltpu.VMEM((LANES, DA_TQ), BF16), pltpu.VMEM((LANES, DA_TQ), BF16),
            s_buf, s_buf, s_buf, s_buf,
            x_buf, x_buf, x_buf, x_buf,
            pltpu.VMEM((1, DA_TQ), F32), pltpu.VMEM((DA_VT_ROWS, DA_TQ), F32),
            pltpu.VMEM((1, DA_TQ), F32), pltpu.VMEM((DA_VT_ROWS, DA_TQ), F32),
        ],
        compiler_params=_params("parallel", "parallel", "arbitrary"),
        name="da_attn",
    )(lam_params, subln_g, qk, qk, vt)


DIL_QSUB = 128
DIL_KWIN = DIL_QSUB + 2 * DIL_HALF
DIL_QBLK = 1024


def _dil_attn_kernel(q_ref, k_ref, v_ref, o_ref, lse_ref, *, sub_len, qblk):
    qi = pl.program_id(3)
    lane = lax.broadcasted_iota(jnp.int32, (DIL_QSUB, LANES), 1)
    lo = lane < DIL_HEAD_DIM
    row = lax.broadcasted_iota(jnp.int32, (DIL_QSUB, DIL_KWIN), 0)
    col = lax.broadcasted_iota(jnp.int32, (DIL_QSUB, DIL_KWIN), 1)

    interior = jnp.abs(col - DIL_HALF - row) <= DIL_HALF
    n_sub = qblk // DIL_QSUB

    for n in range(n_sub):
        q0 = n * DIL_QSUB
        gq0 = qi * qblk + q0
        ws = jnp.clip(gq0 - DIL_HALF, 0, sub_len - DIL_KWIN)
        ws = pl.multiple_of(ws, DIL_HALF)
        qb = q_ref[pl.ds(q0, DIL_QSUB), :]
        kw = k_ref[pl.ds(ws, DIL_KWIN), :]
        vw = v_ref[pl.ds(ws, DIL_KWIN), :]
        if 0 < n < n_sub - 1:
            valid = interior
        else:
            valid = jnp.abs((ws + col) - (gq0 + row)) <= DIL_HALF
        zero = jnp.zeros_like(qb)
        outs, lses = [], []
        for qm in (jnp.where(lo, qb, zero), jnp.where(lo, zero, qb)):
            s = lax.dot_general(qm, kw, (((1,), (1,)), ((), ())), preferred_element_type=F32)
            s = jnp.where(valid, s, NEG_INF)
            m = jnp.max(s, axis=1, keepdims=True)
            p = jnp.exp2(s - m)
            l = jnp.sum(p, axis=1, keepdims=True)
            o = jnp.dot(p.astype(vw.dtype), vw, preferred_element_type=F32) * (1.0 / l)
            outs.append(o)
            lses.append(jnp.broadcast_to(m + jnp.log(l) * LOG2_E, (DIL_QSUB, LANES)))
        o_ref[pl.ds(q0, DIL_QSUB), :] = jnp.where(lo, outs[0], outs[1]).astype(o_ref.dtype)
        lse_ref[pl.ds(q0, DIL_QSUB), :] = jnp.where(lo, lses[0], lses[1])


def _dil_attention_group(proj, g, dilation, row_off, batch, seq):
    sub_len = seq // dilation
    qblk = min(DIL_QBLK, sub_len)
    nqb = sub_len // qblk
    hp = DIL_GROUP_WIDTH // LANES
    soff = row_off // seq

    def qmap(b, r, p, i):
        return (r, (soff + b) * nqb + i, p)

    def kmap(b, r, p, i):
        return (r, soff + b, hp + p)

    def vmap(b, r, p, i):
        return (r, soff + b, 2 * hp + p)

    def omap(b, r, p, i):
        return (r, b * nqb + i, p)

    return pl.pallas_call(
        functools.partial(_dil_attn_kernel, sub_len=sub_len, qblk=qblk),
        grid=(batch, dilation, hp, nqb),
        in_specs=[
            pl.BlockSpec((None, qblk, LANES), qmap),
            pl.BlockSpec((None, sub_len, LANES), kmap),
            pl.BlockSpec((None, sub_len, LANES), vmap),
        ],
        out_specs=[
            pl.BlockSpec((None, qblk, LANES), omap),
            pl.BlockSpec((None, qblk, LANES), omap),
        ],
        out_shape=[
            jax.ShapeDtypeStruct((dilation, batch * sub_len, DIL_GROUP_WIDTH), BF16),
            jax.ShapeDtypeStruct((dilation, batch * sub_len, DIL_GROUP_WIDTH), F32),
        ],
        compiler_params=_params("parallel", "parallel", "parallel", "arbitrary"),
        name=f"dil_attn_g{g}",
    )(proj, proj, proj)


POST_TM = 256


def _layer_norm_rows(z, g, b):
    mu = jnp.mean(z, axis=-1, keepdims=True)
    zc = z - mu
    var = jnp.mean(jnp.square(zc), axis=-1, keepdims=True)
    return zc * lax.rsqrt(var + LN_EPS) * g + b


def _token_order_reader(seg_refs, seg_ends, scratch_ref):
    i = pl.program_id(0)
    dil, rows, width = seg_refs[0].shape
    if dil == 1:
        def read(c):
            cs = slice(c * LANES, (c + 1) * LANES)
            val = seg_refs[-1][0, :, cs]
            for ref, end in reversed(list(zip(seg_refs[:-1], seg_ends[:-1]))):
                val = jnp.where(i < end, ref[0, :, cs], val)
            return val.astype(F32)
        return read
    start = 0
    for ref, end in zip(seg_refs, seg_ends):
        @pl.when((i >= start) & (i < end))
        def _(ref=ref):
            for r in range(dil):
                for c in range(width // LANES):
                    scratch_ref[c, pl.ds(r, rows, stride=dil), :] = (
                        ref[r, :, c * LANES:(c + 1) * LANES].astype(F32))
        start = end
    return lambda c: scratch_ref[c]


def _post_kernel(*refs, n_groups, seg_ends):
    n_seg = len(seg_ends)
    o_refs = [refs[g * n_seg:(g + 1) * n_seg] for g in range(n_groups)]
    n_in = n_groups * n_seg
    lse_refs = []
    if n_groups > 1:
        lse_refs = [refs[n_in + g * n_seg:n_in + (g + 1) * n_seg] for g in range(n_groups)]
        n_in *= 2
    w_ref, x_ref, g_ref, b_ref, wr_ref, xc_ref = refs[n_in:n_in + 6]
    scratch = refs[n_in + 6:]
    d = x_ref.shape[1]
    o_scr = scratch[-1]
    spare = iter(scratch[:-1])
    readers = [_token_order_reader(r, seg_ends, next(spare) if r[0].shape[0] > 1 else None)
               for r in (*o_refs, *lse_refs)]
    read_o, read_lse = readers[:n_groups], readers[n_groups:]
    for c in range(d // LANES):
        if n_groups == 1:
            o_c = read_o[0](c)
        else:
            lses = [rd(c) for rd in read_lse]
            m = functools.reduce(jnp.maximum, lses)
            es = [jnp.exp2(l - m) for l in lses]
            inv = 1.0 / functools.reduce(jnp.add, es)
            o_c = functools.reduce(jnp.add, [(e * inv) * rd(c) for e, rd in zip(es, read_o)])
        o_scr[:, c * LANES:(c + 1) * LANES] = o_c.astype(BF16)
    h = jnp.dot(o_scr[...], w_ref[...], preferred_element_type=F32)
    y = _layer_norm_rows(DEEPNORM_ALPHA * x_ref[...] + h, g_ref[...], b_ref[...])
    xc_ref[:, 0:d] = y
    xc_ref[:, d:d + LANES] = _routing_lanes(y, wr_ref[...])


def _post_attention(os_, lses, seg_rows, w_out, x, g, b, w_router):
    t, d = x.shape
    n_groups = len(os_)
    seg_tiles = [r // POST_TM for r in seg_rows]
    seg_ends = tuple(int(v) for v in np.cumsum(seg_tiles))
    row = pl.BlockSpec((POST_TM, d), lambda i: (i, 0))
    full = pl.BlockSpec((d, d), lambda i: (0, 0))
    vec = pl.BlockSpec((1, d), lambda i: (0, 0))

    def seg_spec(a, s):
        first, count = seg_ends[s] - seg_tiles[s], seg_tiles[s]
        return pl.BlockSpec((a.shape[0], POST_TM // a.shape[0], d),
                            lambda i: (0, jnp.clip(i - first, 0, count - 1), 0))

    arrays = [a for grp in (*os_, *lses) for a in grp]
    specs = [seg_spec(a, s) for grp in (*os_, *lses) for s, a in enumerate(grp)]
    n_scratch = sum(1 for grp in (*os_, *lses) if grp[0].shape[0] > 1)
    return pl.pallas_call(
        functools.partial(_post_kernel, n_groups=n_groups, seg_ends=seg_ends),
        grid=(t // POST_TM,),
        in_specs=specs + [full, row, vec, vec, pl.BlockSpec((d, LANES), lambda i: (0, 0))],
        out_specs=pl.BlockSpec((POST_TM, d + LANES), lambda i: (i, 0)),
        out_shape=jax.ShapeDtypeStruct((t, d + LANES), F32),
        scratch_shapes=[pltpu.VMEM((d // LANES, POST_TM, LANES), F32)] * n_scratch + [pltpu.VMEM((POST_TM, d), BF16)],
        compiler_params=_params("parallel"),
        name="post_attn_ln",
    )(*arrays, w_out, x, g, b, w_router)


ROUTER_TM = 512
ROUTER_EXPERT_LANE0 = MOE_GROUPS
ROUTER_GROUP_LANE = 0


def _first_argmax(vals, vmax, lane_f):
    return jnp.min(jnp.where(vals == vmax, lane_f, float(LANES)), axis=1, keepdims=True)


def _routing_lanes(x, w_router):
    xh = x.astype(BF16)
    xl = (x - xh.astype(F32)).astype(BF16)
    wh = w_router.astype(BF16)
    wl = (w_router - wh.astype(F32)).astype(BF16)
    logits = (jnp.dot(xh, wh, preferred_element_type=F32) + jnp.dot(xh, wl, preferred_element_type=F32)
              + jnp.dot(xl, wh, preferred_element_type=F32))
    lane_f = lax.broadcasted_iota(jnp.int32, logits.shape, 1).astype(F32)
    gl = jnp.where(lane_f < MOE_GROUPS, logits, NEG_INF)
    gmax = jnp.max(gl, axis=1, keepdims=True)
    gw = 1.0 / jnp.sum(jnp.exp(gl - gmax), axis=1, keepdims=True)
    gsel = _first_argmax(gl, gmax, lane_f)
    e0 = ROUTER_EXPERT_LANE0 + MOE_EXPERTS * gsel
    el = jnp.where((lane_f >= e0) & (lane_f < e0 + MOE_EXPERTS), logits, NEG_INF)
    v1 = jnp.max(el, axis=1, keepdims=True)
    i1 = _first_argmax(el, v1, lane_f)
    el2 = jnp.where(lane_f == i1, NEG_INF, el)
    v2 = jnp.max(el2, axis=1, keepdims=True)
    i2 = _first_argmax(el2, v2, lane_f)
    e2 = jnp.exp(v2 - v1)
    inv = gw / (1.0 + e2)
    comb = jnp.where(lane_f == i1, inv, 0.0) + jnp.where(lane_f == i2, e2 * inv, 0.0)
    return comb + jnp.where(lane_f == ROUTER_GROUP_LANE, gsel, 0.0)


MOE_TM = 512
GATHER_UNROLL = 8


def _issue_row_gather(idx_ref, src_hbm, dst_ref, sem, n_rows):
    def body(r, carry):
        pltpu.make_async_copy(src_hbm.at[pl.ds(idx_ref[0, r], 1), :], dst_ref.at[pl.ds(r, 1), :], sem).start()
        return carry
    lax.fori_loop(0, n_rows, body, 0, unroll=GATHER_UNROLL)


def _wait_row_gather(src_hbm, dst_ref, sem, n_rows):
    pltpu.make_async_copy(src_hbm.at[pl.ds(0, n_rows), :], dst_ref, sem).wait()


def _sort_by_group(gid, t, n_pad_rows):
    groups = jnp.arange(MOE_GROUPS, dtype=jnp.int32)
    onehot = (gid[:, None] == groups[None, :]).astype(jnp.int32)
    csum = jnp.cumsum(onehot, axis=0)
    counts = csum[-1]
    rank = jnp.sum(onehot * csum, axis=1) - 1
    tiles_g = (counts + MOE_TM - 1) // MOE_TM
    tile_end = jnp.cumsum(tiles_g)
    row_start = (tile_end - tiles_g) * MOE_TM
    pos = jnp.sum(onehot * row_start[None, :], axis=1) + rank
    src = jnp.zeros((n_pad_rows,), jnp.int32).at[pos].set(jnp.arange(t, dtype=jnp.int32))
    tile_ids = jnp.arange(n_pad_rows // MOE_TM, dtype=jnp.int32)
    tile_group = jnp.minimum(jnp.sum((tile_ids[:, None] >= tile_end[None, :]).astype(jnp.int32), axis=1),
                             MOE_GROUPS - 1)
    return pos, src, tile_group, tile_end[-1:]


def _moe_ffn_kernel(tg_ref, nv_ref, idx_cur_ref, idx_nxt_ref, xc_hbm, wg_ref, wu_ref, wd_ref, f_ref,
                    xbuf_ref, sem_ref):
    i = pl.program_id(0)
    slot = i % 2
    n_valid = nv_ref[0]
    d = f_ref.shape[1]

    @pl.when(i == 0)
    def _():
        _issue_row_gather(idx_cur_ref, xc_hbm, xbuf_ref.at[0], sem_ref.at[0], MOE_TM)

    @pl.when(i + 1 < n_valid)
    def _():
        _issue_row_gather(idx_nxt_ref, xc_hbm, xbuf_ref.at[1 - slot], sem_ref.at[1 - slot], MOE_TM)

    @pl.when(i < n_valid)
    def _():
        _wait_row_gather(xc_hbm, xbuf_ref.at[slot], sem_ref.at[slot], MOE_TM)
        xb = xbuf_ref[slot, :, 0:d].astype(BF16)
        comb = xbuf_ref[slot, :, d:d + LANES]
        lane = lax.broadcasted_iota(jnp.int32, comb.shape, 1)
        lane0 = ROUTER_EXPERT_LANE0 + MOE_EXPERTS * tg_ref[i]
        acc = None
        for e in range(MOE_EXPERTS):
            ce = jnp.sum(jnp.where(lane == lane0 + e, comb, 0.0), axis=1, keepdims=True)
            gate = jnp.dot(xb, wg_ref[0, e], preferred_element_type=F32)
            up = jnp.dot(xb, wu_ref[0, e], preferred_element_type=F32)
            hid = (gate * jax.nn.sigmoid(gate)) * up * ce
            part = jnp.dot(hid.astype(BF16), wd_ref[0, e], preferred_element_type=F32)
            acc = part if acc is None else acc + part
        f_ref[...] = acc

    @pl.when(i >= n_valid)
    def _():
        f_ref[...] = jnp.zeros(f_ref.shape, F32)


def _moe_ffn(xc, src, tile_group, n_valid, w_gate, w_up, w_down):
    d = xc.shape[1] - LANES
    n_tiles = tile_group.shape[0]
    _, ne, _, f = w_gate.shape
    idx = src.reshape(n_tiles, 1, MOE_TM)
    grid_spec = pltpu.PrefetchScalarGridSpec(
        num_scalar_prefetch=2,
        grid=(n_tiles,),
        in_specs=[
            pl.BlockSpec((None, 1, MOE_TM), lambda i, tg, nv: (i, 0, 0), memory_space=pltpu.SMEM),
            pl.BlockSpec((None, 1, MOE_TM), lambda i, tg, nv: (jnp.minimum(i + 1, n_tiles - 1), 0, 0),
                         memory_space=pltpu.SMEM),
            pl.BlockSpec(memory_space=pl.ANY),
            pl.BlockSpec((1, ne, d, f), lambda i, tg, nv: (tg[i], 0, 0, 0)),
            pl.BlockSpec((1, ne, d, f), lambda i, tg, nv: (tg[i], 0, 0, 0)),
            pl.BlockSpec((1, ne, f, d), lambda i, tg, nv: (tg[i], 0, 0, 0)),
        ],
        out_specs=pl.BlockSpec((MOE_TM, d), lambda i, tg, nv: (i, 0)),
        scratch_shapes=[pltpu.VMEM((2, MOE_TM, d + LANES), F32), pltpu.SemaphoreType.DMA((2,))],
    )
    return pl.pallas_call(
        _moe_ffn_kernel,
        grid_spec=grid_spec,
        out_shape=jax.ShapeDtypeStruct((n_tiles * MOE_TM, d), F32),
        compiler_params=_params("arbitrary"),
        name="moe_ffn",
    )(tile_group, n_valid, idx, idx, xc, w_gate, w_up, w_down)


LN2_TM = 512


LN2_CHUNK = 64


def _ln2_kernel(idx_cur_ref, idx_nxt_ref, f_hbm, x_ref, g_ref, b_ref, *rest, n_tiles, seg_ends):
    i = pl.program_id(0)
    if seg_ends is None:
        y_ref, yb_ref, fbuf_ref, sem_ref = rest
    else:
        seg_refs = rest[:len(seg_ends)]
        fbuf_ref, sem_ref, y_ref = rest[len(seg_ends):]
        yb_ref = None

    @pl.when(i == 0)
    def _():
        _issue_row_gather(idx_cur_ref, f_hbm, fbuf_ref.at[0], sem_ref.at[0], LN2_TM)

    def tile(cur, nxt):
        _wait_row_gather(f_hbm, fbuf_ref.at[cur], sem_ref.at[cur], LN2_TM)
        g = g_ref[...]
        b = b_ref[...]
        for c in range(LN2_TM // LN2_CHUNK):
            rows = pl.ds(c * LN2_CHUNK, LN2_CHUNK)
            for r in range(c * LN2_CHUNK, (c + 1) * LN2_CHUNK):
                pltpu.make_async_copy(f_hbm.at[pl.ds(idx_nxt_ref[0, r], 1), :],
                                      fbuf_ref.at[nxt, pl.ds(r, 1), :], sem_ref.at[nxt]).start()
            y = _layer_norm_rows(DEEPNORM_ALPHA * x_ref[rows, :] + fbuf_ref[cur, rows, :], g, b)
            y_ref[rows, :] = y
            if yb_ref is not None:
                yb_ref[rows, :] = y.astype(BF16)

    @pl.when(i % 2 == 0)
    def _():
        tile(0, 1)

    @pl.when(i % 2 == 1)
    def _():
        tile(1, 0)

    if seg_ends is not None:
        start = 0
        for ref, end in zip(seg_refs, seg_ends):
            @pl.when((i >= start) & (i < end))
            def _(ref=ref):
                ref[...] = y_ref[...]
            start = end

    @pl.when(i == n_tiles - 1)
    def _():
        spare = n_tiles % 2
        _wait_row_gather(f_hbm, fbuf_ref.at[spare], sem_ref.at[spare], LN2_TM)


def _ln2_gather(f_sorted, pos, x, g, b, seg_rows=None):
    t = x.shape[0]
    d = f_sorted.shape[1]
    n_tiles = t // LN2_TM
    idx = pos.reshape(n_tiles, 1, LN2_TM)
    row = pl.BlockSpec((LN2_TM, d), lambda i: (i, 0))
    vec = pl.BlockSpec((1, d), lambda i: (0, 0))
    scratch = [pltpu.VMEM((2, LN2_TM, d), F32), pltpu.SemaphoreType.DMA((2,))]
    if seg_rows is None:
        seg_ends = None
        out_specs = [row, row]
        out_shape = [jax.ShapeDtypeStruct((t, d), F32), jax.ShapeDtypeStruct((t, d), BF16)]
    else:
        seg_tiles = [r // LN2_TM for r in seg_rows]
        seg_ends = tuple(int(v) for v in np.cumsum(seg_tiles))

        def seg_spec(s):
            first, count = seg_ends[s] - seg_tiles[s], seg_tiles[s]
            return pl.BlockSpec((LN2_TM, d), lambda i: (jnp.clip(i - first, 0, count - 1), 0))

        out_specs = [seg_spec(s) for s in range(len(seg_rows))]
        out_shape = [jax.ShapeDtypeStruct((r, d), F32) for r in seg_rows]
        scratch.append(pltpu.VMEM((LN2_TM, d), F32))
    return pl.pallas_call(
        functools.partial(_ln2_kernel, n_tiles=n_tiles, seg_ends=seg_ends),
        grid=(n_tiles,),
        in_specs=[
            pl.BlockSpec((None, 1, LN2_TM), lambda i: (i, 0, 0), memory_space=pltpu.SMEM),
            pl.BlockSpec((None, 1, LN2_TM), lambda i: (jnp.minimum(i + 1, n_tiles - 1), 0, 0),
                         memory_space=pltpu.SMEM),
            pl.BlockSpec(memory_space=pl.ANY),
            row, vec, vec,
        ],
        out_specs=out_specs,
        out_shape=out_shape,
        scratch_shapes=scratch,
        compiler_params=_params("arbitrary"),
        name="ln2_gather",
    )(idx, idx, f_sorted, x, g, b)


def _da_weight_layout(w_in):
    d = w_in.shape[0]
    w = DA_QK_WIDTH
    q1, q2, k1, k2 = (w_in[:, i * w:(i + 1) * w].reshape(d, DA_HEADS, DA_HEAD_DIM) for i in range(4))
    qk = jnp.concatenate([jnp.concatenate([q1, q2], axis=2).reshape(d, 2 * w),
                          jnp.concatenate([k1, k2], axis=2).reshape(d, 2 * w)], axis=1)
    return qk, w_in[:, 4 * w:]


def _trunk(xs, da_w_in, da_w_out, da_lambda_q1, da_lambda_k1, da_lambda_q2, da_lambda_k2, da_subln_g,
           dl_w_in, dl_w_out, ln1_g, ln1_b, ln2_g, ln2_b,
           moe_router_group, moe_router_expert, moe_w_gate, moe_w_up, moe_w_down):
    d = xs[0].shape[-1]
    segs = []
    off = 0
    for x in xs:
        segs.append((off, x.shape[0], x.shape[1]))
        off += x.shape[0] * x.shape[1]
    t_total = off
    x = jnp.concatenate([v.reshape(-1, d) for v in xs], axis=0)
    xb = x.astype(BF16)

    max_len = max(s for _, _, s in segs)
    rope = _rope_tables(max_len)
    pos_blocks = np.concatenate([
        np.tile(np.arange(s // PROJ_TM, dtype=np.int32), b) for _, b, s in segs])
    pos_blocks = jnp.asarray(pos_blocks)

    for i in range(DEPTH):
        j = i // N_MIXERS
        if i % N_MIXERS == 0:
            lambda_init = 0.8 - 0.6 * math.exp(-0.3 * i)
            w_qk, w_v = _da_weight_layout(da_w_in[j])
            qk = _proj(xb, w_qk.astype(BF16), rope, pos_blocks,
                       q_cols=2 * DA_QK_WIDTH, rope_cols=4 * DA_QK_WIDTH, q_scale=QK_SCALE_LOG2)
            vt = _proj_t(xb, w_v.T.astype(BF16))
            lam_params = jnp.stack([da_lambda_q1[j], da_lambda_k1[j], da_lambda_q2[j], da_lambda_k2[j]]).astype(F32)
            g_col = da_subln_g[j].astype(F32).reshape(DA_V_DIM, 1)
            os_ = [[_da_attention(qk, vt, lam_params, g_col, lambda_init, ro, b, s) for ro, b, s in segs]]
            lses = []
            w_out = da_w_out[j]
        else:
            os_, lses = [], []
            gw = 3 * DIL_GROUP_WIDTH
            for g, (window, dilation) in enumerate(DIL_PATTERNS):
                assert window // (2 * dilation) == DIL_HALF
                proj = _proj(xb, dl_w_in[j][:, g * gw:(g + 1) * gw].astype(BF16), rope, pos_blocks,
                             q_cols=DIL_GROUP_WIDTH, rope_cols=2 * DIL_GROUP_WIDTH, q_scale=QK_SCALE_LOG2,
                             dilation=dilation)
                pairs = [_dil_attention_group(proj, g, dilation, ro, b, s) for ro, b, s in segs]
                os_.append([p[0] for p in pairs])
                lses.append([p[1] for p in pairs])
            w_out = dl_w_out[j]
        w_router = jnp.concatenate([
            moe_router_group[i].astype(F32),
            moe_router_expert[i].reshape(d, N_EXPERTS).astype(F32),
            jnp.zeros((d, LANES - MOE_GROUPS - N_EXPERTS), F32)], axis=1)
        xc = _post_attention(os_, lses, [b * s for _, b, s in segs], w_out.astype(BF16), x,
                             ln1_g[i].reshape(1, d).astype(F32), ln1_b[i].reshape(1, d).astype(F32), w_router)
        gid = xc[:, d + ROUTER_GROUP_LANE].astype(jnp.int32)
        pos, src, tile_group, n_valid = _sort_by_group(gid, t_total, t_total + MOE_GROUPS * MOE_TM)
        f_sorted = _moe_ffn(xc, src, tile_group, n_valid,
                            moe_w_gate[i].astype(BF16), moe_w_up[i].astype(BF16), moe_w_down[i].astype(BF16))
        ln2 = (ln2_g[i].reshape(1, d).astype(F32), ln2_b[i].reshape(1, d).astype(F32))
        if i + 1 < DEPTH:
            x, xb = _ln2_gather(f_sorted, pos, xc, *ln2)
        else:
            ys = _ln2_gather(f_sorted, pos, xc, *ln2, seg_rows=[b * s for _, b, s in segs])
    return [y.reshape(v.shape) for y, v in zip(ys, xs)]


def kernel(x_prompt, x_sample, da_w_in, da_w_out, da_lambda_q1, da_lambda_k1, da_lambda_q2, da_lambda_k2, da_subln_g, dl_w_in, dl_w_out, ln1_g, ln1_b, ln2_g, ln2_b, moe_router_group, moe_router_expert, moe_w_gate, moe_w_up, moe_w_down):
    y_prompt, y_sample = _trunk(
        [x_prompt, x_sample], da_w_in, da_w_out, da_lambda_q1, da_lambda_k1, da_lambda_q2, da_lambda_k2,
        da_subln_g, dl_w_in, dl_w_out, ln1_g, ln1_b, ln2_g, ln2_b,
        moe_router_group, moe_router_expert, moe_w_gate, moe_w_up, moe_w_down)
    return (y_prompt, y_sample)
```

```python
import functools
import math

import jax
import jax.numpy as jnp
import numpy as np
from jax import lax
from jax.experimental import pallas as pl
from jax.experimental.pallas import tpu as pltpu

F32 = jnp.float32
BF16 = jnp.bfloat16

LANES = 128
VMEM_LIMIT_BYTES = 56 * 1024 * 1024

D_MODEL = 1024
DEPTH = 4
N_MIXERS = 2

DA_HEADS = 8
DA_HEAD_DIM = 64
DA_V_DIM = 2 * DA_HEAD_DIM
DA_QK_WIDTH = DA_HEADS * DA_HEAD_DIM

DIL_PATTERNS = ((128, 1), (512, 4), (2048, 16))
DIL_GROUPS = len(DIL_PATTERNS)
DIL_HEADS = 16
DIL_HEAD_DIM = 64
DIL_GROUP_WIDTH = DIL_HEADS * DIL_HEAD_DIM
DIL_HALF = 64

ROPE_THETA = 500000.0
ROPE_ROT = DA_HEAD_DIM // 4
ROPE_HALF = ROPE_ROT // 2

LN_EPS = 1e-5
DEEPNORM_ALPHA = (2.0 * DEPTH) ** 0.25

MOE_GROUPS = 4
MOE_EXPERTS = 4
MOE_FF = 512
N_EXPERTS = MOE_GROUPS * MOE_EXPERTS

NEG_INF = -1e30
LOG2_E = math.log2(math.e)
QK_SCALE_LOG2 = DA_HEAD_DIM ** -0.5 * LOG2_E


def _params(*sem):
    return pltpu.CompilerParams(dimension_semantics=sem, vmem_limit_bytes=VMEM_LIMIT_BYTES)


PROJ_TM = 1024
PROJ_TN = 1024


def _rope_tables(max_len):
    inv = ROPE_THETA ** (-jnp.arange(ROPE_HALF, dtype=F32) * (2.0 / ROPE_ROT))
    ang = jnp.arange(max_len, dtype=F32)[:, None] * inv[None, :]
    cos, sin = jnp.cos(ang), jnp.sin(ang)
    ones = jnp.ones((max_len, DA_HEAD_DIM - ROPE_ROT), F32)
    zeros_h = jnp.zeros((max_len, ROPE_HALF), F32)
    zeros_r = jnp.zeros((max_len, DA_HEAD_DIM - ROPE_ROT), F32)
    c = jnp.concatenate([cos, cos, ones], axis=1)
    s_up = jnp.concatenate([zeros_h, sin, zeros_r], axis=1)
    s_dn = jnp.concatenate([-sin, zeros_h, zeros_r], axis=1)
    reps = LANES // DA_HEAD_DIM
    return jnp.stack([jnp.tile(c, (1, reps)), jnp.tile(s_up, (1, reps)), jnp.tile(s_dn, (1, reps))])


PROJ_MAX_STRIDE = 4
PROJ_CHUNK = 256


def _proj_kernel(pos_ref, x_ref, w_ref, rope_ref, o_ref, ybuf_ref, zbuf_ref, *, n_cols, q_cols, rope_cols, q_scale, dilation):
    del pos_ref
    x = x_ref[...]
    c = rope_ref[0]
    s_up = rope_ref[1]
    s_dn = rope_ref[2]
    rows = x.shape[0] // dilation
    for k in range(n_cols // PROJ_CHUNK):
        c0 = k * PROJ_CHUNK
        y = jnp.dot(x, w_ref[:, c0:c0 + PROJ_CHUNK], preferred_element_type=F32)
        if c0 < q_cols:
            y = y * q_scale
        parts = []
        for h in range(PROJ_CHUNK // LANES):
            yc = y[:, h * LANES:(h + 1) * LANES]
            if c0 < rope_cols:
                yc = yc * c + pltpu.roll(yc, ROPE_HALF, 1) * s_up + pltpu.roll(yc, LANES - ROPE_HALF, 1) * s_dn
            parts.append(yc)
        if dilation == 1:
            for h, yc in enumerate(parts):
                o_ref[0, :, c0 + h * LANES:c0 + (h + 1) * LANES] = yc.astype(o_ref.dtype)
        else:
            for h, yc in enumerate(parts):
                slot = (2 * k + h) % ybuf_ref.shape[0]
                cols = slice(c0 + h * LANES, c0 + (h + 1) * LANES)
                ybuf_ref[slot] = yc
                if dilation <= PROJ_MAX_STRIDE:
                    for r in range(dilation):
                        o_ref[r, :, cols] = ybuf_ref[slot, pl.ds(r, rows, stride=dilation), :].astype(o_ref.dtype)
                else:
                    outer = dilation // PROJ_MAX_STRIDE
                    zslot = h % zbuf_ref.shape[0]
                    for r in range(PROJ_MAX_STRIDE):
                        zbuf_ref[zslot, r] = ybuf_ref[slot, pl.ds(r, rows * outer, stride=PROJ_MAX_STRIDE), :]
                        for a in range(outer):
                            o_ref[r + PROJ_MAX_STRIDE * a, :, cols] = (
                                zbuf_ref[zslot, r, pl.ds(a, rows, stride=outer), :].astype(o_ref.dtype))


def _proj(xb, w, rope, pos_blocks, *, q_cols, rope_cols, q_scale, dilation=1):
    t, d = xb.shape
    n = w.shape[1]
    rows = PROJ_TM // dilation
    grid_spec = pltpu.PrefetchScalarGridSpec(
        num_scalar_prefetch=1,
        grid=(t // PROJ_TM,),
        in_specs=[
            pl.BlockSpec((PROJ_TM, d), lambda i, pos: (i, 0)),
            pl.BlockSpec((d, n), lambda i, pos: (0, 0)),
            pl.BlockSpec((3, PROJ_TM, LANES), lambda i, pos: (0, pos[i], 0)),
        ],
        out_specs=pl.BlockSpec((dilation, rows, n), lambda i, pos: (0, i, 0)),
        scratch_shapes=[pltpu.VMEM((4, PROJ_TM, LANES), F32),
                        pltpu.VMEM((2, PROJ_MAX_STRIDE, PROJ_TM // PROJ_MAX_STRIDE, LANES), F32)],
    )
    return pl.pallas_call(
        functools.partial(_proj_kernel, n_cols=n, q_cols=q_cols, rope_cols=rope_cols, q_scale=q_scale,
                          dilation=dilation),
        grid_spec=grid_spec,
        out_shape=jax.ShapeDtypeStruct((dilation, t // dilation, n), BF16),
        compiler_params=_params("parallel"),
        name="proj_rope",
    )(pos_blocks, xb, w, rope)


DA_TK = 512
DA_ONES_ROWS = 16
DA_VT_ROWS = DA_V_DIM + DA_ONES_ROWS


def _projt_kernel(x_ref, wt_ref, o_ref):
    y = lax.dot_general(wt_ref[...], x_ref[...], (((1,), (1,)), ((), ())), preferred_element_type=F32)
    h = o_ref.shape[0]
    o_ref[:, 0, 0:DA_V_DIM, :] = y.reshape(h, DA_V_DIM, DA_TK).astype(o_ref.dtype)
    o_ref[:, 0, DA_V_DIM:DA_VT_ROWS, :] = jnp.ones((h, DA_ONES_ROWS, DA_TK), o_ref.dtype)


def _proj_t(xb, wt):
    t, d = xb.shape
    n = wt.shape[0]
    h = n // DA_V_DIM
    return pl.pallas_call(
        _projt_kernel,
        grid=(t // DA_TK,),
        in_specs=[
            pl.BlockSpec((DA_TK, d), lambda i: (i, 0)),
            pl.BlockSpec((n, d), lambda i: (0, 0)),
        ],
        out_specs=pl.BlockSpec((h, 1, DA_VT_ROWS, DA_TK), lambda i: (0, i, 0, 0)),
        out_shape=jax.ShapeDtypeStruct((h, t // DA_TK, DA_VT_ROWS, DA_TK), BF16),
        compiler_params=_params("parallel"),
        name="proj_vt",
    )(xb, wt)


DA_TQ = 512
DA_PAIR_UNROLL = 3


def _da_attn_kernel(lam_ref, g_ref, q_ref, k_ref, vt_ref, o_ref, q1_ref, q2_ref,
                    sa1_ref, sa2_ref, sb1_ref, sb2_ref, xa1_ref, xa2_ref, xb1_ref, xb2_ref,
                    m1_ref, a1_ref, m2_ref, a2_ref, *, n_kblk, lambda_init):
    buf_a = ((sa1_ref, xa1_ref), (sa2_ref, xa2_ref))
    buf_b = ((sb1_ref, xb1_ref), (sb2_ref, xb2_ref))
    qt = q_ref[...].astype(F32).T
    row = lax.broadcasted_iota(jnp.int32, qt.shape, 0)
    q1_ref[...] = jnp.where(row < DA_HEAD_DIM, qt, 0.0).astype(q1_ref.dtype)
    q2_ref[...] = jnp.where(row >= DA_HEAD_DIM, qt, 0.0).astype(q2_ref.dtype)
    for m_ref, a_ref in ((m1_ref, a1_ref), (m2_ref, a2_ref)):
        m_ref[...] = jnp.full(m_ref.shape, NEG_INF, F32)
        a_ref[...] = jnp.zeros(a_ref.shape, F32)

    def scores(j, buf):
        k = k_ref[pl.ds(pl.multiple_of(j * DA_TK, DA_TK), DA_TK), :]
        for qm_ref, (s_ref, x_ref) in zip((q1_ref, q2_ref), buf):
            s = jnp.dot(k, qm_ref[...], preferred_element_type=F32)
            s_ref[...] = s
            x_ref[...] = jnp.max(s.reshape(DA_TK // 8, 8, DA_TQ), axis=0)

    def softmax_pv(j, buf):
        vt = vt_ref[0, j]
        for (s_ref, x_ref), m_ref, a_ref in zip(buf, (m1_ref, m2_ref), (a1_ref, a2_ref)):
            m_old = m_ref[...]
            m_new = jnp.maximum(m_old, jnp.max(x_ref[...], axis=0, keepdims=True))
            p = jnp.exp2(s_ref[...] - m_new).astype(vt.dtype)
            a_ref[...] = jnp.exp2(m_old - m_new) * a_ref[...] + jnp.dot(vt, p, preferred_element_type=F32)
            m_ref[...] = m_new

    scores(0, buf_a)

    def pair(jj, carry):
        j = 2 * jj
        scores(j + 1, buf_b)
        softmax_pv(j, buf_a)
        scores(j + 2, buf_a)
        softmax_pv(j + 1, buf_b)
        return carry

    n_pairs = n_kblk // 2 - 1
    lax.fori_loop(0, n_pairs, pair, 0, unroll=DA_PAIR_UNROLL if n_pairs > DA_PAIR_UNROLL else 2)
    scores(n_kblk - 1, buf_b)
    softmax_pv(n_kblk - 2, buf_a)
    softmax_pv(n_kblk - 1, buf_b)

    lp = lam_ref[...]
    lam = (jnp.exp(jnp.sum(lp[0:1] * lp[1:2], axis=1, keepdims=True))
           - jnp.exp(jnp.sum(lp[2:3] * lp[3:4], axis=1, keepdims=True)) + lambda_init)
    a1 = a1_ref[...]
    a2 = a2_ref[...]
    o = (a1[0:DA_V_DIM] * (1.0 / a1[DA_V_DIM:DA_V_DIM + 1])
         - lam * (a2[0:DA_V_DIM] * (1.0 / a2[DA_V_DIM:DA_V_DIM + 1])))
    o = o * lax.rsqrt(jnp.mean(jnp.square(o), axis=0, keepdims=True) + LN_EPS)
    o = o * g_ref[...] * (1.0 - lambda_init)
    o_ref[...] = o.T.astype(o_ref.dtype)


def _da_attention(qk, vt, lam_params, subln_g, lambda_init, row_off, batch, seq):
    h = DA_HEADS
    nq = seq // DA_TQ
    nk = seq // DA_TK
    assert nk % 2 == 0 and nk >= 2
    qoff = row_off // DA_TQ
    soff = row_off // seq
    s_buf = pltpu.VMEM((DA_TK, DA_TQ), F32)
    x_buf = pltpu.VMEM((8, DA_TQ), F32)
    in_specs = [
        pl.BlockSpec((4, DA_HEAD_DIM), lambda b, hh, i: (0, 0)),
        pl.BlockSpec((DA_V_DIM, 1), lambda b, hh, i: (0, 0)),
        pl.BlockSpec((None, DA_TQ, LANES), lambda b, hh, i: (0, qoff + b * nq + i, hh)),
        pl.BlockSpec((None, seq, LANES), lambda b, hh, i: (0, soff + b, h + hh)),
        pl.BlockSpec((1, nk, DA_VT_ROWS, DA_TK), lambda b, hh, i: (hh, soff + b, 0, 0)),
    ]
    return pl.pallas_call(
        functools.partial(_da_attn_kernel, n_kblk=nk, lambda_init=lambda_init),
        grid=(batch, h, nq),
        in_specs=in_specs,
        out_specs=pl.BlockSpec((None, DA_TQ, DA_V_DIM), lambda b, hh, i: (0, b * nq + i, hh)),
        out_shape=jax.ShapeDtypeStruct((1, batch * seq, h * DA_V_DIM), BF16),
        scratch_shapes=[
            pltpu.VMEM((LANES, DA_TQ), BF16), pltpu.VMEM((LANES, DA_TQ), BF16),
            s_buf, s_buf, s_buf, s_buf,
            x_buf, x_buf, x_buf, x_buf,
            pltpu.VMEM((1, DA_TQ), F32), pltpu.VMEM((DA_VT_ROWS, DA_TQ), F32),
            pltpu.VMEM((1, DA_TQ), F32), pltpu.VMEM((DA_VT_ROWS, DA_TQ), F32),
        ],
        compiler_params=_params("parallel", "parallel", "arbitrary"),
        name="da_attn",
    )(lam_params, subln_g, qk, qk, vt)


DIL_QSUB = 128
DIL_KWIN = DIL_QSUB + 2 * DIL_HALF
DIL_QBLK = 1024


def _dil_attn_kernel(q_ref, k_ref, v_ref, o_ref, lse_ref, *, sub_len, qblk):
    qi = pl.program_id(3)
    lane = lax.broadcasted_iota(jnp.int32, (DIL_QSUB, LANES), 1)
    lo = lane < DIL_HEAD_DIM
    row = lax.broadcasted_iota(jnp.int32, (DIL_QSUB, DIL_KWIN), 0)
    col = lax.broadcasted_iota(jnp.int32, (DIL_QSUB, DIL_KWIN), 1)

    interior = jnp.abs(col - DIL_HALF - row) <= DIL_HALF
    n_sub = qblk // DIL_QSUB

    for n in range(n_sub):
        q0 = n * DIL_QSUB
        gq0 = qi * qblk + q0
        ws = jnp.clip(gq0 - DIL_HALF, 0, sub_len - DIL_KWIN)
        ws = pl.multiple_of(ws, DIL_HALF)
        qb = q_ref[pl.ds(q0, DIL_QSUB), :]
        kw = k_ref[pl.ds(ws, DIL_KWIN), :]
        vw = v_ref[pl.ds(ws, DIL_KWIN), :]
        if 0 < n < n_sub - 1:
            valid = interior
        else:
            valid = jnp.abs((ws + col) - (gq0 + row)) <= DIL_HALF
        zero = jnp.zeros_like(qb)
        outs, lses = [], []
        for qm in (jnp.where(lo, qb, zero), jnp.where(lo, zero, qb)):
            s = lax.dot_general(qm, kw, (((1,), (1,)), ((), ())), preferred_element_type=F32)
            s = jnp.where(valid, s, NEG_INF)
            m = jnp.max(s, axis=1, keepdims=True)
            p = jnp.exp2(s - m)
            l = jnp.sum(p, axis=1, keepdims=True)
            o = jnp.dot(p.astype(vw.dtype), vw, preferred_element_type=F32) * (1.0 / l)
            outs.append(o)
            lses.append(jnp.broadcast_to(m + jnp.log(l) * LOG2_E, (DIL_QSUB, LANES)))
        o_ref[pl.ds(q0, DIL_QSUB), :] = jnp.where(lo, outs[0], outs[1]).astype(o_ref.dtype)
        lse_ref[pl.ds(q0, DIL_QSUB), :] = jnp.where(lo, lses[0], lses[1])


def _dil_attention_group(proj, g, dilation, row_off, batch, seq):
    sub_len = seq // dilation
    qblk = min(DIL_QBLK, sub_len)
    nqb = sub_len // qblk
    hp = DIL_GROUP_WIDTH // LANES
    soff = row_off // seq

    def qmap(b, r, p, i):
        return (r, (soff + b) * nqb + i, p)

    def kmap(b, r, p, i):
        return (r, soff + b, hp + p)

    def vmap(b, r, p, i):
        return (r, soff + b, 2 * hp + p)

    def omap(b, r, p, i):
        return (r, b * nqb + i, p)

    return pl.pallas_call(
        functools.partial(_dil_attn_kernel, sub_len=sub_len, qblk=qblk),
        grid=(batch, dilation, hp, nqb),
        in_specs=[
            pl.BlockSpec((None, qblk, LANES), qmap),
            pl.BlockSpec((None, sub_len, LANES), kmap),
            pl.BlockSpec((None, sub_len, LANES), vmap),
        ],
        out_specs=[
            pl.BlockSpec((None, qblk, LANES), omap),
            pl.BlockSpec((None, qblk, LANES), omap),
        ],
        out_shape=[
            jax.ShapeDtypeStruct((dilation, batch * sub_len, DIL_GROUP_WIDTH), BF16),
            jax.ShapeDtypeStruct((dilation, batch * sub_len, DIL_GROUP_WIDTH), F32),
        ],
        compiler_params=_params("parallel", "parallel", "parallel", "arbitrary"),
        name=f"dil_attn_g{g}",
    )(proj, proj, proj)


POST_TM = 256


def _layer_norm_rows(z, g, b):
    mu = jnp.mean(z, axis=-1, keepdims=True)
    zc = z - mu
    var = jnp.mean(jnp.square(zc), axis=-1, keepdims=True)
    return zc * lax.rsqrt(var + LN_EPS) * g + b


def _token_order_reader(seg_refs, seg_ends, scratch_ref):
    i = pl.program_id(0)
    dil, rows, width = seg_refs[0].shape
    if dil == 1:
        def read(c):
            cs = slice(c * LANES, (c + 1) * LANES)
            val = seg_refs[-1][0, :, cs]
            for ref, end in reversed(list(zip(seg_refs[:-1], seg_ends[:-1]))):
                val = jnp.where(i < end, ref[0, :, cs], val)
            return val.astype(F32)
        return read
    start = 0
    for ref, end in zip(seg_refs, seg_ends):
        @pl.when((i >= start) & (i < end))
        def _(ref=ref):
            for r in range(dil):
                for c in range(width // LANES):
                    scratch_ref[c, pl.ds(r, rows, stride=dil), :] = (
                        ref[r, :, c * LANES:(c + 1) * LANES].astype(F32))
        start = end
    return lambda c: scratch_ref[c]


def _post_kernel(*refs, n_groups, seg_ends):
    n_seg = len(seg_ends)
    o_refs = [refs[g * n_seg:(g + 1) * n_seg] for g in range(n_groups)]
    n_in = n_groups * n_seg
    lse_refs = []
    if n_groups > 1:
        lse_refs = [refs[n_in + g * n_seg:n_in + (g + 1) * n_seg] for g in range(n_groups)]
        n_in *= 2
    w_ref, x_ref, g_ref, b_ref, wr_ref, xc_ref = refs[n_in:n_in + 6]
    scratch = refs[n_in + 6:]
    d = x_ref.shape[1]
    o_scr = scratch[-1]
    spare = iter(scratch[:-1])
    readers = [_token_order_reader(r, seg_ends, next(spare) if r[0].shape[0] > 1 else None)
               for r in (*o_refs, *lse_refs)]
    read_o, read_lse = readers[:n_groups], readers[n_groups:]
    for c in range(d // LANES):
        if n_groups == 1:
            o_c = read_o[0](c)
        else:
            lses = [rd(c) for rd in read_lse]
            m = functools.reduce(jnp.maximum, lses)
            es = [jnp.exp2(l - m) for l in lses]
            inv = 1.0 / functools.reduce(jnp.add, es)
            o_c = functools.reduce(jnp.add, [(e * inv) * rd(c) for e, rd in zip(es, read_o)])
        o_scr[:, c * LANES:(c + 1) * LANES] = o_c.astype(BF16)
    h = jnp.dot(o_scr[...], w_ref[...], preferred_element_type=F32)
    y = _layer_norm_rows(DEEPNORM_ALPHA * x_ref[...] + h, g_ref[...], b_ref[...])
    xc_ref[:, 0:d] = y
    xc_ref[:, d:d + LANES] = _routing_lanes(y, wr_ref[...])


def _post_attention(os_, lses, seg_rows, w_out, x, g, b, w_router):
    t, d = x.shape
    n_groups = len(os_)
    seg_tiles = [r // POST_TM for r in seg_rows]
    seg_ends = tuple(int(v) for v in np.cumsum(seg_tiles))
    row = pl.BlockSpec((POST_TM, d), lambda i: (i, 0))
    full = pl.BlockSpec((d, d), lambda i: (0, 0))
    vec = pl.BlockSpec((1, d), lambda i: (0, 0))

    def seg_spec(a, s):
        first, count = seg_ends[s] - seg_tiles[s], seg_tiles[s]
        return pl.BlockSpec((a.shape[0], POST_TM // a.shape[0], d),
                            lambda i: (0, jnp.clip(i - first, 0, count - 1), 0))

    arrays = [a for grp in (*os_, *lses) for a in grp]
    specs = [seg_spec(a, s) for grp in (*os_, *lses) for s, a in enumerate(grp)]
    n_scratch = sum(1 for grp in (*os_, *lses) if grp[0].shape[0] > 1)
    return pl.pallas_call(
        functools.partial(_post_kernel, n_groups=n_groups, seg_ends=seg_ends),
        grid=(t // POST_TM,),
        in_specs=specs + [full, row, vec, vec, pl.BlockSpec((d, LANES), lambda i: (0, 0))],
        out_specs=pl.BlockSpec((POST_TM, d + LANES), lambda i: (i, 0)),
        out_shape=jax.ShapeDtypeStruct((t, d + LANES), F32),
        scratch_shapes=[pltpu.VMEM((d // LANES, POST_TM, LANES), F32)] * n_scratch + [pltpu.VMEM((POST_TM, d), BF16)],
        compiler_params=_params("parallel"),
        name="post_attn_ln",
    )(*arrays, w_out, x, g, b, w_router)


ROUTER_TM = 512
ROUTER_EXPERT_LANE0 = MOE_GROUPS
ROUTER_GROUP_LANE = 0


def _first_argmax(vals, vmax, lane_f):
    return jnp.min(jnp.where(vals == vmax, lane_f, float(LANES)), axis=1, keepdims=True)


def _routing_lanes(x, w_router):
    xh = x.astype(BF16)
    xl = (x - xh.astype(F32)).astype(BF16)
    wh = w_router.astype(BF16)
    wl = (w_router - wh.astype(F32)).astype(BF16)
    logits = (jnp.dot(xh, wh, preferred_element_type=F32) + jnp.dot(xh, wl, preferred_element_type=F32)
              + jnp.dot(xl, wh, preferred_element_type=F32))
    lane_f = lax.broadcasted_iota(jnp.int32, logits.shape, 1).astype(F32)
    gl = jnp.where(lane_f < MOE_GROUPS, logits, NEG_INF)
    gmax = jnp.max(gl, axis=1, keepdims=True)
    gw = 1.0 / jnp.sum(jnp.exp(gl - gmax), axis=1, keepdims=True)
    gsel = _first_argmax(gl, gmax, lane_f)
    e0 = ROUTER_EXPERT_LANE0 + MOE_EXPERTS * gsel
    el = jnp.where((lane_f >= e0) & (lane_f < e0 + MOE_EXPERTS), logits, NEG_INF)
    v1 = jnp.max(el, axis=1, keepdims=True)
    i1 = _first_argmax(el, v1, lane_f)
    el2 = jnp.where(lane_f == i1, NEG_INF, el)
    v2 = jnp.max(el2, axis=1, keepdims=True)
    i2 = _first_argmax(el2, v2, lane_f)
    e2 = jnp.exp(v2 - v1)
    inv = gw / (1.0 + e2)
    comb = jnp.where(lane_f == i1, inv, 0.0) + jnp.where(lane_f == i2, e2 * inv, 0.0)
    return comb + jnp.where(lane_f == ROUTER_GROUP_LANE, gsel, 0.0)


MOE_TM = 512
GATHER_UNROLL = 8


def _issue_row_gather(idx_ref, src_hbm, dst_ref, sem, n_rows):
    def body(r, carry):
        pltpu.make_async_copy(src_hbm.at[pl.ds(idx_ref[0, r], 1), :], dst_ref.at[pl.ds(r, 1), :], sem).start()
        return carry
    lax.fori_loop(0, n_rows, body, 0, unroll=GATHER_UNROLL)


def _wait_row_gather(src_hbm, dst_ref, sem, n_rows):
    pltpu.make_async_copy(src_hbm.at[pl.ds(0, n_rows), :], dst_ref, sem).wait()


def _sort_by_group(gid, t, n_pad_rows):
    groups = jnp.arange(MOE_GROUPS, dtype=jnp.int32)
    onehot = (gid[:, None] == groups[None, :]).astype(jnp.int32)
    csum = jnp.cumsum(onehot, axis=0)
    counts = csum[-1]
    rank = jnp.sum(onehot * csum, axis=1) - 1
    tiles_g = (counts + MOE_TM - 1) // MOE_TM
    tile_end = jnp.cumsum(tiles_g)
    row_start = (tile_end - tiles_g) * MOE_TM
    pos = jnp.sum(onehot * row_start[None, :], axis=1) + rank
    src = jnp.zeros((n_pad_rows,), jnp.int32).at[pos].set(jnp.arange(t, dtype=jnp.int32))
    tile_ids = jnp.arange(n_pad_rows // MOE_TM, dtype=jnp.int32)
    tile_group = jnp.minimum(jnp.sum((tile_ids[:, None] >= tile_end[None, :]).astype(jnp.int32), axis=1),
                             MOE_GROUPS - 1)
    return pos, src, tile_group, tile_end[-1:]


def _moe_ffn_kernel(tg_ref, nv_ref, idx_cur_ref, idx_nxt_ref, xc_hbm, wg_ref, wu_ref, wd_ref, f_ref,
                    xbuf_ref, sem_ref):
    i = pl.program_id(0)
    slot = i % 2
    n_valid = nv_ref[0]
    d = f_ref.shape[1]

    @pl.when(i == 0)
    def _():
        _issue_row_gather(idx_cur_ref, xc_hbm, xbuf_ref.at[0], sem_ref.at[0], MOE_TM)

    @pl.when(i + 1 < n_valid)
    def _():
        _issue_row_gather(idx_nxt_ref, xc_hbm, xbuf_ref.at[1 - slot], sem_ref.at[1 - slot], MOE_TM)

    @pl.when(i < n_valid)
    def _():
        _wait_row_gather(xc_hbm, xbuf_ref.at[slot], sem_ref.at[slot], MOE_TM)
        xb = xbuf_ref[slot, :, 0:d].astype(BF16)
        comb = xbuf_ref[slot, :, d:d + LANES]
        lane = lax.broadcasted_iota(jnp.int32, comb.shape, 1)
        lane0 = ROUTER_EXPERT_LANE0 + MOE_EXPERTS * tg_ref[i]
        acc = None
        for e in range(MOE_EXPERTS):
            ce = jnp.sum(jnp.where(lane == lane0 + e, comb, 0.0), axis=1, keepdims=True)
            gate = jnp.dot(xb, wg_ref[0, e], preferred_element_type=F32)
            up = jnp.dot(xb, wu_ref[0, e], preferred_element_type=F32)
            hid = (gate * jax.nn.sigmoid(gate)) * up * ce
            part = jnp.dot(hid.astype(BF16), wd_ref[0, e], preferred_element_type=F32)
            acc = part if acc is None else acc + part
        f_ref[...] = acc

    @pl.when(i >= n_valid)
    def _():
        f_ref[...] = jnp.zeros(f_ref.shape, F32)


def _moe_ffn(xc, src, tile_group, n_valid, w_gate, w_up, w_down):
    d = xc.shape[1] - LANES
    n_tiles = tile_group.shape[0]
    _, ne, _, f = w_gate.shape
    idx = src.reshape(n_tiles, 1, MOE_TM)
    grid_spec = pltpu.PrefetchScalarGridSpec(
        num_scalar_prefetch=2,
        grid=(n_tiles,),
        in_specs=[
            pl.BlockSpec((None, 1, MOE_TM), lambda i, tg, nv: (i, 0, 0), memory_space=pltpu.SMEM),
            pl.BlockSpec((None, 1, MOE_TM), lambda i, tg, nv: (jnp.minimum(i + 1, n_tiles - 1), 0, 0),
                         memory_space=pltpu.SMEM),
            pl.BlockSpec(memory_space=pl.ANY),
            pl.BlockSpec((1, ne, d, f), lambda i, tg, nv: (tg[i], 0, 0, 0)),
            pl.BlockSpec((1, ne, d, f), lambda i, tg, nv: (tg[i], 0, 0, 0)),
            pl.BlockSpec((1, ne, f, d), lambda i, tg, nv: (tg[i], 0, 0, 0)),
        ],
        out_specs=pl.BlockSpec((MOE_TM, d), lambda i, tg, nv: (i, 0)),
        scratch_shapes=[pltpu.VMEM((2, MOE_TM, d + LANES), F32), pltpu.SemaphoreType.DMA((2,))],
    )
    return pl.pallas_call(
        _moe_ffn_kernel,
        grid_spec=grid_spec,
        out_shape=jax.ShapeDtypeStruct((n_tiles * MOE_TM, d), F32),
        compiler_params=_params("arbitrary"),
        name="moe_ffn",
    )(tile_group, n_valid, idx, idx, xc, w_gate, w_up, w_down)


LN2_TM = 512


LN2_CHUNK = 64


def _ln2_kernel(idx_cur_ref, idx_nxt_ref, f_hbm, x_ref, g_ref, b_ref, *rest, n_tiles, seg_ends):
    i = pl.program_id(0)
    if seg_ends is None:
        y_ref, yb_ref, fbuf_ref, sem_ref = rest
    else:
        seg_refs = rest[:len(seg_ends)]
        fbuf_ref, sem_ref, y_ref = rest[len(seg_ends):]
        yb_ref = None

    @pl.when(i == 0)
    def _():
        _issue_row_gather(idx_cur_ref, f_hbm, fbuf_ref.at[0], sem_ref.at[0], LN2_TM)

    def tile(cur, nxt):
        _wait_row_gather(f_hbm, fbuf_ref.at[cur], sem_ref.at[cur], LN2_TM)
        g = g_ref[...]
        b = b_ref[...]
        for c in range(LN2_TM // LN2_CHUNK):
            rows = pl.ds(c * LN2_CHUNK, LN2_CHUNK)
            for r in range(c * LN2_CHUNK, (c + 1) * LN2_CHUNK):
                pltpu.make_async_copy(f_hbm.at[pl.ds(idx_nxt_ref[0, r], 1), :],
                                      fbuf_ref.at[nxt, pl.ds(r, 1), :], sem_ref.at[nxt]).start()
            y = _layer_norm_rows(DEEPNORM_ALPHA * x_ref[rows, :] + fbuf_ref[cur, rows, :], g, b)
            y_ref[rows, :] = y
            if yb_ref is not None:
                yb_ref[rows, :] = y.astype(BF16)

    @pl.when(i % 2 == 0)
    def _():
        tile(0, 1)

    @pl.when(i % 2 == 1)
    def _():
        tile(1, 0)

    if seg_ends is not None:
        start = 0
        for ref, end in zip(seg_refs, seg_ends):
            @pl.when((i >= start) & (i < end))
            def _(ref=ref):
                ref[...] = y_ref[...]
            start = end

    @pl.when(i == n_tiles - 1)
    def _():
        spare = n_tiles % 2
        _wait_row_gather(f_hbm, fbuf_ref.at[spare], sem_ref.at[spare], LN2_TM)


def _ln2_gather(f_sorted, pos, x, g, b, seg_rows=None):
    t = x.shape[0]
    d = f_sorted.shape[1]
    n_tiles = t // LN2_TM
    idx = pos.reshape(n_tiles, 1, LN2_TM)
    row = pl.BlockSpec((LN2_TM, d), lambda i: (i, 0))
    vec = pl.BlockSpec((1, d), lambda i: (0, 0))
    scratch = [pltpu.VMEM((2, LN2_TM, d), F32), pltpu.SemaphoreType.DMA((2,))]
    if seg_rows is None:
        seg_ends = None
        out_specs = [row, row]
        out_shape = [jax.ShapeDtypeStruct((t, d), F32), jax.ShapeDtypeStruct((t, d), BF16)]
    else:
        seg_tiles = [r // LN2_TM for r in seg_rows]
        seg_ends = tuple(int(v) for v in np.cumsum(seg_tiles))

        def seg_spec(s):
            first, count = seg_ends[s] - seg_tiles[s], seg_tiles[s]
            return pl.BlockSpec((LN2_TM, d), lambda i: (jnp.clip(i - first, 0, count - 1), 0))

        out_specs = [seg_spec(s) for s in range(len(seg_rows))]
        out_shape = [jax.ShapeDtypeStruct((r, d), F32) for r in seg_rows]
        scratch.append(pltpu.VMEM((LN2_TM, d), F32))
    return pl.pallas_call(
        functools.partial(_ln2_kernel, n_tiles=n_tiles, seg_ends=seg_ends),
        grid=(n_tiles,),
        in_specs=[
            pl.BlockSpec((None, 1, LN2_TM), lambda i: (i, 0, 0), memory_space=pltpu.SMEM),
            pl.BlockSpec((None, 1, LN2_TM), lambda i: (jnp.minimum(i + 1, n_tiles - 1), 0, 0),
                         memory_space=pltpu.SMEM),
            pl.BlockSpec(memory_space=pl.ANY),
            row, vec, vec,
        ],
        out_specs=out_specs,
        out_shape=out_shape,
        scratch_shapes=scratch,
        compiler_params=_params("arbitrary"),
        name="ln2_gather",
    )(idx, idx, f_sorted, x, g, b)


def _da_weight_layout(w_in):
    d = w_in.shape[0]
    w = DA_QK_WIDTH
    q1, q2, k1, k2 = (w_in[:, i * w:(i + 1) * w].reshape(d, DA_HEADS, DA_HEAD_DIM) for i in range(4))
    qk = jnp.concatenate([jnp.concatenate([q1, q2], axis=2).reshape(d, 2 * w),
                          jnp.concatenate([k1, k2], axis=2).reshape(d, 2 * w)], axis=1)
    return qk, w_in[:, 4 * w:]


def _trunk(xs, da_w_in, da_w_out, da_lambda_q1, da_lambda_k1, da_lambda_q2, da_lambda_k2, da_subln_g,
           dl_w_in, dl_w_out, ln1_g, ln1_b, ln2_g, ln2_b,
           moe_router_group, moe_router_expert, moe_w_gate, moe_w_up, moe_w_down):
    d = xs[0].shape[-1]
    segs = []
    off = 0
    for x in xs:
        segs.append((off, x.shape[0], x.shape[1]))
        off += x.shape[0] * x.shape[1]
    t_total = off
    x = jnp.concatenate([v.reshape(-1, d) for v in xs], axis=0)
    xb = x.astype(BF16)

    max_len = max(s for _, _, s in segs)
    rope = _rope_tables(max_len)
    pos_blocks = np.concatenate([
        np.tile(np.arange(s // PROJ_TM, dtype=np.int32), b) for _, b, s in segs])
    pos_blocks = jnp.asarray(pos_blocks)

    for i in range(DEPTH):
        j = i // N_MIXERS
        if i % N_MIXERS == 0:
            lambda_init = 0.8 - 0.6 * math.exp(-0.3 * i)
            w_qk, w_v = _da_weight_layout(da_w_in[j])
            qk = _proj(xb, w_qk.astype(BF16), rope, pos_blocks,
                       q_cols=2 * DA_QK_WIDTH, rope_cols=4 * DA_QK_WIDTH, q_scale=QK_SCALE_LOG2)
            vt = _proj_t(xb, w_v.T.astype(BF16))
            lam_params = jnp.stack([da_lambda_q1[j], da_lambda_k1[j], da_lambda_q2[j], da_lambda_k2[j]]).astype(F32)
            g_col = da_subln_g[j].astype(F32).reshape(DA_V_DIM, 1)
            os_ = [[_da_attention(qk, vt, lam_params, g_col, lambda_init, ro, b, s) for ro, b, s in segs]]
            lses = []
            w_out = da_w_out[j]
        else:
            os_, lses = [], []
            gw = 3 * DIL_GROUP_WIDTH
            for g, (window, dilation) in enumerate(DIL_PATTERNS):
                assert window // (2 * dilation) == DIL_HALF
                proj = _proj(xb, dl_w_in[j][:, g * gw:(g + 1) * gw].astype(BF16), rope, pos_blocks,
                             q_cols=DIL_GROUP_WIDTH, rope_cols=2 * DIL_GROUP_WIDTH, q_scale=QK_SCALE_LOG2,
                             dilation=dilation)
                pairs = [_dil_attention_group(proj, g, dilation, ro, b, s) for ro, b, s in segs]
                os_.append([p[0] for p in pairs])
                lses.append([p[1] for p in pairs])
            w_out = dl_w_out[j]
        w_router = jnp.concatenate([
            moe_router_group[i].astype(F32),
            moe_router_expert[i].reshape(d, N_EXPERTS).astype(F32),
            jnp.zeros((d, LANES - MOE_GROUPS - N_EXPERTS), F32)], axis=1)
        xc = _post_attention(os_, lses, [b * s for _, b, s in segs], w_out.astype(BF16), x,
                             ln1_g[i].reshape(1, d).astype(F32), ln1_b[i].reshape(1, d).astype(F32), w_router)
        gid = xc[:, d + ROUTER_GROUP_LANE].astype(jnp.int32)
        pos, src, tile_group, n_valid = _sort_by_group(gid, t_total, t_total + MOE_GROUPS * MOE_TM)
        f_sorted = _moe_ffn(xc, src, tile_group, n_valid,
                            moe_w_gate[i].astype(BF16), moe_w_up[i].astype(BF16), moe_w_down[i].astype(BF16))
        ln2 = (ln2_g[i].reshape(1, d).astype(F32), ln2_b[i].reshape(1, d).astype(F32))
        if i + 1 < DEPTH:
            x, xb = _ln2_gather(f_sorted, pos, xc, *ln2)
        else:
            ys = _ln2_gather(f_sorted, pos, xc, *ln2, seg_rows=[b * s for _, b, s in segs])
    return [y.reshape(v.shape) for y, v in zip(ys, xs)]


def kernel(x_prompt, x_sample, da_w_in, da_w_out, da_lambda_q1, da_lambda_k1, da_lambda_q2, da_lambda_k2, da_subln_g, dl_w_in, dl_w_out, ln1_g, ln1_b, ln2_g, ln2_b, moe_router_group, moe_router_expert, moe_w_gate, moe_w_up, moe_w_down):
    y_prompt, y_sample = _trunk(
        [x_prompt, x_sample], da_w_in, da_w_out, da_lambda_q1, da_lambda_k1, da_lambda_q2, da_lambda_k2,
        da_subln_g, dl_w_in, dl_w_out, ln1_g, ln1_b, ln2_g, ln2_b,
        moe_router_group, moe_router_expert, moe_w_gate, moe_w_up, moe_w_down)
    return (y_prompt, y_sample)
```

```python
import functools
import math

import jax
import jax.numpy as jnp
import numpy as np
from jax import lax
from jax.experimental import pallas as pl
from jax.experimental.pallas import tpu as pltpu

F32 = jnp.float32
BF16 = jnp.bfloat16

LANES = 128
VMEM_LIMIT_BYTES = 56 * 1024 * 1024

D_MODEL = 1024
DEPTH = 4
N_MIXERS = 2

DA_HEADS = 8
DA_HEAD_DIM = 64
DA_V_DIM = 2 * DA_HEAD_DIM
DA_QK_WIDTH = DA_HEADS * DA_HEAD_DIM

DIL_PATTERNS = ((128, 1), (512, 4), (2048, 16))
DIL_GROUPS = len(DIL_PATTERNS)
DIL_HEADS = 16
DIL_HEAD_DIM = 64
DIL_GROUP_WIDTH = DIL_HEADS * DIL_HEAD_DIM
DIL_HALF = 64

ROPE_THETA = 500000.0
ROPE_ROT = DA_HEAD_DIM // 4
ROPE_HALF = ROPE_ROT // 2

LN_EPS = 1e-5
DEEPNORM_ALPHA = (2.0 * DEPTH) ** 0.25

MOE_GROUPS = 4
MOE_EXPERTS = 4
MOE_FF = 512
N_EXPERTS = MOE_GROUPS * MOE_EXPERTS

NEG_INF = -1e30
LOG2_E = math.log2(math.e)
QK_SCALE_LOG2 = DA_HEAD_DIM ** -0.5 * LOG2_E


def _params(*sem):
    return pltpu.CompilerParams(dimension_semantics=sem, vmem_limit_bytes=VMEM_LIMIT_BYTES)


PROJ_TM = 1024
PROJ_TN = 1024


def _rope_tables(max_len):
    inv = ROPE_THETA ** (-jnp.arange(ROPE_HALF, dtype=F32) * (2.0 / ROPE_ROT))
    ang = jnp.arange(max_len, dtype=F32)[:, None] * inv[None, :]
    cos, sin = jnp.cos(ang), jnp.sin(ang)
    ones = jnp.ones((max_len, DA_HEAD_DIM - ROPE_ROT), F32)
    zeros_h = jnp.zeros((max_len, ROPE_HALF), F32)
    zeros_r = jnp.zeros((max_len, DA_HEAD_DIM - ROPE_ROT), F32)
    c = jnp.concatenate([cos, cos, ones], axis=1)
    s_up = jnp.concatenate([zeros_h, sin, zeros_r], axis=1)
    s_dn = jnp.concatenate([-sin, zeros_h, zeros_r], axis=1)
    reps = LANES // DA_HEAD_DIM
    return jnp.stack([jnp.tile(c, (1, reps)), jnp.tile(s_up, (1, reps)), jnp.tile(s_dn, (1, reps))])


PROJ_MAX_STRIDE = 4
PROJ_CHUNK = 256


def _proj_kernel(pos_ref, x_ref, w_ref, rope_ref, o_ref, ybuf_ref, zbuf_ref, *, n_cols, q_cols, rope_cols, q_scale, dilation):
    del pos_ref
    x = x_ref[...]
    c = rope_ref[0]
    s_up = rope_ref[1]
    s_dn = rope_ref[2]
    rows = x.shape[0] // dilation
    for k in range(n_cols // PROJ_CHUNK):
        c0 = k * PROJ_CHUNK
        y = jnp.dot(x, w_ref[:, c0:c0 + PROJ_CHUNK], preferred_element_type=F32)
        if c0 < q_cols:
            y = y * q_scale
        parts = []
        for h in range(PROJ_CHUNK // LANES):
            yc = y[:, h * LANES:(h + 1) * LANES]
            if c0 < rope_cols:
                yc = yc * c + pltpu.roll(yc, ROPE_HALF, 1) * s_up + pltpu.roll(yc, LANES - ROPE_HALF, 1) * s_dn
            parts.append(yc)
        if dilation == 1:
            for h, yc in enumerate(parts):
                o_ref[0, :, c0 + h * LANES:c0 + (h + 1) * LANES] = yc.astype(o_ref.dtype)
        else:
            for h, yc in enumerate(parts):
                slot = (2 * k + h) % ybuf_ref.shape[0]
                cols = slice(c0 + h * LANES, c0 + (h + 1) * LANES)
                ybuf_ref[slot] = yc
                if dilation <= PROJ_MAX_STRIDE:
                    for r in range(dilation):
                        o_ref[r, :, cols] = ybuf_ref[slot, pl.ds(r, rows, stride=dilation), :].astype(o_ref.dtype)
                else:
                    outer = dilation // PROJ_MAX_STRIDE
                    zslot = h % zbuf_ref.shape[0]
                    for r in range(PROJ_MAX_STRIDE):
                        zbuf_ref[zslot, r] = ybuf_ref[slot, pl.ds(r, rows * outer, stride=PROJ_MAX_STRIDE), :]
                        for a in range(outer):
                            o_ref[r + PROJ_MAX_STRIDE * a, :, cols] = (
                                zbuf_ref[zslot, r, pl.ds(a, rows, stride=outer), :].astype(o_ref.dtype))


def _proj(xb, w, rope, pos_blocks, *, q_cols, rope_cols, q_scale, dilation=1):
    t, d = xb.shape
    n = w.shape[1]
    rows = PROJ_TM // dilation
    grid_spec = pltpu.PrefetchScalarGridSpec(
        num_scalar_prefetch=1,
        grid=(t // PROJ_TM,),
        in_specs=[
            pl.BlockSpec((PROJ_TM, d), lambda i, pos: (i, 0)),
            pl.BlockSpec((d, n), lambda i, pos: (0, 0)),
            pl.BlockSpec((3, PROJ_TM, LANES), lambda i, pos: (0, pos[i], 0)),
        ],
        out_specs=pl.BlockSpec((dilation, rows, n), lambda i, pos: (0, i, 0)),
        scratch_shapes=[pltpu.VMEM((4, PROJ_TM, LANES), F32),
                        pltpu.VMEM((2, PROJ_MAX_STRIDE, PROJ_TM // PROJ_MAX_STRIDE, LANES), F32)],
    )
    return pl.pallas_call(
        functools.partial(_proj_kernel, n_cols=n, q_cols=q_cols, rope_cols=rope_cols, q_scale=q_scale,
                          dilation=dilation),
        grid_spec=grid_spec,
        out_shape=jax.ShapeDtypeStruct((dilation, t // dilation, n), BF16),
        compiler_params=_params("parallel"),
        name="proj_rope",
    )(pos_blocks, xb, w, rope)


DA_TK = 512
DA_ONES_ROWS = 16
DA_VT_ROWS = DA_V_DIM + DA_ONES_ROWS


def _projt_kernel(x_ref, wt_ref, o_ref):
    y = lax.dot_general(wt_ref[...], x_ref[...], (((1,), (1,)), ((), ())), preferred_element_type=F32)
    h = o_ref.shape[0]
    o_ref[:, 0, 0:DA_V_DIM, :] = y.reshape(h, DA_V_DIM, DA_TK).astype(o_ref.dtype)
    o_ref[:, 0, DA_V_DIM:DA_VT_ROWS, :] = jnp.ones((h, DA_ONES_ROWS, DA_TK), o_ref.dtype)


def _proj_t(xb, wt):
    t, d = xb.shape
    n = wt.shape[0]
    h = n // DA_V_DIM
    return pl.pallas_call(
        _projt_kernel,
        grid=(t // DA_TK,),
        in_specs=[
            pl.BlockSpec((DA_TK, d), lambda i: (i, 0)),
            pl.BlockSpec((n, d), lambda i: (0, 0)),
        ],
        out_specs=pl.BlockSpec((h, 1, DA_VT_ROWS, DA_TK), lambda i: (0, i, 0, 0)),
        out_shape=jax.ShapeDtypeStruct((h, t // DA_TK, DA_VT_ROWS, DA_TK), BF16),
        compiler_params=_params("parallel"),
        name="proj_vt",
    )(xb, wt)


DA_TQ = 512
DA_PAIR_UNROLL = 3


def _da_attn_kernel(lam_ref, g_ref, q_ref, k_ref, vt_ref, o_ref, q1_ref, q2_ref,
                    sa1_ref, sa2_ref, sb1_ref, sb2_ref, xa1_ref, xa2_ref, xb1_ref, xb2_ref,
                    m1_ref, a1_ref, m2_ref, a2_ref, *, n_kblk, lambda_init):
    buf_a = ((sa1_ref, xa1_ref), (sa2_ref, xa2_ref))
    buf_b = ((sb1_ref, xb1_ref), (sb2_ref, xb2_ref))
    qt = q_ref[...].astype(F32).T
    row = lax.broadcasted_iota(jnp.int32, qt.shape, 0)
    q1_ref[...] = jnp.where(row < DA_HEAD_DIM, qt, 0.0).astype(q1_ref.dtype)
    q2_ref[...] = jnp.where(row >= DA_HEAD_DIM, qt, 0.0).astype(q2_ref.dtype)
    for m_ref, a_ref in ((m1_ref, a1_ref), (m2_ref, a2_ref)):
        m_ref[...] = jnp.full(m_ref.shape, NEG_INF, F32)
        a_ref[...] = jnp.zeros(a_ref.shape, F32)

    def scores(j, buf):
        k = k_ref[pl.ds(pl.multiple_of(j * DA_TK, DA_TK), DA_TK), :]
        for qm_ref, (s_ref, x_ref) in zip((q1_ref, q2_ref), buf):
            s = jnp.dot(k, qm_ref[...], preferred_element_type=F32)
            s_ref[...] = s
            x_ref[...] = jnp.max(s.reshape(DA_TK // 8, 8, DA_TQ), axis=0)

    def softmax_pv(j, buf):
        vt = vt_ref[0, j]
        for (s_ref, x_ref), m_ref, a_ref in zip(buf, (m1_ref, m2_ref), (a1_ref, a2_ref)):
            m_old = m_ref[...]
            m_new = jnp.maximum(m_old, jnp.max(x_ref[...], axis=0, keepdims=True))
            p = jnp.exp2(s_ref[...] - m_new).astype(vt.dtype)
            a_ref[...] = jnp.exp2(m_old - m_new) * a_ref[...] + jnp.dot(vt, p, preferred_element_type=F32)
            m_ref[...] = m_new

    scores(0, buf_a)

    def pair(jj, carry):
        j = 2 * jj
        scores(j + 1, buf_b)
        softmax_pv(j, buf_a)
        scores(j + 2, buf_a)
        softmax_pv(j + 1, buf_b)
        return carry

    n_pairs = n_kblk // 2 - 1
    lax.fori_loop(0, n_pairs, pair, 0, unroll=DA_PAIR_UNROLL if n_pairs > DA_PAIR_UNROLL else 2)
    scores(n_kblk - 1, buf_b)
    softmax_pv(n_kblk - 2, buf_a)
    softmax_pv(n_kblk - 1, buf_b)

    lp = lam_ref[...]
    lam = (jnp.exp(jnp.sum(lp[0:1] * lp[1:2], axis=1, keepdims=True))
           - jnp.exp(jnp.sum(lp[2:3] * lp[3:4], axis=1, keepdims=True)) + lambda_init)
    a1 = a1_ref[...]
    a2 = a2_ref[...]
    o = (a1[0:DA_V_DIM] * (1.0 / a1[DA_V_DIM:DA_V_DIM + 1])
         - lam * (a2[0:DA_V_DIM] * (1.0 / a2[DA_V_DIM:DA_V_DIM + 1])))
    o = o * lax.rsqrt(jnp.mean(jnp.square(o), axis=0, keepdims=True) + LN_EPS)
    o = o * g_ref[...] * (1.0 - lambda_init)
    o_ref[...] = o.T.astype(o_ref.dtype)


def _da_attention(qk, vt, lam_params, subln_g, lambda_init, row_off, batch, seq):
    h = DA_HEADS
    nq = seq // DA_TQ
    nk = seq // DA_TK
    assert nk % 2 == 0 and nk >= 2
    qoff = row_off // DA_TQ
    soff = row_off // seq
    s_buf = pltpu.VMEM((DA_TK, DA_TQ), F32)
    x_buf = pltpu.VMEM((8, DA_TQ), F32)
    in_specs = [
        pl.BlockSpec((4, DA_HEAD_DIM), lambda b, hh, i: (0, 0)),
        pl.BlockSpec((DA_V_DIM, 1), lambda b, hh, i: (0, 0)),
        pl.BlockSpec((None, DA_TQ, LANES), lambda b, hh, i: (0, qoff + b * nq + i, hh)),
        pl.BlockSpec((None, seq, LANES), lambda b, hh, i: (0, soff + b, h + hh)),
        pl.BlockSpec((1, nk, DA_VT_ROWS, DA_TK), lambda b, hh, i: (hh, soff + b, 0, 0)),
    ]
    return pl.pallas_call(
        functools.partial(_da_attn_kernel, n_kblk=nk, lambda_init=lambda_init),
        grid=(batch, h, nq),
        in_specs=in_specs,
        out_specs=pl.BlockSpec((None, DA_TQ, DA_V_DIM), lambda b, hh, i: (0, b * nq + i, hh)),
        out_shape=jax.ShapeDtypeStruct((1, batch * seq, h * DA_V_DIM), BF16),
        scratch_shapes=[
            pltpu.VMEM((LANES, DA_TQ), BF16), pltpu.VMEM((LANES, DA_TQ), BF16),
            s_buf, s_buf, s_buf, s_buf,
            x_buf, x_buf, x_buf, x_buf,
            pltpu.VMEM((1, DA_TQ), F32), pltpu.VMEM((DA_VT_ROWS, DA_TQ), F32),
            pltpu.VMEM((1, DA_TQ), F32), pltpu.VMEM((DA_VT_ROWS, DA_TQ), F32),
        ],
        compiler_params=_params("parallel", "parallel", "arbitrary"),
        name="da_attn",
    )(lam_params, subln_g, qk, qk, vt)


DIL_QSUB = 128
DIL_KWIN = DIL_QSUB + 2 * DIL_HALF
DIL_QBLK = 1024
DIL_SUBS_PER_STEP = 16


def _dil_attn_kernel(q_ref, k_ref, v_ref, o_ref, lse_ref, *, sub_len, qblk):
    qi = pl.program_id(3)
    lane = lax.broadcasted_iota(jnp.int32, (DIL_QSUB, LANES), 1)
    lo = lane < DIL_HEAD_DIM
    row = lax.broadcasted_iota(jnp.int32, (DIL_QSUB, DIL_KWIN), 0)
    col = lax.broadcasted_iota(jnp.int32, (DIL_QSUB, DIL_KWIN), 1)

    interior = jnp.abs(col - DIL_HALF - row) <= DIL_HALF
    n_sub = qblk // DIL_QSUB

    for pair_idx, n in [(a, b) for a in range(q_ref.shape[1] // LANES) for b in range(n_sub)]:
        cols = slice(pair_idx * LANES, (pair_idx + 1) * LANES)
        q0 = n * DIL_QSUB
        gq0 = qi * qblk + q0
        ws = jnp.clip(gq0 - DIL_HALF, 0, sub_len - DIL_KWIN)
        ws = pl.multiple_of(ws, DIL_HALF)
        qb = q_ref[pl.ds(q0, DIL_QSUB), cols]
        kw = k_ref[pl.ds(ws, DIL_KWIN), cols]
        vw = v_ref[pl.ds(ws, DIL_KWIN), cols]
        if 0 < n < n_sub - 1:
            valid = interior
        else:
            valid = jnp.abs((ws + col) - (gq0 + row)) <= DIL_HALF
        zero = jnp.zeros_like(qb)
        outs, lses = [], []
        for qm in (jnp.where(lo, qb, zero), jnp.where(lo, zero, qb)):
            s = lax.dot_general(qm, kw, (((1,), (1,)), ((), ())), preferred_element_type=F32)
            s = jnp.where(valid, s, NEG_INF)
            m = jnp.max(s, axis=1, keepdims=True)
            p = jnp.exp2(s - m)
            l = jnp.sum(p, axis=1, keepdims=True)
            o = jnp.dot(p.astype(vw.dtype), vw, preferred_element_type=F32) * (1.0 / l)
            outs.append(o)
            lses.append(jnp.broadcast_to(m + jnp.log(l) * LOG2_E, (DIL_QSUB, LANES)))
        o_ref[pl.ds(q0, DIL_QSUB), cols] = jnp.where(lo, outs[0], outs[1]).astype(o_ref.dtype)
        lse_ref[pl.ds(q0, DIL_QSUB), cols] = jnp.where(lo, lses[0], lses[1])


def _dil_attention_group(proj, g, dilation, row_off, batch, seq):
    sub_len = seq // dilation
    qblk = min(DIL_QBLK, sub_len)
    nqb = sub_len // qblk
    n_pairs = DIL_GROUP_WIDTH // LANES
    per_step = min(n_pairs, max(1, DIL_SUBS_PER_STEP // (qblk // DIL_QSUB)))
    hp = n_pairs // per_step
    width = per_step * LANES
    soff = row_off // seq

    def qmap(b, r, p, i):
        return (r, (soff + b) * nqb + i, p)

    def kmap(b, r, p, i):
        return (r, soff + b, hp + p)

    def vmap(b, r, p, i):
        return (r, soff + b, 2 * hp + p)

    def omap(b, r, p, i):
        return (r, b * nqb + i, p)

    return pl.pallas_call(
        functools.partial(_dil_attn_kernel, sub_len=sub_len, qblk=qblk),
        grid=(batch, dilation, hp, nqb),
        in_specs=[
            pl.BlockSpec((None, qblk, width), qmap),
            pl.BlockSpec((None, sub_len, width), kmap),
            pl.BlockSpec((None, sub_len, width), vmap),
        ],
        out_specs=[
            pl.BlockSpec((None, qblk, width), omap),
            pl.BlockSpec((None, qblk, width), omap),
        ],
        out_shape=[
            jax.ShapeDtypeStruct((dilation, batch * sub_len, DIL_GROUP_WIDTH), BF16),
            jax.ShapeDtypeStruct((dilation, batch * sub_len, DIL_GROUP_WIDTH), F32),
        ],
        compiler_params=_params("parallel", "parallel", "parallel", "arbitrary"),
        name=f"dil_attn_g{g}",
    )(proj, proj, proj)


POST_TM = 256


def _layer_norm_rows(z, g, b):
    mu = jnp.mean(z, axis=-1, keepdims=True)
    zc = z - mu
    var = jnp.mean(jnp.square(zc), axis=-1, keepdims=True)
    return zc * lax.rsqrt(var + LN_EPS) * g + b


def _token_order_reader(seg_refs, seg_ends, scratch_ref):
    i = pl.program_id(0)
    dil, rows, width = seg_refs[0].shape
    if dil == 1:
        def read(c):
            cs = slice(c * LANES, (c + 1) * LANES)
            val = seg_refs[-1][0, :, cs]
            for ref, end in reversed(list(zip(seg_refs[:-1], seg_ends[:-1]))):
                val = jnp.where(i < end, ref[0, :, cs], val)
            return val.astype(F32)
        return read
    start = 0
    for ref, end in zip(seg_refs, seg_ends):
        @pl.when((i >= start) & (i < end))
        def _(ref=ref):
            for r in range(dil):
                for c in range(width // LANES):
                    scratch_ref[c, pl.ds(r, rows, stride=dil), :] = (
                        ref[r, :, c * LANES:(c + 1) * LANES].astype(F32))
        start = end
    return lambda c: scratch_ref[c]


def _post_kernel(*refs, n_groups, seg_ends):
    n_seg = len(seg_ends)
    o_refs = [refs[g * n_seg:(g + 1) * n_seg] for g in range(n_groups)]
    n_in = n_groups * n_seg
    lse_refs = []
    if n_groups > 1:
        lse_refs = [refs[n_in + g * n_seg:n_in + (g + 1) * n_seg] for g in range(n_groups)]
        n_in *= 2
    w_ref, x_ref, g_ref, b_ref, wr_ref, xc_ref = refs[n_in:n_in + 6]
    scratch = refs[n_in + 6:]
    d = x_ref.shape[1]
    o_scr = scratch[-1]
    spare = iter(scratch[:-1])
    readers = [_token_order_reader(r, seg_ends, next(spare) if r[0].shape[0] > 1 else None)
               for r in (*o_refs, *lse_refs)]
    read_o, read_lse = readers[:n_groups], readers[n_groups:]
    for c in range(d // LANES):
        if n_groups == 1:
            o_c = read_o[0](c)
        else:
            lses = [rd(c) for rd in read_lse]
            m = functools.reduce(jnp.maximum, lses)
            es = [jnp.exp2(l - m) for l in lses]
            inv = 1.0 / functools.reduce(jnp.add, es)
            o_c = functools.reduce(jnp.add, [(e * inv) * rd(c) for e, rd in zip(es, read_o)])
        o_scr[:, c * LANES:(c + 1) * LANES] = o_c.astype(BF16)
    h = jnp.dot(o_scr[...], w_ref[...], preferred_element_type=F32)
    y = _layer_norm_rows(DEEPNORM_ALPHA * x_ref[...] + h, g_ref[...], b_ref[...])
    xc_ref[:, 0:d] = y
    xc_ref[:, d:d + LANES] = _routing_lanes(y, wr_ref[...])


def _post_attention(os_, lses, seg_rows, w_out, x, g, b, w_router):
    t, d = x.shape
    n_groups = len(os_)
    seg_tiles = [r // POST_TM for r in seg_rows]
    seg_ends = tuple(int(v) for v in np.cumsum(seg_tiles))
    row = pl.BlockSpec((POST_TM, d), lambda i: (i, 0))
    full = pl.BlockSpec((d, d), lambda i: (0, 0))
    vec = pl.BlockSpec((1, d), lambda i: (0, 0))

    def seg_spec(a, s):
        first, count = seg_ends[s] - seg_tiles[s], seg_tiles[s]
        return pl.BlockSpec((a.shape[0], POST_TM // a.shape[0], d),
                            lambda i: (0, jnp.clip(i - first, 0, count - 1), 0))

    arrays = [a for grp in (*os_, *lses) for a in grp]
    specs = [seg_spec(a, s) for grp in (*os_, *lses) for s, a in enumerate(grp)]
    n_scratch = sum(1 for grp in (*os_, *lses) if grp[0].shape[0] > 1)
    return pl.pallas_call(
        functools.partial(_post_kernel, n_groups=n_groups, seg_ends=seg_ends),
        grid=(t // POST_TM,),
        in_specs=specs + [full, row, vec, vec, pl.BlockSpec((d, LANES), lambda i: (0, 0))],
        out_specs=pl.BlockSpec((POST_TM, d + LANES), lambda i: (i, 0)),
        out_shape=jax.ShapeDtypeStruct((t, d + LANES), F32),
        scratch_shapes=[pltpu.VMEM((d // LANES, POST_TM, LANES), F32)] * n_scratch + [pltpu.VMEM((POST_TM, d), BF16)],
        compiler_params=_params("parallel"),
        name="post_attn_ln",
    )(*arrays, w_out, x, g, b, w_router)


ROUTER_TM = 512
ROUTER_EXPERT_LANE0 = MOE_GROUPS
ROUTER_GROUP_LANE = 0


def _first_argmax(vals, vmax, lane_f):
    return jnp.min(jnp.where(vals == vmax, lane_f, float(LANES)), axis=1, keepdims=True)


def _routing_lanes(x, w_router):
    xh = x.astype(BF16)
    xl = (x - xh.astype(F32)).astype(BF16)
    wh = w_router.astype(BF16)
    wl = (w_router - wh.astype(F32)).astype(BF16)
    logits = (jnp.dot(xh, wh, preferred_element_type=F32) + jnp.dot(xh, wl, preferred_element_type=F32)
              + jnp.dot(xl, wh, preferred_element_type=F32))
    lane_f = lax.broadcasted_iota(jnp.int32, logits.shape, 1).astype(F32)
    gl = jnp.where(lane_f < MOE_GROUPS, logits, NEG_INF)
    gmax = jnp.max(gl, axis=1, keepdims=True)
    gw = 1.0 / jnp.sum(jnp.exp(gl - gmax), axis=1, keepdims=True)
    gsel = _first_argmax(gl, gmax, lane_f)
    e0 = ROUTER_EXPERT_LANE0 + MOE_EXPERTS * gsel
    el = jnp.where((lane_f >= e0) & (lane_f < e0 + MOE_EXPERTS), logits, NEG_INF)
    v1 = jnp.max(el, axis=1, keepdims=True)
    i1 = _first_argmax(el, v1, lane_f)
    el2 = jnp.where(lane_f == i1, NEG_INF, el)
    v2 = jnp.max(el2, axis=1, keepdims=True)
    i2 = _first_argmax(el2, v2, lane_f)
    e2 = jnp.exp(v2 - v1)
    inv = gw / (1.0 + e2)
    comb = jnp.where(lane_f == i1, inv, 0.0) + jnp.where(lane_f == i2, e2 * inv, 0.0)
    return comb + jnp.where(lane_f == ROUTER_GROUP_LANE, gsel, 0.0)


MOE_TM = 512
GATHER_UNROLL = 8


def _issue_row_gather(idx_ref, src_hbm, dst_ref, sem, n_rows):
    def body(r, carry):
        pltpu.make_async_copy(src_hbm.at[pl.ds(idx_ref[0, r], 1), :], dst_ref.at[pl.ds(r, 1), :], sem).start()
        return carry
    lax.fori_loop(0, n_rows, body, 0, unroll=GATHER_UNROLL)


def _wait_row_gather(src_hbm, dst_ref, sem, n_rows):
    pltpu.make_async_copy(src_hbm.at[pl.ds(0, n_rows), :], dst_ref, sem).wait()


def _sort_by_group(gid, t, n_pad_rows):
    groups = jnp.arange(MOE_GROUPS, dtype=jnp.int32)
    onehot = (gid[:, None] == groups[None, :]).astype(jnp.int32)
    csum = jnp.cumsum(onehot, axis=0)
    counts = csum[-1]
    rank = jnp.sum(onehot * csum, axis=1) - 1
    tiles_g = (counts + MOE_TM - 1) // MOE_TM
    tile_end = jnp.cumsum(tiles_g)
    row_start = (tile_end - tiles_g) * MOE_TM
    pos = jnp.sum(onehot * row_start[None, :], axis=1) + rank
    src = jnp.zeros((n_pad_rows,), jnp.int32).at[pos].set(jnp.arange(t, dtype=jnp.int32))
    tile_ids = jnp.arange(n_pad_rows // MOE_TM, dtype=jnp.int32)
    tile_group = jnp.minimum(jnp.sum((tile_ids[:, None] >= tile_end[None, :]).astype(jnp.int32), axis=1),
                             MOE_GROUPS - 1)
    return pos, src, tile_group, tile_end[-1:]


def _moe_ffn_kernel(tg_ref, nv_ref, idx_cur_ref, idx_nxt_ref, xc_hbm, wg_ref, wu_ref, wd_ref, f_ref,
                    xbuf_ref, sem_ref):
    i = pl.program_id(0)
    slot = i % 2
    n_valid = nv_ref[0]
    d = f_ref.shape[1]

    @pl.when(i == 0)
    def _():
        _issue_row_gather(idx_cur_ref, xc_hbm, xbuf_ref.at[0], sem_ref.at[0], MOE_TM)

    @pl.when(i + 1 < n_valid)
    def _():
        _issue_row_gather(idx_nxt_ref, xc_hbm, xbuf_ref.at[1 - slot], sem_ref.at[1 - slot], MOE_TM)

    @pl.when(i < n_valid)
    def _():
        _wait_row_gather(xc_hbm, xbuf_ref.at[slot], sem_ref.at[slot], MOE_TM)
        xb = xbuf_ref[slot, :, 0:d].astype(BF16)
        comb = xbuf_ref[slot, :, d:d + LANES]
        lane = lax.broadcasted_iota(jnp.int32, comb.shape, 1)
        lane0 = ROUTER_EXPERT_LANE0 + MOE_EXPERTS * tg_ref[i]
        acc = None
        for e in range(MOE_EXPERTS):
            ce = jnp.sum(jnp.where(lane == lane0 + e, comb, 0.0), axis=1, keepdims=True)
            gate = jnp.dot(xb, wg_ref[0, e], preferred_element_type=F32)
            up = jnp.dot(xb, wu_ref[0, e], preferred_element_type=F32)
            hid = (gate * jax.nn.sigmoid(gate)) * up * ce
            part = jnp.dot(hid.astype(BF16), wd_ref[0, e], preferred_element_type=F32)
            acc = part if acc is None else acc + part
        f_ref[...] = acc

    @pl.when(i >= n_valid)
    def _():
        f_ref[...] = jnp.zeros(f_ref.shape, F32)


def _moe_ffn(xc, src, tile_group, n_valid, w_gate, w_up, w_down):
    d = xc.shape[1] - LANES
    n_tiles = tile_group.shape[0]
    _, ne, _, f = w_gate.shape
    idx = src.reshape(n_tiles, 1, MOE_TM)
    grid_spec = pltpu.PrefetchScalarGridSpec(
        num_scalar_prefetch=2,
        grid=(n_tiles,),
        in_specs=[
            pl.BlockSpec((None, 1, MOE_TM), lambda i, tg, nv: (i, 0, 0), memory_space=pltpu.SMEM),
            pl.BlockSpec((None, 1, MOE_TM), lambda i, tg, nv: (jnp.minimum(i + 1, n_tiles - 1), 0, 0),
                         memory_space=pltpu.SMEM),
            pl.BlockSpec(memory_space=pl.ANY),
            pl.BlockSpec((1, ne, d, f), lambda i, tg, nv: (tg[i], 0, 0, 0)),
            pl.BlockSpec((1, ne, d, f), lambda i, tg, nv: (tg[i], 0, 0, 0)),
            pl.BlockSpec((1, ne, f, d), lambda i, tg, nv: (tg[i], 0, 0, 0)),
        ],
        out_specs=pl.BlockSpec((MOE_TM, d), lambda i, tg, nv: (i, 0)),
        scratch_shapes=[pltpu.VMEM((2, MOE_TM, d + LANES), F32), pltpu.SemaphoreType.DMA((2,))],
    )
    return pl.pallas_call(
        _moe_ffn_kernel,
        grid_spec=grid_spec,
        out_shape=jax.ShapeDtypeStruct((n_tiles * MOE_TM, d), F32),
        compiler_params=_params("arbitrary"),
        name="moe_ffn",
    )(tile_group, n_valid, idx, idx, xc, w_gate, w_up, w_down)


LN2_TM = 512


LN2_CHUNK = 64


def _ln2_kernel(idx_cur_ref, idx_nxt_ref, f_hbm, x_ref, g_ref, b_ref, *rest, n_tiles, seg_ends):
    i = pl.program_id(0)
    if seg_ends is None:
        y_ref, yb_ref, fbuf_ref, sem_ref = rest
    else:
        seg_refs = rest[:len(seg_ends)]
        fbuf_ref, sem_ref, y_ref = rest[len(seg_ends):]
        yb_ref = None

    @pl.when(i == 0)
    def _():
        _issue_row_gather(idx_cur_ref, f_hbm, fbuf_ref.at[0], sem_ref.at[0], LN2_TM)

    def tile(cur, nxt):
        _wait_row_gather(f_hbm, fbuf_ref.at[cur], sem_ref.at[cur], LN2_TM)
        g = g_ref[...]
        b = b_ref[...]
        for c in range(LN2_TM // LN2_CHUNK):
            rows = pl.ds(c * LN2_CHUNK, LN2_CHUNK)
            for r in range(c * LN2_CHUNK, (c + 1) * LN2_CHUNK):
                pltpu.make_async_copy(f_hbm.at[pl.ds(idx_nxt_ref[0, r], 1), :],
                                      fbuf_ref.at[nxt, pl.ds(r, 1), :], sem_ref.at[nxt]).start()
            y = _layer_norm_rows(DEEPNORM_ALPHA * x_ref[rows, :] + fbuf_ref[cur, rows, :], g, b)
            y_ref[rows, :] = y
            if yb_ref is not None:
                yb_ref[rows, :] = y.astype(BF16)

    @pl.when(i % 2 == 0)
    def _():
        tile(0, 1)

    @pl.when(i % 2 == 1)
    def _():
        tile(1, 0)

    if seg_ends is not None:
        start = 0
        for ref, end in zip(seg_refs, seg_ends):
            @pl.when((i >= start) & (i < end))
            def _(ref=ref):
                ref[...] = y_ref[...]
            start = end

    @pl.when(i == n_tiles - 1)
    def _():
        spare = n_tiles % 2
        _wait_row_gather(f_hbm, fbuf_ref.at[spare], sem_ref.at[spare], LN2_TM)


def _ln2_gather(f_sorted, pos, x, g, b, seg_rows=None):
    t = x.shape[0]
    d = f_sorted.shape[1]
    n_tiles = t // LN2_TM
    idx = pos.reshape(n_tiles, 1, LN2_TM)
    row = pl.BlockSpec((LN2_TM, d), lambda i: (i, 0))
    vec = pl.BlockSpec((1, d), lambda i: (0, 0))
    scratch = [pltpu.VMEM((2, LN2_TM, d), F32), pltpu.SemaphoreType.DMA((2,))]
    if seg_rows is None:
        seg_ends = None
        out_specs = [row, row]
        out_shape = [jax.ShapeDtypeStruct((t, d), F32), jax.ShapeDtypeStruct((t, d), BF16)]
    else:
        seg_tiles = [r // LN2_TM for r in seg_rows]
        seg_ends = tuple(int(v) for v in np.cumsum(seg_tiles))

        def seg_spec(s):
            first, count = seg_ends[s] - seg_tiles[s], seg_tiles[s]
            return pl.BlockSpec((LN2_TM, d), lambda i: (jnp.clip(i - first, 0, count - 1), 0))

        out_specs = [seg_spec(s) for s in range(len(seg_rows))]
        out_shape = [jax.ShapeDtypeStruct((r, d), F32) for r in seg_rows]
        scratch.append(pltpu.VMEM((LN2_TM, d), F32))
    return pl.pallas_call(
        functools.partial(_ln2_kernel, n_tiles=n_tiles, seg_ends=seg_ends),
        grid=(n_tiles,),
        in_specs=[
            pl.BlockSpec((None, 1, LN2_TM), lambda i: (i, 0, 0), memory_space=pltpu.SMEM),
            pl.BlockSpec((None, 1, LN2_TM), lambda i: (jnp.minimum(i + 1, n_tiles - 1), 0, 0),
                         memory_space=pltpu.SMEM),
            pl.BlockSpec(memory_space=pl.ANY),
            row, vec, vec,
        ],
        out_specs=out_specs,
        out_shape=out_shape,
        scratch_shapes=scratch,
        compiler_params=_params("arbitrary"),
        name="ln2_gather",
    )(idx, idx, f_sorted, x, g, b)


def _da_weight_layout(w_in):
    d = w_in.shape[0]
    w = DA_QK_WIDTH
    q1, q2, k1, k2 = (w_in[:, i * w:(i + 1) * w].reshape(d, DA_HEADS, DA_HEAD_DIM) for i in range(4))
    qk = jnp.concatenate([jnp.concatenate([q1, q2], axis=2).reshape(d, 2 * w),
                          jnp.concatenate([k1, k2], axis=2).reshape(d, 2 * w)], axis=1)
    return qk, w_in[:, 4 * w:]


def _trunk(xs, da_w_in, da_w_out, da_lambda_q1, da_lambda_k1, da_lambda_q2, da_lambda_k2, da_subln_g,
           dl_w_in, dl_w_out, ln1_g, ln1_b, ln2_g, ln2_b,
           moe_router_group, moe_router_expert, moe_w_gate, moe_w_up, moe_w_down):
    d = xs[0].shape[-1]
    segs = []
    off = 0
    for x in xs:
        segs.append((off, x.shape[0], x.shape[1]))
        off += x.shape[0] * x.shape[1]
    t_total = off
    x = jnp.concatenate([v.reshape(-1, d) for v in xs], axis=0)
    xb = x.astype(BF16)

    max_len = max(s for _, _, s in segs)
    rope = _rope_tables(max_len)
    pos_blocks = np.concatenate([
        np.tile(np.arange(s // PROJ_TM, dtype=np.int32), b) for _, b, s in segs])
    pos_blocks = jnp.asarray(pos_blocks)

    for i in range(DEPTH):
        j = i // N_MIXERS
        if i % N_MIXERS == 0:
            lambda_init = 0.8 - 0.6 * math.exp(-0.3 * i)
            w_qk, w_v = _da_weight_layout(da_w_in[j])
            qk = _proj(xb, w_qk.astype(BF16), rope, pos_blocks,
                       q_cols=2 * DA_QK_WIDTH, rope_cols=4 * DA_QK_WIDTH, q_scale=QK_SCALE_LOG2)
            vt = _proj_t(xb, w_v.T.astype(BF16))
            lam_params = jnp.stack([da_lambda_q1[j], da_lambda_k1[j], da_lambda_q2[j], da_lambda_k2[j]]).astype(F32)
            g_col = da_subln_g[j].astype(F32).reshape(DA_V_DIM, 1)
            os_ = [[_da_attention(qk, vt, lam_params, g_col, lambda_init, ro, b, s) for ro, b, s in segs]]
            lses = []
            w_out = da_w_out[j]
        else:
            os_, lses = [], []
            gw = 3 * DIL_GROUP_WIDTH
            for g, (window, dilation) in enumerate(DIL_PATTERNS):
                assert window // (2 * dilation) == DIL_HALF
                proj = _proj(xb, dl_w_in[j][:, g * gw:(g + 1) * gw].astype(BF16), rope, pos_blocks,
                             q_cols=DIL_GROUP_WIDTH, rope_cols=2 * DIL_GROUP_WIDTH, q_scale=QK_SCALE_LOG2,
                             dilation=dilation)
                pairs = [_dil_attention_group(proj, g, dilation, ro, b, s) for ro, b, s in segs]
                os_.append([p[0] for p in pairs])
                lses.append([p[1] for p in pairs])
            w_out = dl_w_out[j]
        w_router = jnp.concatenate([
            moe_router_group[i].astype(F32),
            moe_router_expert[i].reshape(d, N_EXPERTS).astype(F32),
            jnp.zeros((d, LANES - MOE_GROUPS - N_EXPERTS), F32)], axis=1)
        xc = _post_attention(os_, lses, [b * s for _, b, s in segs], w_out.astype(BF16), x,
                             ln1_g[i].reshape(1, d).astype(F32), ln1_b[i].reshape(1, d).astype(F32), w_router)
        gid = xc[:, d + ROUTER_GROUP_LANE].astype(jnp.int32)
        pos, src, tile_group, n_valid = _sort_by_group(gid, t_total, t_total + MOE_GROUPS * MOE_TM)
        f_sorted = _moe_ffn(xc, src, tile_group, n_valid,
                            moe_w_gate[i].astype(BF16), moe_w_up[i].astype(BF16), moe_w_down[i].astype(BF16))
        ln2 = (ln2_g[i].reshape(1, d).astype(F32), ln2_b[i].reshape(1, d).astype(F32))
        if i + 1 < DEPTH:
            x, xb = _ln2_gather(f_sorted, pos, xc, *ln2)
        else:
            ys = _ln2_gather(f_sorted, pos, xc, *ln2, seg_rows=[b * s for _, b, s in segs])
    return [y.reshape(v.shape) for y, v in zip(ys, xs)]


def kernel(x_prompt, x_sample, da_w_in, da_w_out, da_lambda_q1, da_lambda_k1, da_lambda_q2, da_lambda_k2, da_subln_g, dl_w_in, dl_w_out, ln1_g, ln1_b, ln2_g, ln2_b, moe_router_group, moe_router_expert, moe_w_gate, moe_w_up, moe_w_down):
    y_prompt, y_sample = _trunk(
        [x_prompt, x_sample], da_w_in, da_w_out, da_lambda_q1, da_lambda_k1, da_lambda_q2, da_lambda_k2,
        da_subln_g, dl_w_in, dl_w_out, ln1_g, ln1_b, ln2_g, ln2_b,
        moe_router_group, moe_router_expert, moe_w_gate, moe_w_up, moe_w_down)
    return (y_prompt, y_sample)
```

```python
import functools
import math

import jax
import jax.numpy as jnp
import numpy as np
from jax import lax
from jax.experimental import pallas as pl
from jax.experimental.pallas import tpu as pltpu

F32 = jnp.float32
BF16 = jnp.bfloat16

LANES = 128
VMEM_LIMIT_BYTES = 56 * 1024 * 1024

D_MODEL = 1024
DEPTH = 4
N_MIXERS = 2

DA_HEADS = 8
DA_HEAD_DIM = 64
DA_V_DIM = 2 * DA_HEAD_DIM
DA_QK_WIDTH = DA_HEADS * DA_HEAD_DIM

DIL_PATTERNS = ((128, 1), (512, 4), (2048, 16))
DIL_GROUPS = len(DIL_PATTERNS)
DIL_HEADS = 16
DIL_HEAD_DIM = 64
DIL_GROUP_WIDTH = DIL_HEADS * DIL_HEAD_DIM
DIL_HALF = 64

ROPE_THETA = 500000.0
ROPE_ROT = DA_HEAD_DIM // 4
ROPE_HALF = ROPE_ROT // 2

LN_EPS = 1e-5
DEEPNORM_ALPHA = (2.0 * DEPTH) ** 0.25

MOE_GROUPS = 4
MOE_EXPERTS = 4
MOE_FF = 512
N_EXPERTS = MOE_GROUPS * MOE_EXPERTS

NEG_INF = -1e30
LOG2_E = math.log2(math.e)
QK_SCALE_LOG2 = DA_HEAD_DIM ** -0.5 * LOG2_E


def _params(*sem):
    return pltpu.CompilerParams(dimension_semantics=sem, vmem_limit_bytes=VMEM_LIMIT_BYTES)


PROJ_TM = 1024
PROJ_TN = 1024


def _rope_tables(max_len):
    inv = ROPE_THETA ** (-jnp.arange(ROPE_HALF, dtype=F32) * (2.0 / ROPE_ROT))
    ang = jnp.arange(max_len, dtype=F32)[:, None] * inv[None, :]
    cos, sin = jnp.cos(ang), jnp.sin(ang)
    ones = jnp.ones((max_len, DA_HEAD_DIM - ROPE_ROT), F32)
    zeros_h = jnp.zeros((max_len, ROPE_HALF), F32)
    zeros_r = jnp.zeros((max_len, DA_HEAD_DIM - ROPE_ROT), F32)
    c = jnp.concatenate([cos, cos, ones], axis=1)
    s_up = jnp.concatenate([zeros_h, sin, zeros_r], axis=1)
    s_dn = jnp.concatenate([-sin, zeros_h, zeros_r], axis=1)
    reps = LANES // DA_HEAD_DIM
    return jnp.stack([jnp.tile(c, (1, reps)), jnp.tile(s_up, (1, reps)), jnp.tile(s_dn, (1, reps))])


PROJ_MAX_STRIDE = 4
PROJ_CHUNK = 256


def _proj_kernel(pos_ref, x_ref, w_ref, rope_ref, o_ref, ybuf_ref, zbuf_ref, *, n_cols, q_cols, rope_cols, q_scale, dilation):
    del pos_ref
    x = x_ref[...]
    c = rope_ref[0]
    s_up = rope_ref[1]
    s_dn = rope_ref[2]
    rows = x.shape[0] // dilation
    for k in range(n_cols // PROJ_CHUNK):
        c0 = k * PROJ_CHUNK
        y = jnp.dot(x, w_ref[:, c0:c0 + PROJ_CHUNK], preferred_element_type=F32)
        if c0 < q_cols:
            y = y * q_scale
        parts = []
        for h in range(PROJ_CHUNK // LANES):
            yc = y[:, h * LANES:(h + 1) * LANES]
            if c0 < rope_cols:
                yc = yc * c + pltpu.roll(yc, ROPE_HALF, 1) * s_up + pltpu.roll(yc, LANES - ROPE_HALF, 1) * s_dn
            parts.append(yc)
        if dilation == 1:
            for h, yc in enumerate(parts):
                o_ref[0, :, c0 + h * LANES:c0 + (h + 1) * LANES] = yc.astype(o_ref.dtype)
        else:
            for h, yc in enumerate(parts):
                slot = (2 * k + h) % ybuf_ref.shape[0]
                cols = slice(c0 + h * LANES, c0 + (h + 1) * LANES)
                ybuf_ref[slot] = yc
                if dilation <= PROJ_MAX_STRIDE:
                    for r in range(dilation):
                        o_ref[r, :, cols] = ybuf_ref[slot, pl.ds(r, rows, stride=dilation), :].astype(o_ref.dtype)
                else:
                    outer = dilation // PROJ_MAX_STRIDE
                    zslot = h % zbuf_ref.shape[0]
                    for r in range(PROJ_MAX_STRIDE):
                        zbuf_ref[zslot, r] = ybuf_ref[slot, pl.ds(r, rows * outer, stride=PROJ_MAX_STRIDE), :]
                        for a in range(outer):
                            o_ref[r + PROJ_MAX_STRIDE * a, :, cols] = (
                                zbuf_ref[zslot, r, pl.ds(a, rows, stride=outer), :].astype(o_ref.dtype))


def _proj(xb, w, rope, pos_blocks, *, q_cols, rope_cols, q_scale, dilation=1):
    t, d = xb.shape
    n = w.shape[1]
    rows = PROJ_TM // dilation
    grid_spec = pltpu.PrefetchScalarGridSpec(
        num_scalar_prefetch=1,
        grid=(t // PROJ_TM,),
        in_specs=[
            pl.BlockSpec((PROJ_TM, d), lambda i, pos: (i, 0)),
            pl.BlockSpec((d, n), lambda i, pos: (0, 0)),
            pl.BlockSpec((3, PROJ_TM, LANES), lambda i, pos: (0, pos[i], 0)),
        ],
        out_specs=pl.BlockSpec((dilation, rows, n), lambda i, pos: (0, i, 0)),
        scratch_shapes=[pltpu.VMEM((4, PROJ_TM, LANES), F32),
                        pltpu.VMEM((2, PROJ_MAX_STRIDE, PROJ_TM // PROJ_MAX_STRIDE, LANES), F32)],
    )
    return pl.pallas_call(
        functools.partial(_proj_kernel, n_cols=n, q_cols=q_cols, rope_cols=rope_cols, q_scale=q_scale,
                          dilation=dilation),
        grid_spec=grid_spec,
        out_shape=jax.ShapeDtypeStruct((dilation, t // dilation, n), BF16),
        compiler_params=_params("parallel"),
        name="proj_rope",
    )(pos_blocks, xb, w, rope)


DA_TK = 512
DA_ONES_ROWS = 16
DA_VT_ROWS = DA_V_DIM + DA_ONES_ROWS


def _projt_kernel(x_ref, wt_ref, o_ref):
    y = lax.dot_general(wt_ref[...], x_ref[...], (((1,), (1,)), ((), ())), preferred_element_type=F32)
    h = o_ref.shape[0]
    o_ref[:, 0, 0:DA_V_DIM, :] = y.reshape(h, DA_V_DIM, DA_TK).astype(o_ref.dtype)
    o_ref[:, 0, DA_V_DIM:DA_VT_ROWS, :] = jnp.ones((h, DA_ONES_ROWS, DA_TK), o_ref.dtype)


def _proj_t(xb, wt):
    t, d = xb.shape
    n = wt.shape[0]
    h = n // DA_V_DIM
    return pl.pallas_call(
        _projt_kernel,
        grid=(t // DA_TK,),
        in_specs=[
            pl.BlockSpec((DA_TK, d), lambda i: (i, 0)),
            pl.BlockSpec((n, d), lambda i: (0, 0)),
        ],
        out_specs=pl.BlockSpec((h, 1, DA_VT_ROWS, DA_TK), lambda i: (0, i, 0, 0)),
        out_shape=jax.ShapeDtypeStruct((h, t // DA_TK, DA_VT_ROWS, DA_TK), BF16),
        compiler_params=_params("parallel"),
        name="proj_vt",
    )(xb, wt)


DA_TQ = 512
DA_PAIR_UNROLL = 3


def _da_attn_kernel(lam_ref, g_ref, q_ref, k_ref, vt_ref, o_ref, q1_ref, q2_ref,
                    sa1_ref, sa2_ref, sb1_ref, sb2_ref, xa1_ref, xa2_ref, xb1_ref, xb2_ref,
                    m1_ref, a1_ref, m2_ref, a2_ref, *, n_kblk, lambda_init):
    buf_a = ((sa1_ref, xa1_ref), (sa2_ref, xa2_ref))
    buf_b = ((sb1_ref, xb1_ref), (sb2_ref, xb2_ref))
    qt = q_ref[...].astype(F32).T
    row = lax.broadcasted_iota(jnp.int32, qt.shape, 0)
    q1_ref[...] = jnp.where(row < DA_HEAD_DIM, qt, 0.0).astype(q1_ref.dtype)
    q2_ref[...] = jnp.where(row >= DA_HEAD_DIM, qt, 0.0).astype(q2_ref.dtype)
    for m_ref, a_ref in ((m1_ref, a1_ref), (m2_ref, a2_ref)):
        m_ref[...] = jnp.full(m_ref.shape, NEG_INF, F32)
        a_ref[...] = jnp.zeros(a_ref.shape, F32)

    def scores(j, buf):
        k = k_ref[pl.ds(pl.multiple_of(j * DA_TK, DA_TK), DA_TK), :]
        for qm_ref, (s_ref, x_ref) in zip((q1_ref, q2_ref), buf):
            s = jnp.dot(k, qm_ref[...], preferred_element_type=F32)
            s_ref[...] = s
            x_ref[...] = jnp.max(s.reshape(DA_TK // 8, 8, DA_TQ), axis=0)

    def softmax_pv(j, buf):
        vt = vt_ref[0, j]
        for (s_ref, x_ref), m_ref, a_ref in zip(buf, (m1_ref, m2_ref), (a1_ref, a2_ref)):
            m_old = m_ref[...]
            m_new = jnp.maximum(m_old, jnp.max(x_ref[...], axis=0, keepdims=True))
            p = jnp.exp2(s_ref[...] - m_new).astype(vt.dtype)
            a_ref[...] = jnp.exp2(m_old - m_new) * a_ref[...] + jnp.dot(vt, p, preferred_element_type=F32)
            m_ref[...] = m_new

    scores(0, buf_a)

    def pair(jj, carry):
        j = 2 * jj
        scores(j + 1, buf_b)
        softmax_pv(j, buf_a)
        scores(j + 2, buf_a)
        softmax_pv(j + 1, buf_b)
        return carry

    n_pairs = n_kblk // 2 - 1
    lax.fori_loop(0, n_pairs, pair, 0, unroll=DA_PAIR_UNROLL if n_pairs > DA_PAIR_UNROLL else 2)
    scores(n_kblk - 1, buf_b)
    softmax_pv(n_kblk - 2, buf_a)
    softmax_pv(n_kblk - 1, buf_b)

    lp = lam_ref[...]
    lam = (jnp.exp(jnp.sum(lp[0:1] * lp[1:2], axis=1, keepdims=True))
           - jnp.exp(jnp.sum(lp[2:3] * lp[3:4], axis=1, keepdims=True)) + lambda_init)
    a1 = a1_ref[...]
    a2 = a2_ref[...]
    o = (a1[0:DA_V_DIM] * (1.0 / a1[DA_V_DIM:DA_V_DIM + 1])
         - lam * (a2[0:DA_V_DIM] * (1.0 / a2[DA_V_DIM:DA_V_DIM + 1])))
    o = o * lax.rsqrt(jnp.mean(jnp.square(o), axis=0, keepdims=True) + LN_EPS)
    o = o * g_ref[...] * (1.0 - lambda_init)
    o_ref[...] = o.T.astype(o_ref.dtype)


def _da_attention(qk, vt, lam_params, subln_g, lambda_init, row_off, batch, seq):
    h = DA_HEADS
    nq = seq // DA_TQ
    nk = seq // DA_TK
    assert nk % 2 == 0 and nk >= 2
    qoff = row_off // DA_TQ
    soff = row_off // seq
    s_buf = pltpu.VMEM((DA_TK, DA_TQ), F32)
    x_buf = pltpu.VMEM((8, DA_TQ), F32)
    in_specs = [
        pl.BlockSpec((4, DA_HEAD_DIM), lambda b, hh, i: (0, 0)),
        pl.BlockSpec((DA_V_DIM, 1), lambda b, hh, i: (0, 0)),
        pl.BlockSpec((None, DA_TQ, LANES), lambda b, hh, i: (0, qoff + b * nq + i, hh)),
        pl.BlockSpec((None, seq, LANES), lambda b, hh, i: (0, soff + b, h + hh)),
        pl.BlockSpec((1, nk, DA_VT_ROWS, DA_TK), lambda b, hh, i: (hh, soff + b, 0, 0)),
    ]
    return pl.pallas_call(
        functools.partial(_da_attn_kernel, n_kblk=nk, lambda_init=lambda_init),
        grid=(batch, h, nq),
        in_specs=in_specs,
        out_specs=pl.BlockSpec((None, DA_TQ, DA_V_DIM), lambda b, hh, i: (0, b * nq + i, hh)),
        out_shape=jax.ShapeDtypeStruct((1, batch * seq, h * DA_V_DIM), BF16),
        scratch_shapes=[
            pltpu.VMEM((LANES, DA_TQ), BF16), pltpu.VMEM((LANES, DA_TQ), BF16),
            s_buf, s_buf, s_buf, s_buf,
            x_buf, x_buf, x_buf, x_buf,
            pltpu.VMEM((1, DA_TQ), F32), pltpu.VMEM((DA_VT_ROWS, DA_TQ), F32),
            pltpu.VMEM((1, DA_TQ), F32), pltpu.VMEM((DA_VT_ROWS, DA_TQ), F32),
        ],
        compiler_params=_params("parallel", "parallel", "arbitrary"),
        name="da_attn",
    )(lam_params, subln_g, qk, qk, vt)


DIL_QSUB = 128
DIL_KWIN = DIL_QSUB + 2 * DIL_HALF
DIL_QBLK = 1024
DIL_SUBS_PER_STEP = 32


def _dil_attn_kernel(q_ref, k_ref, v_ref, o_ref, lse_ref, *, sub_len, qblk):
    qi = pl.program_id(3)
    lane = lax.broadcasted_iota(jnp.int32, (DIL_QSUB, LANES), 1)
    lo = lane < DIL_HEAD_DIM
    row = lax.broadcasted_iota(jnp.int32, (DIL_QSUB, DIL_KWIN), 0)
    col = lax.broadcasted_iota(jnp.int32, (DIL_QSUB, DIL_KWIN), 1)

    interior = jnp.abs(col - DIL_HALF - row) <= DIL_HALF
    n_sub = qblk // DIL_QSUB

    for pair_idx, n in [(a, b) for a in range(q_ref.shape[1] // LANES) for b in range(n_sub)]:
        cols = slice(pair_idx * LANES, (pair_idx + 1) * LANES)
        q0 = n * DIL_QSUB
        gq0 = qi * qblk + q0
        ws = jnp.clip(gq0 - DIL_HALF, 0, sub_len - DIL_KWIN)
        ws = pl.multiple_of(ws, DIL_HALF)
        qb = q_ref[pl.ds(q0, DIL_QSUB), cols]
        kw = k_ref[pl.ds(ws, DIL_KWIN), cols]
        vw = v_ref[pl.ds(ws, DIL_KWIN), cols]
        if 0 < n < n_sub - 1:
            valid = interior
        else:
            valid = jnp.abs((ws + col) - (gq0 + row)) <= DIL_HALF
        zero = jnp.zeros_like(qb)
        outs, lses = [], []
        for qm in (jnp.where(lo, qb, zero), jnp.where(lo, zero, qb)):
            s = lax.dot_general(qm, kw, (((1,), (1,)), ((), ())), preferred_element_type=F32)
            s = jnp.where(valid, s, NEG_INF)
            m = jnp.max(s, axis=1, keepdims=True)
            p = jnp.exp2(s - m)
            l = jnp.sum(p, axis=1, keepdims=True)
            o = jnp.dot(p.astype(vw.dtype), vw, preferred_element_type=F32) * (1.0 / l)
            outs.append(o)
            lses.append(jnp.broadcast_to(m + jnp.log(l) * LOG2_E, (DIL_QSUB, LANES)))
        o_ref[pl.ds(q0, DIL_QSUB), cols] = jnp.where(lo, outs[0], outs[1]).astype(o_ref.dtype)
        lse_ref[pl.ds(q0, DIL_QSUB), cols] = jnp.where(lo, lses[0], lses[1])


def _dil_attention_group(proj, g, dilation, row_off, batch, seq):
    sub_len = seq // dilation
    qblk = min(DIL_QBLK, sub_len)
    nqb = sub_len // qblk
    n_pairs = DIL_GROUP_WIDTH // LANES
    per_step = min(n_pairs, max(1, DIL_SUBS_PER_STEP // (qblk // DIL_QSUB)))
    hp = n_pairs // per_step
    width = per_step * LANES
    soff = row_off // seq

    def qmap(b, r, p, i):
        return (r, (soff + b) * nqb + i, p)

    def kmap(b, r, p, i):
        return (r, soff + b, hp + p)

    def vmap(b, r, p, i):
        return (r, soff + b, 2 * hp + p)

    def omap(b, r, p, i):
        return (r, b * nqb + i, p)

    return pl.pallas_call(
        functools.partial(_dil_attn_kernel, sub_len=sub_len, qblk=qblk),
        grid=(batch, dilation, hp, nqb),
        in_specs=[
            pl.BlockSpec((None, qblk, width), qmap),
            pl.BlockSpec((None, sub_len, width), kmap),
            pl.BlockSpec((None, sub_len, width), vmap),
        ],
        out_specs=[
            pl.BlockSpec((None, qblk, width), omap),
            pl.BlockSpec((None, qblk, width), omap),
        ],
        out_shape=[
            jax.ShapeDtypeStruct((dilation, batch * sub_len, DIL_GROUP_WIDTH), BF16),
            jax.ShapeDtypeStruct((dilation, batch * sub_len, DIL_GROUP_WIDTH), F32),
        ],
        compiler_params=_params("parallel", "parallel", "parallel", "arbitrary"),
        name=f"dil_attn_g{g}",
    )(proj, proj, proj)


POST_TM = 256


def _layer_norm_rows(z, g, b):
    mu = jnp.mean(z, axis=-1, keepdims=True)
    zc = z - mu
    var = jnp.mean(jnp.square(zc), axis=-1, keepdims=True)
    return zc * lax.rsqrt(var + LN_EPS) * g + b


def _token_order_reader(seg_refs, seg_ends, scratch_ref):
    i = pl.program_id(0)
    dil, rows, width = seg_refs[0].shape
    if dil == 1:
        def read(c):
            cs = slice(c * LANES, (c + 1) * LANES)
            val = seg_refs[-1][0, :, cs]
            for ref, end in reversed(list(zip(seg_refs[:-1], seg_ends[:-1]))):
                val = jnp.where(i < end, ref[0, :, cs], val)
            return val.astype(F32)
        return read
    start = 0
    for ref, end in zip(seg_refs, seg_ends):
        @pl.when((i >= start) & (i < end))
        def _(ref=ref):
            for r in range(dil):
                for c in range(width // LANES):
                    scratch_ref[c, pl.ds(r, rows, stride=dil), :] = (
                        ref[r, :, c * LANES:(c + 1) * LANES].astype(F32))
        start = end
    return lambda c: scratch_ref[c]


def _post_kernel(*refs, n_groups, seg_ends):
    n_seg = len(seg_ends)
    o_refs = [refs[g * n_seg:(g + 1) * n_seg] for g in range(n_groups)]
    n_in = n_groups * n_seg
    lse_refs = []
    if n_groups > 1:
        lse_refs = [refs[n_in + g * n_seg:n_in + (g + 1) * n_seg] for g in range(n_groups)]
        n_in *= 2
    w_ref, x_ref, g_ref, b_ref, wr_ref, xc_ref = refs[n_in:n_in + 6]
    scratch = refs[n_in + 6:]
    d = x_ref.shape[1]
    o_scr = scratch[-1]
    spare = iter(scratch[:-1])
    readers = [_token_order_reader(r, seg_ends, next(spare) if r[0].shape[0] > 1 else None)
               for r in (*o_refs, *lse_refs)]
    read_o, read_lse = readers[:n_groups], readers[n_groups:]
    for c in range(d // LANES):
        if n_groups == 1:
            o_c = read_o[0](c)
        else:
            lses = [rd(c) for rd in read_lse]
            m = functools.reduce(jnp.maximum, lses)
            es = [jnp.exp2(l - m) for l in lses]
            inv = 1.0 / functools.reduce(jnp.add, es)
            o_c = functools.reduce(jnp.add, [(e * inv) * rd(c) for e, rd in zip(es, read_o)])
        o_scr[:, c * LANES:(c + 1) * LANES] = o_c.astype(BF16)
    h = jnp.dot(o_scr[...], w_ref[...], preferred_element_type=F32)
    y = _layer_norm_rows(DEEPNORM_ALPHA * x_ref[...] + h, g_ref[...], b_ref[...])
    xc_ref[:, 0:d] = y
    xc_ref[:, d:d + LANES] = _routing_lanes(y, wr_ref[...])


def _post_attention(os_, lses, seg_rows, w_out, x, g, b, w_router):
    t, d = x.shape
    n_groups = len(os_)
    tm = POST_TM
    seg_tiles = [r // tm for r in seg_rows]
    seg_ends = tuple(int(v) for v in np.cumsum(seg_tiles))
    row = pl.BlockSpec((tm, d), lambda i: (i, 0))
    full = pl.BlockSpec((d, d), lambda i: (0, 0))
    vec = pl.BlockSpec((1, d), lambda i: (0, 0))

    def seg_spec(a, s):
        first, count = seg_ends[s] - seg_tiles[s], seg_tiles[s]
        return pl.BlockSpec((a.shape[0], tm // a.shape[0], d),
                            lambda i: (0, jnp.clip(i - first, 0, count - 1), 0))

    arrays = [a for grp in (*os_, *lses) for a in grp]
    specs = [seg_spec(a, s) for grp in (*os_, *lses) for s, a in enumerate(grp)]
    n_scratch = sum(1 for grp in (*os_, *lses) if grp[0].shape[0] > 1)
    return pl.pallas_call(
        functools.partial(_post_kernel, n_groups=n_groups, seg_ends=seg_ends),
        grid=(t // tm,),
        in_specs=specs + [full, row, vec, vec, pl.BlockSpec((d, LANES), lambda i: (0, 0))],
        out_specs=pl.BlockSpec((tm, d + LANES), lambda i: (i, 0)),
        out_shape=jax.ShapeDtypeStruct((t, d + LANES), F32),
        scratch_shapes=[pltpu.VMEM((d // LANES, tm, LANES), F32)] * n_scratch + [pltpu.VMEM((tm, d), BF16)],
        compiler_params=_params("parallel"),
        name="post_attn_ln",
    )(*arrays, w_out, x, g, b, w_router)


ROUTER_TM = 512
ROUTER_EXPERT_LANE0 = MOE_GROUPS
ROUTER_GROUP_LANE = 0


def _first_argmax(vals, vmax, lane_f):
    return jnp.min(jnp.where(vals == vmax, lane_f, float(LANES)), axis=1, keepdims=True)


def _routing_lanes(x, w_router):
    xh = x.astype(BF16)
    xl = (x - xh.astype(F32)).astype(BF16)
    wh = w_router.astype(BF16)
    wl = (w_router - wh.astype(F32)).astype(BF16)
    logits = (jnp.dot(xh, wh, preferred_element_type=F32) + jnp.dot(xh, wl, preferred_element_type=F32)
              + jnp.dot(xl, wh, preferred_element_type=F32))
    lane_f = lax.broadcasted_iota(jnp.int32, logits.shape, 1).astype(F32)
    gl = jnp.where(lane_f < MOE_GROUPS, logits, NEG_INF)
    gmax = jnp.max(gl, axis=1, keepdims=True)
    gw = 1.0 / jnp.sum(jnp.exp(gl - gmax), axis=1, keepdims=True)
    gsel = _first_argmax(gl, gmax, lane_f)
    e0 = ROUTER_EXPERT_LANE0 + MOE_EXPERTS * gsel
    el = jnp.where((lane_f >= e0) & (lane_f < e0 + MOE_EXPERTS), logits, NEG_INF)
    v1 = jnp.max(el, axis=1, keepdims=True)
    i1 = _first_argmax(el, v1, lane_f)
    el2 = jnp.where(lane_f == i1, NEG_INF, el)
    v2 = jnp.max(el2, axis=1, keepdims=True)
    i2 = _first_argmax(el2, v2, lane_f)
    e2 = jnp.exp(v2 - v1)
    inv = gw / (1.0 + e2)
    comb = jnp.where(lane_f == i1, inv, 0.0) + jnp.where(lane_f == i2, e2 * inv, 0.0)
    return comb + jnp.where(lane_f == ROUTER_GROUP_LANE, gsel, 0.0)


MOE_TM = 512
GATHER_UNROLL = 8


def _issue_row_gather(idx_ref, src_hbm, dst_ref, sem, n_rows):
    def body(r, carry):
        pltpu.make_async_copy(src_hbm.at[pl.ds(idx_ref[0, r], 1), :], dst_ref.at[pl.ds(r, 1), :], sem).start()
        return carry
    lax.fori_loop(0, n_rows, body, 0, unroll=GATHER_UNROLL)


def _wait_row_gather(src_hbm, dst_ref, sem, n_rows):
    pltpu.make_async_copy(src_hbm.at[pl.ds(0, n_rows), :], dst_ref, sem).wait()


def _sort_by_group(gid, t, n_pad_rows):
    groups = jnp.arange(MOE_GROUPS, dtype=jnp.int32)
    onehot = (gid[:, None] == groups[None, :]).astype(jnp.int32)
    csum = jnp.cumsum(onehot, axis=0)
    counts = csum[-1]
    rank = jnp.sum(onehot * csum, axis=1) - 1
    tiles_g = (counts + MOE_TM - 1) // MOE_TM
    tile_end = jnp.cumsum(tiles_g)
    row_start = (tile_end - tiles_g) * MOE_TM
    pos = jnp.sum(onehot * row_start[None, :], axis=1) + rank
    src = jnp.zeros((n_pad_rows,), jnp.int32).at[pos].set(jnp.arange(t, dtype=jnp.int32))
    tile_ids = jnp.arange(n_pad_rows // MOE_TM, dtype=jnp.int32)
    tile_group = jnp.minimum(jnp.sum((tile_ids[:, None] >= tile_end[None, :]).astype(jnp.int32), axis=1),
                             MOE_GROUPS - 1)
    return pos, src, tile_group, tile_end[-1:]


def _moe_ffn_kernel(tg_ref, nv_ref, idx_cur_ref, idx_nxt_ref, xc_hbm, wg_ref, wu_ref, wd_ref, f_ref,
                    xbuf_ref, sem_ref):
    i = pl.program_id(0)
    slot = i % 2
    n_valid = nv_ref[0]
    d = f_ref.shape[1]

    @pl.when(i == 0)
    def _():
        _issue_row_gather(idx_cur_ref, xc_hbm, xbuf_ref.at[0], sem_ref.at[0], MOE_TM)

    @pl.when(i + 1 < n_valid)
    def _():
        _issue_row_gather(idx_nxt_ref, xc_hbm, xbuf_ref.at[1 - slot], sem_ref.at[1 - slot], MOE_TM)

    @pl.when(i < n_valid)
    def _():
        _wait_row_gather(xc_hbm, xbuf_ref.at[slot], sem_ref.at[slot], MOE_TM)
        xb = xbuf_ref[slot, :, 0:d].astype(BF16)
        comb = xbuf_ref[slot, :, d:d + LANES]
        lane = lax.broadcasted_iota(jnp.int32, comb.shape, 1)
        lane0 = ROUTER_EXPERT_LANE0 + MOE_EXPERTS * tg_ref[i]
        acc = None
        for e in range(MOE_EXPERTS):
            ce = jnp.sum(jnp.where(lane == lane0 + e, comb, 0.0), axis=1, keepdims=True)
            gate = jnp.dot(xb, wg_ref[0, e], preferred_element_type=F32)
            up = jnp.dot(xb, wu_ref[0, e], preferred_element_type=F32)
            hid = (gate * jax.nn.sigmoid(gate)) * up * ce
            part = jnp.dot(hid.astype(BF16), wd_ref[0, e], preferred_element_type=F32)
            acc = part if acc is None else acc + part
        f_ref[...] = acc

    @pl.when(i >= n_valid)
    def _():
        f_ref[...] = jnp.zeros(f_ref.shape, F32)


def _moe_ffn(xc, src, tile_group, n_valid, w_gate, w_up, w_down):
    d = xc.shape[1] - LANES
    n_tiles = tile_group.shape[0]
    _, ne, _, f = w_gate.shape
    idx = src.reshape(n_tiles, 1, MOE_TM)
    grid_spec = pltpu.PrefetchScalarGridSpec(
        num_scalar_prefetch=2,
        grid=(n_tiles,),
        in_specs=[
            pl.BlockSpec((None, 1, MOE_TM), lambda i, tg, nv: (i, 0, 0), memory_space=pltpu.SMEM),
            pl.BlockSpec((None, 1, MOE_TM), lambda i, tg, nv: (jnp.minimum(i + 1, n_tiles - 1), 0, 0),
                         memory_space=pltpu.SMEM),
            pl.BlockSpec(memory_space=pl.ANY),
            pl.BlockSpec((1, ne, d, f), lambda i, tg, nv: (tg[i], 0, 0, 0)),
            pl.BlockSpec((1, ne, d, f), lambda i, tg, nv: (tg[i], 0, 0, 0)),
            pl.BlockSpec((1, ne, f, d), lambda i, tg, nv: (tg[i], 0, 0, 0)),
        ],
        out_specs=pl.BlockSpec((MOE_TM, d), lambda i, tg, nv: (i, 0)),
        scratch_shapes=[pltpu.VMEM((2, MOE_TM, d + LANES), F32), pltpu.SemaphoreType.DMA((2,))],
    )
    return pl.pallas_call(
        _moe_ffn_kernel,
        grid_spec=grid_spec,
        out_shape=jax.ShapeDtypeStruct((n_tiles * MOE_TM, d), F32),
        compiler_params=_params("arbitrary"),
        name="moe_ffn",
    )(tile_group, n_valid, idx, idx, xc, w_gate, w_up, w_down)


LN2_TM = 512


LN2_CHUNK = 64


def _ln2_kernel(idx_cur_ref, idx_nxt_ref, f_hbm, x_ref, g_ref, b_ref, *rest, n_tiles, seg_ends):
    i = pl.program_id(0)
    if seg_ends is None:
        y_ref, yb_ref, fbuf_ref, sem_ref = rest
    else:
        seg_refs = rest[:len(seg_ends)]
        fbuf_ref, sem_ref, y_ref = rest[len(seg_ends):]
        yb_ref = None

    @pl.when(i == 0)
    def _():
        _issue_row_gather(idx_cur_ref, f_hbm, fbuf_ref.at[0], sem_ref.at[0], LN2_TM)

    def tile(cur, nxt):
        _wait_row_gather(f_hbm, fbuf_ref.at[cur], sem_ref.at[cur], LN2_TM)
        g = g_ref[...]
        b = b_ref[...]
        for c in range(LN2_TM // LN2_CHUNK):
            rows = pl.ds(c * LN2_CHUNK, LN2_CHUNK)
            for r in range(c * LN2_CHUNK, (c + 1) * LN2_CHUNK):
                pltpu.make_async_copy(f_hbm.at[pl.ds(idx_nxt_ref[0, r], 1), :],
                                      fbuf_ref.at[nxt, pl.ds(r, 1), :], sem_ref.at[nxt]).start()
            y = _layer_norm_rows(DEEPNORM_ALPHA * x_ref[rows, :] + fbuf_ref[cur, rows, :], g, b)
            y_ref[rows, :] = y
            if yb_ref is not None:
                yb_ref[rows, :] = y.astype(BF16)

    @pl.when(i % 2 == 0)
    def _():
        tile(0, 1)

    @pl.when(i % 2 == 1)
    def _():
        tile(1, 0)

    if seg_ends is not None:
        start = 0
        for ref, end in zip(seg_refs, seg_ends):
            @pl.when((i >= start) & (i < end))
            def _(ref=ref):
                ref[...] = y_ref[...]
            start = end

    @pl.when(i == n_tiles - 1)
    def _():
        spare = n_tiles % 2
        _wait_row_gather(f_hbm, fbuf_ref.at[spare], sem_ref.at[spare], LN2_TM)


def _ln2_gather(f_sorted, pos, x, g, b, seg_rows=None):
    t = x.shape[0]
    d = f_sorted.shape[1]
    n_tiles = t // LN2_TM
    idx = pos.reshape(n_tiles, 1, LN2_TM)
    row = pl.BlockSpec((LN2_TM, d), lambda i: (i, 0))
    vec = pl.BlockSpec((1, d), lambda i: (0, 0))
    scratch = [pltpu.VMEM((2, LN2_TM, d), F32), pltpu.SemaphoreType.DMA((2,))]
    if seg_rows is None:
        seg_ends = None
        out_specs = [row, row]
        out_shape = [jax.ShapeDtypeStruct((t, d), F32), jax.ShapeDtypeStruct((t, d), BF16)]
    else:
        seg_tiles = [r // LN2_TM for r in seg_rows]
        seg_ends = tuple(int(v) for v in np.cumsum(seg_tiles))

        def seg_spec(s):
            first, count = seg_ends[s] - seg_tiles[s], seg_tiles[s]
            return pl.BlockSpec((LN2_TM, d), lambda i: (jnp.clip(i - first, 0, count - 1), 0))

        out_specs = [seg_spec(s) for s in range(len(seg_rows))]
        out_shape = [jax.ShapeDtypeStruct((r, d), F32) for r in seg_rows]
        scratch.append(pltpu.VMEM((LN2_TM, d), F32))
    return pl.pallas_call(
        functools.partial(_ln2_kernel, n_tiles=n_tiles, seg_ends=seg_ends),
        grid=(n_tiles,),
        in_specs=[
            pl.BlockSpec((None, 1, LN2_TM), lambda i: (i, 0, 0), memory_space=pltpu.SMEM),
            pl.BlockSpec((None, 1, LN2_TM), lambda i: (jnp.minimum(i + 1, n_tiles - 1), 0, 0),
                         memory_space=pltpu.SMEM),
            pl.BlockSpec(memory_space=pl.ANY),
            row, vec, vec,
        ],
        out_specs=out_specs,
        out_shape=out_shape,
        scratch_shapes=scratch,
        compiler_params=_params("arbitrary"),
        name="ln2_gather",
    )(idx, idx, f_sorted, x, g, b)


def _da_weight_layout(w_in):
    d = w_in.shape[0]
    w = DA_QK_WIDTH
    q1, q2, k1, k2 = (w_in[:, i * w:(i + 1) * w].reshape(d, DA_HEADS, DA_HEAD_DIM) for i in range(4))
    qk = jnp.concatenate([jnp.concatenate([q1, q2], axis=2).reshape(d, 2 * w),
                          jnp.concatenate([k1, k2], axis=2).reshape(d, 2 * w)], axis=1)
    return qk, w_in[:, 4 * w:]


def _trunk(xs, da_w_in, da_w_out, da_lambda_q1, da_lambda_k1, da_lambda_q2, da_lambda_k2, da_subln_g,
           dl_w_in, dl_w_out, ln1_g, ln1_b, ln2_g, ln2_b,
           moe_router_group, moe_router_expert, moe_w_gate, moe_w_up, moe_w_down):
    d = xs[0].shape[-1]
    segs = []
    off = 0
    for x in xs:
        segs.append((off, x.shape[0], x.shape[1]))
        off += x.shape[0] * x.shape[1]
    t_total = off
    x = jnp.concatenate([v.reshape(-1, d) for v in xs], axis=0)
    xb = x.astype(BF16)

    max_len = max(s for _, _, s in segs)
    rope = _rope_tables(max_len)
    pos_blocks = np.concatenate([
        np.tile(np.arange(s // PROJ_TM, dtype=np.int32), b) for _, b, s in segs])
    pos_blocks = jnp.asarray(pos_blocks)

    for i in range(DEPTH):
        j = i // N_MIXERS
        if i % N_MIXERS == 0:
            lambda_init = 0.8 - 0.6 * math.exp(-0.3 * i)
            w_qk, w_v = _da_weight_layout(da_w_in[j])
            qk = _proj(xb, w_qk.astype(BF16), rope, pos_blocks,
                       q_cols=2 * DA_QK_WIDTH, rope_cols=4 * DA_QK_WIDTH, q_scale=QK_SCALE_LOG2)
            vt = _proj_t(xb, w_v.T.astype(BF16))
            lam_params = jnp.stack([da_lambda_q1[j], da_lambda_k1[j], da_lambda_q2[j], da_lambda_k2[j]]).astype(F32)
            g_col = da_subln_g[j].astype(F32).reshape(DA_V_DIM, 1)
            os_ = [[_da_attention(qk, vt, lam_params, g_col, lambda_init, ro, b, s) for ro, b, s in segs]]
            lses = []
            w_out = da_w_out[j]
        else:
            os_, lses = [], []
            gw = 3 * DIL_GROUP_WIDTH
            for g, (window, dilation) in enumerate(DIL_PATTERNS):
                assert window // (2 * dilation) == DIL_HALF
                proj = _proj(xb, dl_w_in[j][:, g * gw:(g + 1) * gw].astype(BF16), rope, pos_blocks,
                             q_cols=DIL_GROUP_WIDTH, rope_cols=2 * DIL_GROUP_WIDTH, q_scale=QK_SCALE_LOG2,
                             dilation=dilation)
                pairs = [_dil_attention_group(proj, g, dilation, ro, b, s) for ro, b, s in segs]
                os_.append([p[0] for p in pairs])
                lses.append([p[1] for p in pairs])
            w_out = dl_w_out[j]
        w_router = jnp.concatenate([
            moe_router_group[i].astype(F32),
            moe_router_expert[i].reshape(d, N_EXPERTS).astype(F32),
            jnp.zeros((d, LANES - MOE_GROUPS - N_EXPERTS), F32)], axis=1)
        xc = _post_attention(os_, lses, [b * s for _, b, s in segs], w_out.astype(BF16), x,
                             ln1_g[i].reshape(1, d).astype(F32), ln1_b[i].reshape(1, d).astype(F32), w_router)
        gid = xc[:, d + ROUTER_GROUP_LANE].astype(jnp.int32)
        pos, src, tile_group, n_valid = _sort_by_group(gid, t_total, t_total + MOE_GROUPS * MOE_TM)
        f_sorted = _moe_ffn(xc, src, tile_group, n_valid,
                            moe_w_gate[i].astype(BF16), moe_w_up[i].astype(BF16), moe_w_down[i].astype(BF16))
        ln2 = (ln2_g[i].reshape(1, d).astype(F32), ln2_b[i].reshape(1, d).astype(F32))
        if i + 1 < DEPTH:
            x, xb = _ln2_gather(f_sorted, pos, xc, *ln2)
        else:
            ys = _ln2_gather(f_sorted, pos, xc, *ln2, seg_rows=[b * s for _, b, s in segs])
    return [y.reshape(v.shape) for y, v in zip(ys, xs)]


def kernel(x_prompt, x_sample, da_w_in, da_w_out, da_lambda_q1, da_lambda_k1, da_lambda_q2, da_lambda_k2, da_subln_g, dl_w_in, dl_w_out, ln1_g, ln1_b, ln2_g, ln2_b, moe_router_group, moe_router_expert, moe_w_gate, moe_w_up, moe_w_down):
    y_prompt, y_sample = _trunk(
        [x_prompt, x_sample], da_w_in, da_w_out, da_lambda_q1, da_lambda_k1, da_lambda_q2, da_lambda_k2,
        da_subln_g, dl_w_in, dl_w_out, ln1_g, ln1_b, ln2_g, ln2_b,
        moe_router_group, moe_router_expert, moe_w_gate, moe_w_up, moe_w_down)
    return (y_prompt, y_sample)
```

```python
import functools
import math

import jax
import jax.numpy as jnp
import numpy as np
from jax import lax
from jax.experimental import pallas as pl
from jax.experimental.pallas import tpu as pltpu

F32 = jnp.float32
BF16 = jnp.bfloat16

LANES = 128
VMEM_LIMIT_BYTES = 56 * 1024 * 1024

D_MODEL = 1024
DEPTH = 4
N_MIXERS = 2

DA_HEADS = 8
DA_HEAD_DIM = 64
DA_V_DIM = 2 * DA_HEAD_DIM
DA_QK_WIDTH = DA_HEADS * DA_HEAD_DIM

DIL_PATTERNS = ((128, 1), (512, 4), (2048, 16))
DIL_GROUPS = len(DIL_PATTERNS)
DIL_HEADS = 16
DIL_HEAD_DIM = 64
DIL_GROUP_WIDTH = DIL_HEADS * DIL_HEAD_DIM
DIL_HALF = 64

ROPE_THETA = 500000.0
ROPE_ROT = DA_HEAD_DIM // 4
ROPE_HALF = ROPE_ROT // 2

LN_EPS = 1e-5
DEEPNORM_ALPHA = (2.0 * DEPTH) ** 0.25

MOE_GROUPS = 4
MOE_EXPERTS = 4
MOE_FF = 512
N_EXPERTS = MOE_GROUPS * MOE_EXPERTS

NEG_INF = -1e30
LOG2_E = math.log2(math.e)
QK_SCALE_LOG2 = DA_HEAD_DIM ** -0.5 * LOG2_E


def _params(*sem):
    return pltpu.CompilerParams(dimension_semantics=sem, vmem_limit_bytes=VMEM_LIMIT_BYTES)


PROJ_TM = 1024
PROJ_TN = 1024


def _rope_tables(max_len):
    inv = ROPE_THETA ** (-jnp.arange(ROPE_HALF, dtype=F32) * (2.0 / ROPE_ROT))
    ang = jnp.arange(max_len, dtype=F32)[:, None] * inv[None, :]
    cos, sin = jnp.cos(ang), jnp.sin(ang)
    ones = jnp.ones((max_len, DA_HEAD_DIM - ROPE_ROT), F32)
    zeros_h = jnp.zeros((max_len, ROPE_HALF), F32)
    zeros_r = jnp.zeros((max_len, DA_HEAD_DIM - ROPE_ROT), F32)
    c = jnp.concatenate([cos, cos, ones], axis=1)
    s_up = jnp.concatenate([zeros_h, sin, zeros_r], axis=1)
    s_dn = jnp.concatenate([-sin, zeros_h, zeros_r], axis=1)
    reps = LANES // DA_HEAD_DIM
    return jnp.stack([jnp.tile(c, (1, reps)), jnp.tile(s_up, (1, reps)), jnp.tile(s_dn, (1, reps))])


PROJ_MAX_STRIDE = 4
PROJ_CHUNK = 256


def _proj_kernel(pos_ref, x_ref, w_ref, rope_ref, o_ref, ybuf_ref, zbuf_ref, *, n_cols, q_cols, rope_cols, q_scale, dilation):
    del pos_ref
    x = x_ref[...]
    c = rope_ref[0]
    s_up = rope_ref[1]
    s_dn = rope_ref[2]
    rows = x.shape[0] // dilation
    for k in range(n_cols // PROJ_CHUNK):
        c0 = k * PROJ_CHUNK
        y = jnp.dot(x, w_ref[:, c0:c0 + PROJ_CHUNK], preferred_element_type=F32)
        if c0 < q_cols:
            y = y * q_scale
        parts = []
        for h in range(PROJ_CHUNK // LANES):
            yc = y[:, h * LANES:(h + 1) * LANES]
            if c0 < rope_cols:
                yc = yc * c + pltpu.roll(yc, ROPE_HALF, 1) * s_up + pltpu.roll(yc, LANES - ROPE_HALF, 1) * s_dn
            parts.append(yc)
        if dilation == 1:
            for h, yc in enumerate(parts):
                o_ref[0, :, c0 + h * LANES:c0 + (h + 1) * LANES] = yc.astype(o_ref.dtype)
        else:
            for h, yc in enumerate(parts):
                slot = (2 * k + h) % ybuf_ref.shape[0]
                cols = slice(c0 + h * LANES, c0 + (h + 1) * LANES)
                ybuf_ref[slot] = yc
                if dilation <= PROJ_MAX_STRIDE:
                    for r in range(dilation):
                        o_ref[r, :, cols] = ybuf_ref[slot, pl.ds(r, rows, stride=dilation), :].astype(o_ref.dtype)
                else:
                    outer = dilation // PROJ_MAX_STRIDE
                    zslot = h % zbuf_ref.shape[0]
                    for r in range(PROJ_MAX_STRIDE):
                        zbuf_ref[zslot, r] = ybuf_ref[slot, pl.ds(r, rows * outer, stride=PROJ_MAX_STRIDE), :]
                        for a in range(outer):
                            o_ref[r + PROJ_MAX_STRIDE * a, :, cols] = (
                                zbuf_ref[zslot, r, pl.ds(a, rows, stride=outer), :].astype(o_ref.dtype))


def _proj(xb, w, rope, pos_blocks, *, q_cols, rope_cols, q_scale, dilation=1):
    t, d = xb.shape
    n = w.shape[1]
    rows = PROJ_TM // dilation
    grid_spec = pltpu.PrefetchScalarGridSpec(
        num_scalar_prefetch=1,
        grid=(t // PROJ_TM,),
        in_specs=[
            pl.BlockSpec((PROJ_TM, d), lambda i, pos: (i, 0)),
            pl.BlockSpec((d, n), lambda i, pos: (0, 0)),
            pl.BlockSpec((3, PROJ_TM, LANES), lambda i, pos: (0, pos[i], 0)),
        ],
        out_specs=pl.BlockSpec((dilation, rows, n), lambda i, pos: (0, i, 0)),
        scratch_shapes=[pltpu.VMEM((4, PROJ_TM, LANES), F32),
                        pltpu.VMEM((2, PROJ_MAX_STRIDE, PROJ_TM // PROJ_MAX_STRIDE, LANES), F32)],
    )
    return pl.pallas_call(
        functools.partial(_proj_kernel, n_cols=n, q_cols=q_cols, rope_cols=rope_cols, q_scale=q_scale,
                          dilation=dilation),
        grid_spec=grid_spec,
        out_shape=jax.ShapeDtypeStruct((dilation, t // dilation, n), BF16),
        compiler_params=_params("parallel"),
        name="proj_rope",
    )(pos_blocks, xb, w, rope)


DA_TK = 512
DA_ONES_ROWS = 16
DA_VT_ROWS = DA_V_DIM + DA_ONES_ROWS


def _projt_kernel(x_ref, wt_ref, o_ref):
    y = lax.dot_general(wt_ref[...], x_ref[...], (((1,), (1,)), ((), ())), preferred_element_type=F32)
    h = o_ref.shape[0]
    o_ref[:, 0, 0:DA_V_DIM, :] = y.reshape(h, DA_V_DIM, DA_TK).astype(o_ref.dtype)
    o_ref[:, 0, DA_V_DIM:DA_VT_ROWS, :] = jnp.ones((h, DA_ONES_ROWS, DA_TK), o_ref.dtype)


def _proj_t(xb, wt):
    t, d = xb.shape
    n = wt.shape[0]
    h = n // DA_V_DIM
    return pl.pallas_call(
        _projt_kernel,
        grid=(t // DA_TK,),
        in_specs=[
            pl.BlockSpec((DA_TK, d), lambda i: (i, 0)),
            pl.BlockSpec((n, d), lambda i: (0, 0)),
        ],
        out_specs=pl.BlockSpec((h, 1, DA_VT_ROWS, DA_TK), lambda i: (0, i, 0, 0)),
        out_shape=jax.ShapeDtypeStruct((h, t // DA_TK, DA_VT_ROWS, DA_TK), BF16),
        compiler_params=_params("parallel"),
        name="proj_vt",
    )(xb, wt)


DA_TQ = 512
DA_PAIR_UNROLL = 3
DA_HEADS_PER_STEP = 2


def _da_attn_kernel(lam_ref, g_ref, q_ref, *rest, n_kblk, lambda_init):
    for hs in range(q_ref.shape[1] // LANES):
        _da_one_head(lam_ref, g_ref, q_ref, *rest, hs=hs, n_kblk=n_kblk, lambda_init=lambda_init)


def _da_one_head(lam_ref, g_ref, q_ref, k_ref, vt_ref, o_ref, q1_ref, q2_ref,
                 sa1_ref, sa2_ref, sb1_ref, sb2_ref, xa1_ref, xa2_ref, xb1_ref, xb2_ref,
                 m1_ref, a1_ref, m2_ref, a2_ref, *, hs, n_kblk, lambda_init):
    cols = slice(hs * LANES, (hs + 1) * LANES)
    buf_a = ((sa1_ref, xa1_ref), (sa2_ref, xa2_ref))
    buf_b = ((sb1_ref, xb1_ref), (sb2_ref, xb2_ref))
    qt = q_ref[:, cols].astype(F32).T
    row = lax.broadcasted_iota(jnp.int32, qt.shape, 0)
    q1_ref[...] = jnp.where(row < DA_HEAD_DIM, qt, 0.0).astype(q1_ref.dtype)
    q2_ref[...] = jnp.where(row >= DA_HEAD_DIM, qt, 0.0).astype(q2_ref.dtype)
    for m_ref, a_ref in ((m1_ref, a1_ref), (m2_ref, a2_ref)):
        m_ref[...] = jnp.full(m_ref.shape, NEG_INF, F32)
        a_ref[...] = jnp.zeros(a_ref.shape, F32)

    def scores(j, buf):
        k = k_ref[pl.ds(pl.multiple_of(j * DA_TK, DA_TK), DA_TK), cols]
        for qm_ref, (s_ref, x_ref) in zip((q1_ref, q2_ref), buf):
            s = jnp.dot(k, qm_ref[...], preferred_element_type=F32)
            s_ref[...] = s
            x_ref[...] = jnp.max(s.reshape(DA_TK // 8, 8, DA_TQ), axis=0)

    def softmax_pv(j, buf):
        vt = vt_ref[hs, j]
        for (s_ref, x_ref), m_ref, a_ref in zip(buf, (m1_ref, m2_ref), (a1_ref, a2_ref)):
            m_old = m_ref[...]
            m_new = jnp.maximum(m_old, jnp.max(x_ref[...], axis=0, keepdims=True))
            p = jnp.exp2(s_ref[...] - m_new).astype(vt.dtype)
            a_ref[...] = jnp.exp2(m_old - m_new) * a_ref[...] + jnp.dot(vt, p, preferred_element_type=F32)
            m_ref[...] = m_new

    scores(0, buf_a)

    def pair(jj, carry):
        j = 2 * jj
        scores(j + 1, buf_b)
        softmax_pv(j, buf_a)
        scores(j + 2, buf_a)
        softmax_pv(j + 1, buf_b)
        return carry

    n_pairs = n_kblk // 2 - 1
    lax.fori_loop(0, n_pairs, pair, 0, unroll=DA_PAIR_UNROLL if n_pairs > DA_PAIR_UNROLL else 2)
    scores(n_kblk - 1, buf_b)
    softmax_pv(n_kblk - 2, buf_a)
    softmax_pv(n_kblk - 1, buf_b)

    lp = lam_ref[...]
    lam = (jnp.exp(jnp.sum(lp[0:1] * lp[1:2], axis=1, keepdims=True))
           - jnp.exp(jnp.sum(lp[2:3] * lp[3:4], axis=1, keepdims=True)) + lambda_init)
    a1 = a1_ref[...]
    a2 = a2_ref[...]
    o = (a1[0:DA_V_DIM] * (1.0 / a1[DA_V_DIM:DA_V_DIM + 1])
         - lam * (a2[0:DA_V_DIM] * (1.0 / a2[DA_V_DIM:DA_V_DIM + 1])))
    o = o * lax.rsqrt(jnp.mean(jnp.square(o), axis=0, keepdims=True) + LN_EPS)
    o = o * g_ref[...] * (1.0 - lambda_init)
    o_ref[:, cols] = o.T.astype(o_ref.dtype)


def _da_attention(qk, vt, lam_params, subln_g, lambda_init, row_off, batch, seq):
    h = DA_HEADS
    nq = seq // DA_TQ
    nk = seq // DA_TK
    assert nk % 2 == 0 and nk >= 2
    qoff = row_off // DA_TQ
    soff = row_off // seq
    s_buf = pltpu.VMEM((DA_TK, DA_TQ), F32)
    x_buf = pltpu.VMEM((8, DA_TQ), F32)
    hs = DA_HEADS_PER_STEP
    width = hs * LANES
    in_specs = [
        pl.BlockSpec((4, DA_HEAD_DIM), lambda b, hh, i: (0, 0)),
        pl.BlockSpec((DA_V_DIM, 1), lambda b, hh, i: (0, 0)),
        pl.BlockSpec((None, DA_TQ, width), lambda b, hh, i: (0, qoff + b * nq + i, hh)),
        pl.BlockSpec((None, seq, width), lambda b, hh, i: (0, soff + b, h // hs + hh)),
        pl.BlockSpec((hs, nk, DA_VT_ROWS, DA_TK), lambda b, hh, i: (hh, soff + b, 0, 0)),
    ]
    return pl.pallas_call(
        functools.partial(_da_attn_kernel, n_kblk=nk, lambda_init=lambda_init),
        grid=(batch, h // hs, nq),
        in_specs=in_specs,
        out_specs=pl.BlockSpec((None, DA_TQ, width), lambda b, hh, i: (0, b * nq + i, hh)),
        out_shape=jax.ShapeDtypeStruct((1, batch * seq, h * DA_V_DIM), BF16),
        scratch_shapes=[
            pltpu.VMEM((LANES, DA_TQ), BF16), pltpu.VMEM((LANES, DA_TQ), BF16),
            s_buf, s_buf, s_buf, s_buf,
            x_buf, x_buf, x_buf, x_buf,
            pltpu.VMEM((1, DA_TQ), F32), pltpu.VMEM((DA_VT_ROWS, DA_TQ), F32),
            pltpu.VMEM((1, DA_TQ), F32), pltpu.VMEM((DA_VT_ROWS, DA_TQ), F32),
        ],
        compiler_params=_params("parallel", "parallel", "arbitrary"),
        name="da_attn",
    )(lam_params, subln_g, qk, qk, vt)


DIL_QSUB = 128
DIL_KWIN = DIL_QSUB + 2 * DIL_HALF
DIL_QBLK = 1024
DIL_SUBS_PER_STEP = 32


def _dil_attn_kernel(q_ref, k_ref, v_ref, o_ref, lse_ref, *, sub_len, qblk):
    qi = pl.program_id(3)
    lane = lax.broadcasted_iota(jnp.int32, (DIL_QSUB, LANES), 1)
    lo = lane < DIL_HEAD_DIM
    row = lax.broadcasted_iota(jnp.int32, (DIL_QSUB, DIL_KWIN), 0)
    col = lax.broadcasted_iota(jnp.int32, (DIL_QSUB, DIL_KWIN), 1)

    interior = jnp.abs(col - DIL_HALF - row) <= DIL_HALF
    n_sub = qblk // DIL_QSUB

    for pair_idx, n in [(a, b) for a in range(q_ref.shape[1] // LANES) for b in range(n_sub)]:
        cols = slice(pair_idx * LANES, (pair_idx + 1) * LANES)
        q0 = n * DIL_QSUB
        gq0 = qi * qblk + q0
        ws = jnp.clip(gq0 - DIL_HALF, 0, sub_len - DIL_KWIN)
        ws = pl.multiple_of(ws, DIL_HALF)
        qb = q_ref[pl.ds(q0, DIL_QSUB), cols]
        kw = k_ref[pl.ds(ws, DIL_KWIN), cols]
        vw = v_ref[pl.ds(ws, DIL_KWIN), cols]
        if 0 < n < n_sub - 1:
            valid = interior
        else:
            valid = jnp.abs((ws + col) - (gq0 + row)) <= DIL_HALF
        zero = jnp.zeros_like(qb)
        outs, lses = [], []
        for qm in (jnp.where(lo, qb, zero), jnp.where(lo, zero, qb)):
            s = lax.dot_general(qm, kw, (((1,), (1,)), ((), ())), preferred_element_type=F32)
            s = jnp.where(valid, s, NEG_INF)
            m = jnp.max(s, axis=1, keepdims=True)
            p = jnp.exp2(s - m)
            l = jnp.sum(p, axis=1, keepdims=True)
            o = jnp.dot(p.astype(vw.dtype), vw, preferred_element_type=F32) * (1.0 / l)
            outs.append(o)
            lses.append(jnp.broadcast_to(m + jnp.log(l) * LOG2_E, (DIL_QSUB, LANES)))
        o_ref[pl.ds(q0, DIL_QSUB), cols] = jnp.where(lo, outs[0], outs[1]).astype(o_ref.dtype)
        lse_ref[pl.ds(q0, DIL_QSUB), cols] = jnp.where(lo, lses[0], lses[1])


def _dil_attention_group(proj, g, dilation, row_off, batch, seq):
    sub_len = seq // dilation
    qblk = min(DIL_QBLK, sub_len)
    nqb = sub_len // qblk
    n_pairs = DIL_GROUP_WIDTH // LANES
    per_step = min(n_pairs, max(1, DIL_SUBS_PER_STEP // (qblk // DIL_QSUB)))
    hp = n_pairs // per_step
    width = per_step * LANES
    soff = row_off // seq

    def qmap(b, r, p, i):
        return (r, (soff + b) * nqb + i, p)

    def kmap(b, r, p, i):
        return (r, soff + b, hp + p)

    def vmap(b, r, p, i):
        return (r, soff + b, 2 * hp + p)

    def omap(b, r, p, i):
        return (r, b * nqb + i, p)

    return pl.pallas_call(
        functools.partial(_dil_attn_kernel, sub_len=sub_len, qblk=qblk),
        grid=(batch, dilation, hp, nqb),
        in_specs=[
            pl.BlockSpec((None, qblk, width), qmap),
            pl.BlockSpec((None, sub_len, width), kmap),
            pl.BlockSpec((None, sub_len, width), vmap),
        ],
        out_specs=[
            pl.BlockSpec((None, qblk, width), omap),
            pl.BlockSpec((None, qblk, width), omap),
        ],
        out_shape=[
            jax.ShapeDtypeStruct((dilation, batch * sub_len, DIL_GROUP_WIDTH), BF16),
            jax.ShapeDtypeStruct((dilation, batch * sub_len, DIL_GROUP_WIDTH), F32),
        ],
        compiler_params=_params("parallel", "parallel", "parallel", "arbitrary"),
        name=f"dil_attn_g{g}",
    )(proj, proj, proj)


POST_TM = 256


def _layer_norm_rows(z, g, b):
    mu = jnp.mean(z, axis=-1, keepdims=True)
    zc = z - mu
    var = jnp.mean(jnp.square(zc), axis=-1, keepdims=True)
    return zc * lax.rsqrt(var + LN_EPS) * g + b


def _token_order_reader(seg_refs, seg_ends, scratch_ref):
    i = pl.program_id(0)
    dil, rows, width = seg_refs[0].shape
    if dil == 1:
        def read(c):
            cs = slice(c * LANES, (c + 1) * LANES)
            val = seg_refs[-1][0, :, cs]
            for ref, end in reversed(list(zip(seg_refs[:-1], seg_ends[:-1]))):
                val = jnp.where(i < end, ref[0, :, cs], val)
            return val.astype(F32)
        return read
    start = 0
    for ref, end in zip(seg_refs, seg_ends):
        @pl.when((i >= start) & (i < end))
        def _(ref=ref):
            for r in range(dil):
                for c in range(width // LANES):
                    scratch_ref[c, pl.ds(r, rows, stride=dil), :] = (
                        ref[r, :, c * LANES:(c + 1) * LANES].astype(F32))
        start = end
    return lambda c: scratch_ref[c]


def _post_kernel(*refs, n_groups, seg_ends):
    n_seg = len(seg_ends)
    o_refs = [refs[g * n_seg:(g + 1) * n_seg] for g in range(n_groups)]
    n_in = n_groups * n_seg
    lse_refs = []
    if n_groups > 1:
        lse_refs = [refs[n_in + g * n_seg:n_in + (g + 1) * n_seg] for g in range(n_groups)]
        n_in *= 2
    w_ref, x_ref, g_ref, b_ref, wr_ref, xc_ref = refs[n_in:n_in + 6]
    scratch = refs[n_in + 6:]
    d = x_ref.shape[1]
    o_scr = scratch[-1]
    spare = iter(scratch[:-1])
    readers = [_token_order_reader(r, seg_ends, next(spare) if r[0].shape[0] > 1 else None)
               for r in (*o_refs, *lse_refs)]
    read_o, read_lse = readers[:n_groups], readers[n_groups:]
    for c in range(d // LANES):
        if n_groups == 1:
            o_c = read_o[0](c)
        else:
            lses = [rd(c) for rd in read_lse]
            m = functools.reduce(jnp.maximum, lses)
            es = [jnp.exp2(l - m) for l in lses]
            inv = 1.0 / functools.reduce(jnp.add, es)
            o_c = functools.reduce(jnp.add, [(e * inv) * rd(c) for e, rd in zip(es, read_o)])
        o_scr[:, c * LANES:(c + 1) * LANES] = o_c.astype(BF16)
    h = jnp.dot(o_scr[...], w_ref[...], preferred_element_type=F32)
    y = _layer_norm_rows(DEEPNORM_ALPHA * x_ref[...] + h, g_ref[...], b_ref[...])
    xc_ref[:, 0:d] = y
    xc_ref[:, d:d + LANES] = _routing_lanes(y, wr_ref[...])


def _post_attention(os_, lses, seg_rows, w_out, x, g, b, w_router):
    t, d = x.shape
    n_groups = len(os_)
    tm = POST_TM
    seg_tiles = [r // tm for r in seg_rows]
    seg_ends = tuple(int(v) for v in np.cumsum(seg_tiles))
    row = pl.BlockSpec((tm, d), lambda i: (i, 0))
    full = pl.BlockSpec((d, d), lambda i: (0, 0))
    vec = pl.BlockSpec((1, d), lambda i: (0, 0))

    def seg_spec(a, s):
        first, count = seg_ends[s] - seg_tiles[s], seg_tiles[s]
        return pl.BlockSpec((a.shape[0], tm // a.shape[0], d),
                            lambda i: (0, jnp.clip(i - first, 0, count - 1), 0))

    arrays = [a for grp in (*os_, *lses) for a in grp]
    specs = [seg_spec(a, s) for grp in (*os_, *lses) for s, a in enumerate(grp)]
    n_scratch = sum(1 for grp in (*os_, *lses) if grp[0].shape[0] > 1)
    return pl.pallas_call(
        functools.partial(_post_kernel, n_groups=n_groups, seg_ends=seg_ends),
        grid=(t // tm,),
        in_specs=specs + [full, row, vec, vec, pl.BlockSpec((d, LANES), lambda i: (0, 0))],
        out_specs=pl.BlockSpec((tm, d + LANES), lambda i: (i, 0)),
        out_shape=jax.ShapeDtypeStruct((t, d + LANES), F32),
        scratch_shapes=[pltpu.VMEM((d // LANES, tm, LANES), F32)] * n_scratch + [pltpu.VMEM((tm, d), BF16)],
        compiler_params=_params("parallel"),
        name="post_attn_ln",
    )(*arrays, w_out, x, g, b, w_router)


ROUTER_TM = 512
ROUTER_EXPERT_LANE0 = MOE_GROUPS
ROUTER_GROUP_LANE = 0


def _first_argmax(vals, vmax, lane_f):
    return jnp.min(jnp.where(vals == vmax, lane_f, float(LANES)), axis=1, keepdims=True)


def _routing_lanes(x, w_router):
    xh = x.astype(BF16)
    xl = (x - xh.astype(F32)).astype(BF16)
    wh = w_router.astype(BF16)
    wl = (w_router - wh.astype(F32)).astype(BF16)
    logits = (jnp.dot(xh, wh, preferred_element_type=F32) + jnp.dot(xh, wl, preferred_element_type=F32)
              + jnp.dot(xl, wh, preferred_element_type=F32))
    lane_f = lax.broadcasted_iota(jnp.int32, logits.shape, 1).astype(F32)
    gl = jnp.where(lane_f < MOE_GROUPS, logits, NEG_INF)
    gmax = jnp.max(gl, axis=1, keepdims=True)
    gw = 1.0 / jnp.sum(jnp.exp(gl - gmax), axis=1, keepdims=True)
    gsel = _first_argmax(gl, gmax, lane_f)
    e0 = ROUTER_EXPERT_LANE0 + MOE_EXPERTS * gsel
    el = jnp.where((lane_f >= e0) & (lane_f < e0 + MOE_EXPERTS), logits, NEG_INF)
    v1 = jnp.max(el, axis=1, keepdims=True)
    i1 = _first_argmax(el, v1, lane_f)
    el2 = jnp.where(lane_f == i1, NEG_INF, el)
    v2 = jnp.max(el2, axis=1, keepdims=True)
    i2 = _first_argmax(el2, v2, lane_f)
    e2 = jnp.exp(v2 - v1)
    inv = gw / (1.0 + e2)
    comb = jnp.where(lane_f == i1, inv, 0.0) + jnp.where(lane_f == i2, e2 * inv, 0.0)
    return comb + jnp.where(lane_f == ROUTER_GROUP_LANE, gsel, 0.0)


MOE_TM = 512
GATHER_UNROLL = 8


def _issue_row_gather(idx_ref, src_hbm, dst_ref, sem, n_rows):
    def body(r, carry):
        pltpu.make_async_copy(src_hbm.at[pl.ds(idx_ref[0, r], 1), :], dst_ref.at[pl.ds(r, 1), :], sem).start()
        return carry
    lax.fori_loop(0, n_rows, body, 0, unroll=GATHER_UNROLL)


def _wait_row_gather(src_hbm, dst_ref, sem, n_rows):
    pltpu.make_async_copy(src_hbm.at[pl.ds(0, n_rows), :], dst_ref, sem).wait()


def _sort_by_group(gid, t, n_pad_rows):
    groups = jnp.arange(MOE_GROUPS, dtype=jnp.int32)
    onehot = (gid[:, None] == groups[None, :]).astype(jnp.int32)
    csum = jnp.cumsum(onehot, axis=0)
    counts = csum[-1]
    rank = jnp.sum(onehot * csum, axis=1) - 1
    tiles_g = (counts + MOE_TM - 1) // MOE_TM
    tile_end = jnp.cumsum(tiles_g)
    row_start = (tile_end - tiles_g) * MOE_TM
    pos = jnp.sum(onehot * row_start[None, :], axis=1) + rank
    src = jnp.zeros((n_pad_rows,), jnp.int32).at[pos].set(jnp.arange(t, dtype=jnp.int32))
    tile_ids = jnp.arange(n_pad_rows // MOE_TM, dtype=jnp.int32)
    tile_group = jnp.minimum(jnp.sum((tile_ids[:, None] >= tile_end[None, :]).astype(jnp.int32), axis=1),
                             MOE_GROUPS - 1)
    return pos, src, tile_group, tile_end[-1:]


def _moe_ffn_kernel(tg_ref, nv_ref, idx_cur_ref, idx_nxt_ref, xc_hbm, wg_ref, wu_ref, wd_ref, f_ref,
                    xbuf_ref, sem_ref):
    i = pl.program_id(0)
    slot = i % 2
    n_valid = nv_ref[0]
    d = f_ref.shape[1]

    @pl.when(i == 0)
    def _():
        _issue_row_gather(idx_cur_ref, xc_hbm, xbuf_ref.at[0], sem_ref.at[0], MOE_TM)

    @pl.when(i + 1 < n_valid)
    def _():
        _issue_row_gather(idx_nxt_ref, xc_hbm, xbuf_ref.at[1 - slot], sem_ref.at[1 - slot], MOE_TM)

    @pl.when(i < n_valid)
    def _():
        _wait_row_gather(xc_hbm, xbuf_ref.at[slot], sem_ref.at[slot], MOE_TM)
        xb = xbuf_ref[slot, :, 0:d].astype(BF16)
        comb = xbuf_ref[slot, :, d:d + LANES]
        lane = lax.broadcasted_iota(jnp.int32, comb.shape, 1)
        lane0 = ROUTER_EXPERT_LANE0 + MOE_EXPERTS * tg_ref[i]
        acc = None
        for e in range(MOE_EXPERTS):
            ce = jnp.sum(jnp.where(lane == lane0 + e, comb, 0.0), axis=1, keepdims=True)
            gate = jnp.dot(xb, wg_ref[0, e], preferred_element_type=F32)
            up = jnp.dot(xb, wu_ref[0, e], preferred_element_type=F32)
            hid = (gate * jax.nn.sigmoid(gate)) * up * ce
            part = jnp.dot(hid.astype(BF16), wd_ref[0, e], preferred_element_type=F32)
            acc = part if acc is None else acc + part
        f_ref[...] = acc

    @pl.when(i >= n_valid)
    def _():
        f_ref[...] = jnp.zeros(f_ref.shape, F32)


def _moe_ffn(xc, src, tile_group, n_valid, w_gate, w_up, w_down):
    d = xc.shape[1] - LANES
    n_tiles = tile_group.shape[0]
    _, ne, _, f = w_gate.shape
    idx = src.reshape(n_tiles, 1, MOE_TM)
    grid_spec = pltpu.PrefetchScalarGridSpec(
        num_scalar_prefetch=2,
        grid=(n_tiles,),
        in_specs=[
            pl.BlockSpec((None, 1, MOE_TM), lambda i, tg, nv: (i, 0, 0), memory_space=pltpu.SMEM),
            pl.BlockSpec((None, 1, MOE_TM), lambda i, tg, nv: (jnp.minimum(i + 1, n_tiles - 1), 0, 0),
                         memory_space=pltpu.SMEM),
            pl.BlockSpec(memory_space=pl.ANY),
            pl.BlockSpec((1, ne, d, f), lambda i, tg, nv: (tg[i], 0, 0, 0)),
            pl.BlockSpec((1, ne, d, f), lambda i, tg, nv: (tg[i], 0, 0, 0)),
            pl.BlockSpec((1, ne, f, d), lambda i, tg, nv: (tg[i], 0, 0, 0)),
        ],
        out_specs=pl.BlockSpec((MOE_TM, d), lambda i, tg, nv: (i, 0)),
        scratch_shapes=[pltpu.VMEM((2, MOE_TM, d + LANES), F32), pltpu.SemaphoreType.DMA((2,))],
    )
    return pl.pallas_call(
        _moe_ffn_kernel,
        grid_spec=grid_spec,
        out_shape=jax.ShapeDtypeStruct((n_tiles * MOE_TM, d), F32),
        compiler_params=_params("arbitrary"),
        name="moe_ffn",
    )(tile_group, n_valid, idx, idx, xc, w_gate, w_up, w_down)


LN2_TM = 512


LN2_CHUNK = 64


def _ln2_kernel(idx_cur_ref, idx_nxt_ref, f_hbm, x_ref, g_ref, b_ref, *rest, n_tiles, seg_ends):
    i = pl.program_id(0)
    if seg_ends is None:
        y_ref, yb_ref, fbuf_ref, sem_ref = rest
    else:
        seg_refs = rest[:len(seg_ends)]
        fbuf_ref, sem_ref, y_ref = rest[len(seg_ends):]
        yb_ref = None

    @pl.when(i == 0)
    def _():
        _issue_row_gather(idx_cur_ref, f_hbm, fbuf_ref.at[0], sem_ref.at[0], LN2_TM)

    def tile(cur, nxt):
        _wait_row_gather(f_hbm, fbuf_ref.at[cur], sem_ref.at[cur], LN2_TM)
        g = g_ref[...]
        b = b_ref[...]
        for c in range(LN2_TM // LN2_CHUNK):
            rows = pl.ds(c * LN2_CHUNK, LN2_CHUNK)
            for r in range(c * LN2_CHUNK, (c + 1) * LN2_CHUNK):
                pltpu.make_async_copy(f_hbm.at[pl.ds(idx_nxt_ref[0, r], 1), :],
                                      fbuf_ref.at[nxt, pl.ds(r, 1), :], sem_ref.at[nxt]).start()
            y = _layer_norm_rows(DEEPNORM_ALPHA * x_ref[rows, :] + fbuf_ref[cur, rows, :], g, b)
            y_ref[rows, :] = y
            if yb_ref is not None:
                yb_ref[rows, :] = y.astype(BF16)

    @pl.when(i % 2 == 0)
    def _():
        tile(0, 1)

    @pl.when(i % 2 == 1)
    def _():
        tile(1, 0)

    if seg_ends is not None:
        start = 0
        for ref, end in zip(seg_refs, seg_ends):
            @pl.when((i >= start) & (i < end))
            def _(ref=ref):
                ref[...] = y_ref[...]
            start = end

    @pl.when(i == n_tiles - 1)
    def _():
        spare = n_tiles % 2
        _wait_row_gather(f_hbm, fbuf_ref.at[spare], sem_ref.at[spare], LN2_TM)


def _ln2_gather(f_sorted, pos, x, g, b, seg_rows=None):
    t = x.shape[0]
    d = f_sorted.shape[1]
    n_tiles = t // LN2_TM
    idx = pos.reshape(n_tiles, 1, LN2_TM)
    row = pl.BlockSpec((LN2_TM, d), lambda i: (i, 0))
    vec = pl.BlockSpec((1, d), lambda i: (0, 0))
    scratch = [pltpu.VMEM((2, LN2_TM, d), F32), pltpu.SemaphoreType.DMA((2,))]
    if seg_rows is None:
        seg_ends = None
        out_specs = [row, row]
        out_shape = [jax.ShapeDtypeStruct((t, d), F32), jax.ShapeDtypeStruct((t, d), BF16)]
    else:
        seg_tiles = [r // LN2_TM for r in seg_rows]
        seg_ends = tuple(int(v) for v in np.cumsum(seg_tiles))

        def seg_spec(s):
            first, count = seg_ends[s] - seg_tiles[s], seg_tiles[s]
            return pl.BlockSpec((LN2_TM, d), lambda i: (jnp.clip(i - first, 0, count - 1), 0))

        out_specs = [seg_spec(s) for s in range(len(seg_rows))]
        out_shape = [jax.ShapeDtypeStruct((r, d), F32) for r in seg_rows]
        scratch.append(pltpu.VMEM((LN2_TM, d), F32))
    return pl.pallas_call(
        functools.partial(_ln2_kernel, n_tiles=n_tiles, seg_ends=seg_ends),
        grid=(n_tiles,),
        in_specs=[
            pl.BlockSpec((None, 1, LN2_TM), lambda i: (i, 0, 0), memory_space=pltpu.SMEM),
            pl.BlockSpec((None, 1, LN2_TM), lambda i: (jnp.minimum(i + 1, n_tiles - 1), 0, 0),
                         memory_space=pltpu.SMEM),
            pl.BlockSpec(memory_space=pl.ANY),
            row, vec, vec,
        ],
        out_specs=out_specs,
        out_shape=out_shape,
        scratch_shapes=scratch,
        compiler_params=_params("arbitrary"),
        name="ln2_gather",
    )(idx, idx, f_sorted, x, g, b)


def _da_weight_layout(w_in):
    d = w_in.shape[0]
    w = DA_QK_WIDTH
    q1, q2, k1, k2 = (w_in[:, i * w:(i + 1) * w].reshape(d, DA_HEADS, DA_HEAD_DIM) for i in range(4))
    qk = jnp.concatenate([jnp.concatenate([q1, q2], axis=2).reshape(d, 2 * w),
                          jnp.concatenate([k1, k2], axis=2).reshape(d, 2 * w)], axis=1)
    return qk, w_in[:, 4 * w:]


def _trunk(xs, da_w_in, da_w_out, da_lambda_q1, da_lambda_k1, da_lambda_q2, da_lambda_k2, da_subln_g,
           dl_w_in, dl_w_out, ln1_g, ln1_b, ln2_g, ln2_b,
           moe_router_group, moe_router_expert, moe_w_gate, moe_w_up, moe_w_down):
    d = xs[0].shape[-1]
    segs = []
    off = 0
    for x in xs:
        segs.append((off, x.shape[0], x.shape[1]))
        off += x.shape[0] * x.shape[1]
    t_total = off
    x = jnp.concatenate([v.reshape(-1, d) for v in xs], axis=0)
    xb = x.astype(BF16)

    max_len = max(s for _, _, s in segs)
    rope = _rope_tables(max_len)
    pos_blocks = np.concatenate([
        np.tile(np.arange(s // PROJ_TM, dtype=np.int32), b) for _, b, s in segs])
    pos_blocks = jnp.asarray(pos_blocks)

    for i in range(DEPTH):
        j = i // N_MIXERS
        if i % N_MIXERS == 0:
            lambda_init = 0.8 - 0.6 * math.exp(-0.3 * i)
            w_qk, w_v = _da_weight_layout(da_w_in[j])
            qk = _proj(xb, w_qk.astype(BF16), rope, pos_blocks,
                       q_cols=2 * DA_QK_WIDTH, rope_cols=4 * DA_QK_WIDTH, q_scale=QK_SCALE_LOG2)
            vt = _proj_t(xb, w_v.T.astype(BF16))
            lam_params = jnp.stack([da_lambda_q1[j], da_lambda_k1[j], da_lambda_q2[j], da_lambda_k2[j]]).astype(F32)
            g_col = da_subln_g[j].astype(F32).reshape(DA_V_DIM, 1)
            os_ = [[_da_attention(qk, vt, lam_params, g_col, lambda_init, ro, b, s) for ro, b, s in segs]]
            lses = []
            w_out = da_w_out[j]
        else:
            os_, lses = [], []
            gw = 3 * DIL_GROUP_WIDTH
            for g, (window, dilation) in enumerate(DIL_PATTERNS):
                assert window // (2 * dilation) == DIL_HALF
                proj = _proj(xb, dl_w_in[j][:, g * gw:(g + 1) * gw].astype(BF16), rope, pos_blocks,
                             q_cols=DIL_GROUP_WIDTH, rope_cols=2 * DIL_GROUP_WIDTH, q_scale=QK_SCALE_LOG2,
                             dilation=dilation)
                pairs = [_dil_attention_group(proj, g, dilation, ro, b, s) for ro, b, s in segs]
                os_.append([p[0] for p in pairs])
                lses.append([p[1] for p in pairs])
            w_out = dl_w_out[j]
        w_router = jnp.concatenate([
            moe_router_group[i].astype(F32),
            moe_router_expert[i].reshape(d, N_EXPERTS).astype(F32),
            jnp.zeros((d, LANES - MOE_GROUPS - N_EXPERTS), F32)], axis=1)
        xc = _post_attention(os_, lses, [b * s for _, b, s in segs], w_out.astype(BF16), x,
                             ln1_g[i].reshape(1, d).astype(F32), ln1_b[i].reshape(1, d).astype(F32), w_router)
        gid = xc[:, d + ROUTER_GROUP_LANE].astype(jnp.int32)
        pos, src, tile_group, n_valid = _sort_by_group(gid, t_total, t_total + MOE_GROUPS * MOE_TM)
        f_sorted = _moe_ffn(xc, src, tile_group, n_valid,
                            moe_w_gate[i].astype(BF16), moe_w_up[i].astype(BF16), moe_w_down[i].astype(BF16))
        ln2 = (ln2_g[i].reshape(1, d).astype(F32), ln2_b[i].reshape(1, d).astype(F32))
        if i + 1 < DEPTH:
            x, xb = _ln2_gather(f_sorted, pos, xc, *ln2)
        else:
            ys = _ln2_gather(f_sorted, pos, xc, *ln2, seg_rows=[b * s for _, b, s in segs])
    return [y.reshape(v.shape) for y, v in zip(ys, xs)]


def kernel(x_prompt, x_sample, da_w_in, da_w_out, da_lambda_q1, da_lambda_k1, da_lambda_q2, da_lambda_k2, da_subln_g, dl_w_in, dl_w_out, ln1_g, ln1_b, ln2_g, ln2_b, moe_router_group, moe_router_expert, moe_w_gate, moe_w_up, moe_w_down):
    y_prompt, y_sample = _trunk(
        [x_prompt, x_sample], da_w_in, da_w_out, da_lambda_q1, da_lambda_k1, da_lambda_q2, da_lambda_k2,
        da_subln_g, dl_w_in, dl_w_out, ln1_g, ln1_b, ln2_g, ln2_b,
        moe_router_group, moe_router_expert, moe_w_gate, moe_w_up, moe_w_down)
    return (y_prompt, y_sample)
```
